```python
import jax, jax.numpy as jnp
from jax import lax
import numpy as np

D_MODEL = 2048
BATCH = 4
SEQ = 2048
DEPTH = 4
DEC_BATCH = 32
DEC_SEQ = 4
PAST_LEN = 16384
PAGE_SIZE = 128

A_HEADS = 16
A_KV_HEADS = 4
A_HEAD_DIM = 64
A_GROUP = A_HEADS // A_KV_HEADS
WINDOW = 128
ROT_DIM = A_HEAD_DIM // 4
ROPE_THETA = 500000.0
B_HEADS = 8
B_KEY_DIM = 128
B_VAL_DIM = 128
B_CHUNK = 16
C_GROUPS = 8
C_GROUP_DIM = 128
C_CHUNK = 128
A_Q_W = A_HEADS * A_HEAD_DIM
A_KV_W = A_KV_HEADS * A_HEAD_DIM
B_K_W = B_HEADS * B_KEY_DIM
B_V_W = B_HEADS * B_VAL_DIM
C_W = C_GROUPS * C_GROUP_DIM
N_BRANCH = 3
IN_SPLITS = (A_Q_W, A_KV_W, A_KV_W, B_K_W, B_K_W, B_V_W, B_V_W, C_W, C_W, N_BRANCH * D_MODEL)
N_IN = sum(IN_SPLITS)
FFN_DIM = ((8 * D_MODEL + 3 * 256 - 1) // (3 * 256)) * 256
EPS = 1e-6

kernel_name = "hybrid_swa_hgrn2_sgu_decoder_step"


def _rmsnorm(x, g):
    xf = x.astype(jnp.float32)
    y = xf * lax.rsqrt(jnp.mean(xf * xf, axis=-1, keepdims=True) + EPS)
    return (y * g.astype(jnp.float32)).astype(x.dtype)


def _partial_rope(x, p0):
    t = x.shape[1]
    half = ROT_DIM // 2
    pos = (p0 + jnp.arange(t, dtype=jnp.int32)).astype(jnp.float32)
    inv_freq = jnp.power(jnp.float32(ROPE_THETA), -jnp.arange(half, dtype=jnp.float32) / half)
    ang = pos[:, None] * inv_freq[None, :]
    cos = jnp.cos(ang)[None, :, None, :]
    sin = jnp.sin(ang)[None, :, None, :]
    xf = x.astype(jnp.float32)
    x1 = xf[..., :half]
    x2 = xf[..., half:ROT_DIM]
    out = jnp.concatenate([x1 * cos - x2 * sin, x2 * cos + x1 * sin, xf[..., ROT_DIM:]], axis=-1)
    return out.astype(x.dtype)


def _window_attention(q, k_all, v_all, p0, sinks):
    b, t = q.shape[:2]
    blk = min(WINDOW, t)
    nb = -(-t // blk)
    tp = nb * blk
    pad = ((0, 0), (0, tp - t), (0, 0), (0, 0))
    q = jnp.pad(q, pad)
    k_all = jnp.pad(k_all, pad)
    v_all = jnp.pad(v_all, pad)
    qb = q.reshape(b, nb, blk, A_KV_HEADS, A_GROUP, A_HEAD_DIM).astype(jnp.float32)
    blocks = jnp.arange(nb, dtype=jnp.int32)
    offs = jnp.arange(WINDOW + blk, dtype=jnp.int32)
    key_idx = blocks[:, None] * blk + offs[None, :]
    kb = jnp.take(k_all, key_idx, axis=1).astype(jnp.float32)
    vb = jnp.take(v_all, key_idx, axis=1).astype(jnp.float32)
    scores = jnp.einsum('bnqhgd,bnkhd->bnhgqk', qb, kb) * (A_HEAD_DIM ** -0.5)
    qi = jnp.arange(blk, dtype=jnp.int32)[:, None]
    key_pos = p0 - WINDOW + key_idx
    valid = (offs[None, :] > qi) & (offs[None, :] <= qi + WINDOW)
    valid = valid[None] & (key_pos[:, None, :] >= 0)
    scores = jnp.where(valid[None, :, None, None], scores, -jnp.inf)
    sink = sinks.astype(jnp.float32).reshape(1, 1, A_KV_HEADS, A_GROUP, 1, 1)
    m = jnp.maximum(jnp.max(scores, axis=-1, keepdims=True), sink)
    p = jnp.exp(scores - m)
    probs = p / (jnp.sum(p, axis=-1, keepdims=True) + jnp.exp(sink - m))
    o = jnp.einsum('bnhgqk,bnkhd->bnqhgd', probs, vb)
    return o.reshape(b, tp, A_Q_W)[:, :t]


def _hgrn2_chunked(q, log_f, v, s0):
    b, t = q.shape[:2]
    L = min(B_CHUNK, t)
    n = -(-t // L)
    tp = n * L
    pad = ((0, 0), (0, tp - t), (0, 0), (0, 0))
    q = jnp.pad(q.astype(jnp.float32), pad)
    log_f = jnp.pad(log_f.astype(jnp.float32), pad)
    v = jnp.pad(v.astype(jnp.float32), pad)
    k = -jnp.expm1(log_f)
    q, log_f, k, v = (a.reshape(b, n, L, B_HEADS, a.shape[-1]) for a in (q, log_f, k, v))
    cum = jnp.cumsum(log_f, axis=2)
    ref = cum[:, :, L // 2:L // 2 + 1]
    att = jnp.einsum('bnthk,bnshk->bnhts', q * jnp.exp(cum - ref), k * jnp.exp(ref - cum))
    causal = jnp.tril(jnp.ones((L, L), dtype=bool))
    att = jnp.where(causal, att, 0.0)
    o_intra = jnp.einsum('bnhts,bnshv->bnthv', att, v)
    q_inter = q * jnp.exp(cum)
    k_state = k * jnp.exp(cum[:, :, -1:] - cum)
    decay = jnp.exp(cum[:, :, -1])

    def step(S, xs):
        q_c, k_c, v_c, d_c = xs
        o_c = jnp.einsum('bthk,bhkv->bthv', q_c, S)
        S = d_c[..., None] * S + jnp.einsum('bshk,bshv->bhkv', k_c, v_c)
        return S, o_c

    xs = (jnp.swapaxes(q_inter, 0, 1), jnp.swapaxes(k_state, 0, 1),
          jnp.swapaxes(v, 0, 1), jnp.swapaxes(decay, 0, 1))
    s_fin, o_inter = lax.scan(step, s0.astype(jnp.float32), xs)
    o = o_intra + jnp.swapaxes(o_inter, 0, 1)
    return o.reshape(b, tp, B_HEADS, B_VAL_DIM)[:, :t], s_fin


def _chunk_sgu(u, v, w_s, b_s):
    b, t = u.shape[:2]
    L = min(C_CHUNK, t)
    n = -(-t // L)
    tp = n * L
    vb = jnp.pad(v, ((0, 0), (0, tp - t), (0, 0), (0, 0))).reshape(b, n, L, C_GROUPS, C_GROUP_DIM)
    causal = jnp.tril(jnp.ones((L, L), dtype=bool))
    w = jnp.where(causal[None], w_s[:, :L, :L], 0.0)
    z = jnp.einsum('gpq,bnqgd->bnpgd', w, vb) + b_s[:, :L].T[None, None, :, :, None]
    z = z.reshape(b, tp, C_GROUPS, C_GROUP_DIM)[:, :t]
    return u * z.astype(u.dtype)


def _layer(x, p0, k_buf, v_buf, s_hgrn, lb, norm_mix, w_in, q_norm, k_norm, sinks,
           hgrn_out_norm, sgu_v_norm, w_spatial, b_spatial, w_branch_a, w_branch_b,
           w_branch_c, w_out, norm_ffn, w_ffn_up, w_ffn_down):
    b, t = x.shape[:2]
    h = _rmsnorm(x, norm_mix)
    proj = h @ w_in
    offsets = [int(o) for o in np.cumsum(IN_SPLITS)[:-1]]
    qa, ka, va, qb, fb, ib, gb, uc, vc, gates = jnp.split(proj, offsets, axis=-1)
    qa = _partial_rope(_rmsnorm(qa.reshape(b, t, A_HEADS, A_HEAD_DIM), q_norm), p0)
    ka = _partial_rope(_rmsnorm(ka.reshape(b, t, A_KV_HEADS, A_HEAD_DIM), k_norm), p0)
    va = va.reshape(b, t, A_KV_HEADS, A_HEAD_DIM)
    k_all = jnp.concatenate([k_buf.astype(ka.dtype), ka], axis=1)
    v_all = jnp.concatenate([v_buf.astype(va.dtype), va], axis=1)
    out_a = _window_attention(qa, k_all, v_all, p0, sinks).astype(x.dtype)
    z_f = fb.reshape(b, t, B_HEADS, B_KEY_DIM).astype(jnp.float32)
    lb = lb.reshape(B_HEADS, B_KEY_DIM)
    log_f = jnp.logaddexp(jnp.log(lb), jnp.log1p(-lb) + jax.nn.log_sigmoid(z_f))
    o_b, s_new = _hgrn2_chunked(qb.reshape(b, t, B_HEADS, B_KEY_DIM), log_f,
                                ib.reshape(b, t, B_HEADS, B_VAL_DIM), s_hgrn)
    g_out = jax.nn.silu(gb.reshape(b, t, B_HEADS, B_VAL_DIM).astype(jnp.float32))
    out_b = (_rmsnorm(o_b, hgrn_out_norm) * g_out).reshape(b, t, B_V_W).astype(x.dtype)
    v_c = _rmsnorm(vc.reshape(b, t, C_GROUPS, C_GROUP_DIM), sgu_v_norm)
    out_c = _chunk_sgu(uc.reshape(b, t, C_GROUPS, C_GROUP_DIM), v_c, w_spatial, b_spatial).reshape(b, t, C_W)
    g_a, g_b, g_c = jnp.split(jax.nn.sigmoid(gates), 3, axis=-1)
    merged = g_a * (out_a @ w_branch_a) + g_b * (out_b @ w_branch_b) + g_c * (out_c @ w_branch_c)
    x = x + merged @ w_out
    h2 = _rmsnorm(x, norm_ffn)
    gate, up = jnp.split(h2 @ w_ffn_up, 2, axis=-1)
    x = x + (jax.nn.silu(gate) * up) @ w_ffn_down
    return x, k_all[:, -WINDOW:], v_all[:, -WINDOW:], s_new.astype(s_hgrn.dtype), v_c


def setup_inputs(seed: int = 0) -> dict:
    key = jax.random.key(seed)
    ks = jax.random.split(key, 24)
    f32 = jnp.float32

    def nrm(k, shape, scale):
        return jax.random.normal(k, shape, f32) * scale

    return {
        "x_prompt": nrm(ks[0], (BATCH, SEQ, D_MODEL), 1.0),
        "x_sample": nrm(ks[1], (DEC_BATCH, DEC_SEQ, D_MODEL), 1.0),
        "cache_k": nrm(ks[2], (DEPTH, DEC_BATCH, WINDOW, A_KV_HEADS, A_HEAD_DIM), 1.0),
        "cache_v": nrm(ks[3], (DEPTH, DEC_BATCH, WINDOW, A_KV_HEADS, A_HEAD_DIM), 1.0),
        "state_hgrn": nrm(ks[4], (DEPTH, DEC_BATCH, B_HEADS, B_KEY_DIM, B_VAL_DIM), 0.5),
        "norm_mix": 1.0 + nrm(ks[5], (DEPTH, D_MODEL), 0.1),
        "w_in": nrm(ks[6], (DEPTH, D_MODEL, N_IN), D_MODEL ** -0.5),
        "q_norm": 1.0 + nrm(ks[7], (DEPTH, A_HEAD_DIM), 0.1),
        "k_norm": 1.0 + nrm(ks[8], (DEPTH, A_HEAD_DIM), 0.1),
        "sinks": nrm(ks[9], (DEPTH, A_HEADS), 0.5),
        "lb_logits": nrm(ks[10], (DEPTH, B_K_W), 0.5),
        "hgrn_out_norm": 1.0 + nrm(ks[11], (DEPTH, B_VAL_DIM), 0.1),
        "sgu_v_norm": 1.0 + nrm(ks[12], (DEPTH, C_GROUP_DIM), 0.1),
        "w_spatial": nrm(ks[13], (DEPTH, C_GROUPS, C_CHUNK, C_CHUNK), C_CHUNK ** -0.5),
        "b_spatial": 1.0 + nrm(ks[14], (DEPTH, C_GROUPS, C_CHUNK), 0.1),
        "w_branch_a": nrm(ks[15], (DEPTH, A_Q_W, D_MODEL), A_Q_W ** -0.5),
        "w_branch_b": nrm(ks[16], (DEPTH, B_V_W, D_MODEL), B_V_W ** -0.5),
        "w_branch_c": nrm(ks[17], (DEPTH, C_W, D_MODEL), C_W ** -0.5),
        "w_out": nrm(ks[18], (DEPTH, D_MODEL, D_MODEL), D_MODEL ** -0.5),
        "norm_ffn": 1.0 + nrm(ks[19], (DEPTH, D_MODEL), 0.1),
        "w_ffn_up": nrm(ks[20], (DEPTH, D_MODEL, 2 * FFN_DIM), D_MODEL ** -0.5),
        "w_ffn_down": nrm(ks[21], (DEPTH, FFN_DIM, D_MODEL), FFN_DIM ** -0.5),
    }


def reference(x_prompt, x_sample, cache_k, cache_v, state_hgrn, norm_mix, w_in, q_norm, k_norm,
              sinks, lb_logits, hgrn_out_norm, sgu_v_norm, w_spatial, b_spatial, w_branch_a,
              w_branch_b, w_branch_c, w_out, norm_ffn, w_ffn_up, w_ffn_down):
    sm = jax.nn.softmax(lb_logits.astype(jnp.float32), axis=0)
    cs = jnp.cumsum(sm, axis=0)
    lower_bounds = cs - cs[0:1]

    yp, ys = x_prompt, x_sample
    bp = x_prompt.shape[0]
    kp_l, vp_l, sp_l, ks_l, vs_l, ss_l, cs_l = [], [], [], [], [], [], []
    for l in range(DEPTH):
        params = (norm_mix[l], w_in[l], q_norm[l], k_norm[l], sinks[l], hgrn_out_norm[l],
                  sgu_v_norm[l], w_spatial[l], b_spatial[l], w_branch_a[l], w_branch_b[l],
                  w_branch_c[l], w_out[l], norm_ffn[l], w_ffn_up[l], w_ffn_down[l])
        zero_buf = jnp.zeros((bp, WINDOW, A_KV_HEADS, A_HEAD_DIM), x_prompt.dtype)
        zero_state = jnp.zeros((bp, B_HEADS, B_KEY_DIM, B_VAL_DIM), state_hgrn.dtype)
        yp, kp, vp, sp, _ = _layer(yp, 0, zero_buf, zero_buf, zero_state, lower_bounds[l], *params)
        ys, kss, vss, sss, vcs = _layer(ys, PAST_LEN, cache_k[l], cache_v[l], state_hgrn[l],
                                        lower_bounds[l], *params)
        kp_l.append(kp); vp_l.append(vp); sp_l.append(sp)
        ks_l.append(kss); vs_l.append(vss); ss_l.append(sss); cs_l.append(vcs)
    new_cache_k_prompt = jnp.stack(kp_l)
    new_cache_v_prompt = jnp.stack(vp_l)
    new_state_hgrn_prompt = jnp.stack(sp_l)
    new_cache_k_sample = jnp.stack(ks_l)
    new_cache_v_sample = jnp.stack(vs_l)
    new_state_hgrn_sample = jnp.stack(ss_l)
    new_sgu_v_sample = jnp.stack(cs_l)
    return (yp, ys, new_cache_k_prompt, new_cache_v_prompt, new_state_hgrn_prompt,
            new_cache_k_sample, new_cache_v_sample, new_state_hgrn_sample, new_sgu_v_sample)
```

```python
import functools

import numpy as np
import jax
import jax.numpy as jnp
from jax import lax
from jax.experimental import pallas as pl
from jax.experimental.pallas import tpu as pltpu

D_MODEL = 2048
BATCH = 4
SEQ = 2048
DEPTH = 4
DEC_BATCH = 32
DEC_SEQ = 4
PAST_LEN = 16384

A_HEADS = 16
A_KV_HEADS = 4
A_HEAD_DIM = 64
A_GROUP = A_HEADS // A_KV_HEADS
WINDOW = 128
ROT_DIM = A_HEAD_DIM // 4
ROT_HALF = ROT_DIM // 2
ROPE_THETA = 500000.0
B_HEADS = 8
B_KEY_DIM = 128
B_VAL_DIM = 128
B_CHUNK = 16
C_GROUPS = 8
C_GROUP_DIM = 128
C_CHUNK = 128
A_Q_W = A_HEADS * A_HEAD_DIM
A_KV_W = A_KV_HEADS * A_HEAD_DIM
B_W = B_HEADS * B_KEY_DIM
C_W = C_GROUPS * C_GROUP_DIM
FFN_DIM = ((8 * D_MODEL + 3 * 256 - 1) // (3 * 256)) * 256
EPS = 1e-6

M_PROMPT = BATCH * SEQ
M_SAMPLE = DEC_BATCH * DEC_SEQ
M_ALL = M_PROMPT + M_SAMPLE

OFF_A = 0
W_A = A_Q_W + 2 * A_KV_W
OFF_B = OFF_A + W_A
W_B = 4 * B_W
OFF_C = OFF_B + W_B
W_C = 2 * C_W
OFF_G = OFF_C + W_C
W_G = 3 * D_MODEL

BLK = 128
TM = 1040
TN = 512
TN_DOWN = 256
SAMPLE_PAD = 16
HGRN_SAMPLE_GROUP = 8
VMEM_LIMIT = 56 * 1024 * 1024

_BF16 = jnp.bfloat16
_F32 = jnp.float32


def _params(n_grid):
    return pltpu.CompilerParams(dimension_semantics=("arbitrary",) * n_grid,
                                vmem_limit_bytes=VMEM_LIMIT)


def _sigmoid(x):
    return 1.0 / (1.0 + jnp.exp(-x))


def _rmsnorm_kernel(x_ref, g_ref, o_ref):
    x = x_ref[...]
    y = x * lax.rsqrt(jnp.mean(x * x, axis=-1, keepdims=True) + EPS)
    o_ref[...] = (y * g_ref[...]).astype(_BF16)


def _rmsnorm(x, g):
    m, d = x.shape
    return pl.pallas_call(
        _rmsnorm_kernel,
        grid=(m // TM,),
        in_specs=[pl.BlockSpec((TM, d), lambda i: (i, 0)),
                  pl.BlockSpec((1, d), lambda i: (0, 0))],
        out_specs=pl.BlockSpec((TM, d), lambda i: (i, 0)),
        out_shape=jax.ShapeDtypeStruct((m, d), _BF16),
        compiler_params=_params(1),
        name="rmsnorm",
    )(x, g.reshape(1, d))


def _mm_kernel(a_ref, w_ref, o_ref):
    o_ref[...] = jnp.dot(a_ref[...], w_ref[...].astype(_BF16), preferred_element_type=_F32)


def _matmul_cols(a, w, col_off, n_cols):
    m, k = a.shape
    off = col_off // TN
    return pl.pallas_call(
        _mm_kernel,
        grid=(m // TM, n_cols // TN),
        in_specs=[pl.BlockSpec((TM, k), lambda i, j: (i, 0)),
                  pl.BlockSpec((k, TN), lambda i, j: (0, j + off))],
        out_specs=pl.BlockSpec((TM, TN), lambda i, j: (i, j)),
        out_shape=jax.ShapeDtypeStruct((m, n_cols), _F32),
        compiler_params=_params(2),
        name="proj_in",
    )(a, w)


def _mm_res_kernel(a_ref, w_ref, r_ref, o_ref):
    o_ref[...] = r_ref[...] + jnp.dot(a_ref[...], w_ref[...].astype(_BF16),
                                      preferred_element_type=_F32)


def _matmul_residual(a, w, r, tn):
    m, k = a.shape
    n = w.shape[1]
    return pl.pallas_call(
        _mm_res_kernel,
        grid=(m // TM, n // tn),
        in_specs=[pl.BlockSpec((TM, k), lambda i, j: (i, 0)),
                  pl.BlockSpec((k, tn), lambda i, j: (0, j)),
                  pl.BlockSpec((TM, tn), lambda i, j: (i, j))],
        out_specs=pl.BlockSpec((TM, tn), lambda i, j: (i, j)),
        out_shape=jax.ShapeDtypeStruct((m, n), _F32),
        compiler_params=_params(2),
        name="proj_residual",
    )(a, w, r)


def _ffn_up_kernel(a_ref, wg_ref, wu_ref, o_ref):
    a = a_ref[...]
    g = jnp.dot(a, wg_ref[...].astype(_BF16), preferred_element_type=_F32)
    u = jnp.dot(a, wu_ref[...].astype(_BF16), preferred_element_type=_F32)
    o_ref[...] = (g * _sigmoid(g) * u).astype(_BF16)


def _ffn_up(a, w_up):
    m, k = a.shape
    nj = FFN_DIM // TN
    return pl.pallas_call(
        _ffn_up_kernel,
        grid=(m // TM, nj),
        in_specs=[pl.BlockSpec((TM, k), lambda i, j: (i, 0)),
                  pl.BlockSpec((k, TN), lambda i, j: (0, j)),
                  pl.BlockSpec((k, TN), lambda i, j: (0, j + nj))],
        out_specs=pl.BlockSpec((TM, TN), lambda i, j: (i, j)),
        out_shape=jax.ShapeDtypeStruct((m, FFN_DIM), _BF16),
        compiler_params=_params(2),
        name="ffn_up",
    )(a, w_up, w_up)


def _merge_kernel(oa_ref, ob_ref, oc_ref, wa_ref, wb_ref, wc_ref, ga_ref, gb_ref, gc_ref, o_ref):
    ya = jnp.dot(oa_ref[...], wa_ref[...].astype(_BF16), preferred_element_type=_F32)
    yb = jnp.dot(ob_ref[...], wb_ref[...].astype(_BF16), preferred_element_type=_F32)
    yc = jnp.dot(oc_ref[...], wc_ref[...].astype(_BF16), preferred_element_type=_F32)
    merged = _sigmoid(ga_ref[...]) * ya + _sigmoid(gb_ref[...]) * yb + _sigmoid(gc_ref[...]) * yc
    o_ref[...] = merged.astype(_BF16)


def _merge(oa, ob, oc, gates, wa, wb, wc):
    m, k = oa.shape
    nj = D_MODEL // TN
    branch = pl.BlockSpec((TM, k), lambda i, j: (i, 0))
    weight = pl.BlockSpec((k, TN), lambda i, j: (0, j))
    return pl.pallas_call(
        _merge_kernel,
        grid=(m // TM, nj),
        in_specs=[branch, branch, branch, weight, weight, weight,
                  pl.BlockSpec((TM, TN), lambda i, j: (i, j)),
                  pl.BlockSpec((TM, TN), lambda i, j: (i, j + nj)),
                  pl.BlockSpec((TM, TN), lambda i, j: (i, j + 2 * nj))],
        out_specs=pl.BlockSpec((TM, TN), lambda i, j: (i, j)),
        out_shape=jax.ShapeDtypeStruct((m, D_MODEL), _BF16),
        compiler_params=_params(2),
        name="merge",
    )(oa, ob, oc, wa, wb, wc, gates, gates, gates)


def _rope_tables(p0, rows):
    pos = (p0 + jnp.arange(rows, dtype=jnp.int32)).astype(_F32)
    inv_freq = jnp.power(jnp.float32(ROPE_THETA), -jnp.arange(ROT_HALF, dtype=_F32) / ROT_HALF)
    ang = pos[:, None] * inv_freq[None, :]
    cos, sin = jnp.cos(ang), jnp.sin(ang)
    rest = A_HEAD_DIM - ROT_DIM
    zeros = jnp.zeros((rows, ROT_HALF), _F32)
    pad = jnp.zeros((rows, rest), _F32)
    c = jnp.concatenate([cos, cos, jnp.ones((rows, rest), _F32)], axis=1)
    s1 = jnp.concatenate([-sin, zeros, pad], axis=1)
    s2 = jnp.concatenate([zeros, sin, pad], axis=1)
    return tuple(jnp.tile(t, (1, 128 // A_HEAD_DIM)) for t in (c, s1, s2))


def _rope(x, c, s1, s2):
    width = x.shape[1]
    reps = width // c.shape[1]
    c, s1, s2 = (jnp.concatenate([t] * reps, axis=1) for t in (c, s1, s2))
    ahead = pltpu.roll(x, width - ROT_HALF, axis=1)
    behind = pltpu.roll(x, ROT_HALF, axis=1)
    return x * c + ahead * s1 + behind * s2


def _head_rstd(x):
    return lax.rsqrt(jnp.mean(x * x, axis=-1, keepdims=True) + EPS)


def _attend(sinks_ref, q_raw, q_rot, keys_ref, vals_ref, valid, o_ref):
    hd = A_HEAD_DIM
    for h in range(A_KV_HEADS):
        keys = keys_ref[:, h * hd:(h + 1) * hd].astype(_BF16)
        vals = vals_ref[:, h * hd:(h + 1) * hd].astype(_BF16)
        for g in range(A_GROUP):
            hh = h * A_GROUP + g
            lanes = slice(hh * hd, (hh + 1) * hd)
            qn = q_rot[:, lanes] * (_head_rstd(q_raw[:, lanes]) * (hd ** -0.5))
            s = lax.dot_general(qn.astype(_BF16), keys, (((1,), (1,)), ((), ())),
                                preferred_element_type=_F32)
            s = jnp.where(valid, s, -jnp.inf)
            sink = sinks_ref[hh]
            m = jnp.maximum(jnp.max(s, axis=-1, keepdims=True), sink)
            p = jnp.exp(s - m)
            den = jnp.sum(p, axis=-1, keepdims=True) + jnp.exp(sink - m)
            o = jnp.dot(p.astype(_BF16), vals, preferred_element_type=_F32) / den
            o_ref[:, lanes] = o.astype(o_ref.dtype)


def _attn_prompt_kernel(sinks_ref, q_ref, kv_ref, c_ref, s1_ref, s2_ref, gq_ref, gk_ref,
                        o_ref, kc_ref, vc_ref, kbuf, vbuf):
    n = pl.program_id(1)

    @pl.when(n == 0)
    def _():
        kbuf[0:BLK, :] = jnp.zeros((BLK, A_KV_W), _F32)
        vbuf[0:BLK, :] = jnp.zeros((BLK, A_KV_W), _F32)

    q = q_ref[...]
    kv = kv_ref[...]
    k = kv[:, :A_KV_W]
    v = kv[:, A_KV_W:]
    c, s1, s2 = c_ref[...], s1_ref[...], s2_ref[...]
    q_rot = _rope(q * gq_ref[...], c, s1, s2)
    k_rot = _rope(k * gk_ref[...], c, s1, s2)
    for h in range(A_KV_HEADS):
        lanes = slice(h * A_HEAD_DIM, (h + 1) * A_HEAD_DIM)
        kbuf[BLK:2 * BLK, lanes] = k_rot[:, lanes] * _head_rstd(k[:, lanes])
    vbuf[BLK:2 * BLK, :] = v

    row = lax.broadcasted_iota(jnp.int32, (BLK, 2 * BLK), 0)
    col = lax.broadcasted_iota(jnp.int32, (BLK, 2 * BLK), 1)
    valid = (col > row) & (col <= row + WINDOW) & ((col >= BLK) | (n > 0))
    _attend(sinks_ref, q, q_rot, kbuf, vbuf, valid, o_ref)

    @pl.when(n == pl.num_programs(1) - 1)
    def _():
        kc_ref[0] = kbuf[BLK:2 * BLK, :]
        vc_ref[0] = vbuf[BLK:2 * BLK, :]

    kbuf[0:BLK, :] = kbuf[BLK:2 * BLK, :]
    vbuf[0:BLK, :] = vbuf[BLK:2 * BLK, :]


def _attn_prompt(a, sinks, gq, gk, tables):
    nb = SEQ // BLK
    c, s1, s2 = tables
    table = pl.BlockSpec((BLK, 128), lambda b, n: (n, 0))
    cache = pl.BlockSpec((1, BLK, A_KV_W), lambda b, n: (b, 0, 0))
    return pl.pallas_call(
        _attn_prompt_kernel,
        grid=(BATCH, nb),
        in_specs=[pl.BlockSpec(memory_space=pltpu.SMEM),
                  pl.BlockSpec((BLK, A_Q_W), lambda b, n: (b * nb + n, 0)),
                  pl.BlockSpec((BLK, 2 * A_KV_W), lambda b, n: (b * nb + n, A_Q_W // (2 * A_KV_W))),
                  table, table, table,
                  pl.BlockSpec((1, A_Q_W), lambda b, n: (0, 0)),
                  pl.BlockSpec((1, A_KV_W), lambda b, n: (0, 0))],
        out_specs=[pl.BlockSpec((BLK, A_Q_W), lambda b, n: (b * nb + n, 0)), cache, cache],
        out_shape=[jax.ShapeDtypeStruct((M_PROMPT, A_Q_W), _BF16),
                   jax.ShapeDtypeStruct((BATCH, WINDOW, A_KV_W), _F32),
                   jax.ShapeDtypeStruct((BATCH, WINDOW, A_KV_W), _F32)],
        scratch_shapes=[pltpu.VMEM((2 * BLK, A_KV_W), _F32), pltpu.VMEM((2 * BLK, A_KV_W), _F32)],
        compiler_params=_params(2),
        name="attn_prompt",
    )(sinks, a, a, c, s1, s2, gq, gk)


def _attn_sample_kernel(sinks_ref, qkv_ref, ck_ref, cv_ref, c_ref, s1_ref, s2_ref, gq_ref, gk_ref,
                        o_ref, kc_ref, vc_ref, kbuf, vbuf):
    qkv = qkv_ref[0]
    q = qkv[:, :A_Q_W]
    k = qkv[:, A_Q_W:A_Q_W + A_KV_W]
    v = qkv[:, A_Q_W + A_KV_W:]
    c, s1, s2 = c_ref[...], s1_ref[...], s2_ref[...]
    q_rot = _rope(q * gq_ref[...], c, s1, s2)
    k_rot = _rope(k * gk_ref[...], c, s1, s2)
    new_end = WINDOW + SAMPLE_PAD
    kbuf[0:WINDOW, :] = ck_ref[0]
    vbuf[0:WINDOW, :] = cv_ref[0]
    for h in range(A_KV_HEADS):
        lanes = slice(h * A_HEAD_DIM, (h + 1) * A_HEAD_DIM)
        kbuf[WINDOW:new_end, lanes] = k_rot[:, lanes] * _head_rstd(k[:, lanes])
    vbuf[WINDOW:new_end, :] = v
    kbuf[new_end:, :] = jnp.zeros((2 * BLK - new_end, A_KV_W), _F32)
    vbuf[new_end:, :] = jnp.zeros((2 * BLK - new_end, A_KV_W), _F32)

    row = lax.broadcasted_iota(jnp.int32, (SAMPLE_PAD, 2 * BLK), 0)
    col = lax.broadcasted_iota(jnp.int32, (SAMPLE_PAD, 2 * BLK), 1)
    valid = (col > row) & (col <= row + WINDOW)
    _attend(sinks_ref, q, q_rot, kbuf, vbuf, valid, o_ref.at[0])
    kc_ref[0] = kbuf[DEC_SEQ:DEC_SEQ + WINDOW, :]
    vc_ref[0] = vbuf[DEC_SEQ:DEC_SEQ + WINDOW, :]


def _attn_sample(a_sample, cache_k, cache_v, sinks, gq, gk, tables):
    c, s1, s2 = tables
    qkv = jnp.pad(a_sample.reshape(DEC_BATCH, DEC_SEQ, W_A),
                  ((0, 0), (0, SAMPLE_PAD - DEC_SEQ), (0, 0)))
    table = pl.BlockSpec((SAMPLE_PAD, 128), lambda b: (0, 0))
    cache = pl.BlockSpec((1, WINDOW, A_KV_W), lambda b: (b, 0, 0))
    o, kc, vc = pl.pallas_call(
        _attn_sample_kernel,
        grid=(DEC_BATCH,),
        in_specs=[pl.BlockSpec(memory_space=pltpu.SMEM),
                  pl.BlockSpec((1, SAMPLE_PAD, W_A), lambda b: (b, 0, 0)),
                  cache, cache, table, table, table,
                  pl.BlockSpec((1, A_Q_W), lambda b: (0, 0)),
                  pl.BlockSpec((1, A_KV_W), lambda b: (0, 0))],
        out_specs=[pl.BlockSpec((1, SAMPLE_PAD, A_Q_W), lambda b: (b, 0, 0)), cache, cache],
        out_shape=[jax.ShapeDtypeStruct((DEC_BATCH, SAMPLE_PAD, A_Q_W), _BF16),
                   jax.ShapeDtypeStruct((DEC_BATCH, WINDOW, A_KV_W), _F32),
                   jax.ShapeDtypeStruct((DEC_BATCH, WINDOW, A_KV_W), _F32)],
        scratch_shapes=[pltpu.VMEM((2 * BLK, A_KV_W), _F32), pltpu.VMEM((2 * BLK, A_KV_W), _F32)],
        compiler_params=_params(1),
        name="attn_sample",
    )(sinks, qkv, cache_k.reshape(DEC_BATCH, WINDOW, A_KV_W), cache_v.reshape(DEC_BATCH, WINDOW, A_KV_W),
      c, s1, s2, gq, gk)
    return o[:, :DEC_SEQ].reshape(M_SAMPLE, A_Q_W), kc, vc


def _lower_bound_kernel(logits_ref, loglb_ref, log1m_ref):
    x = logits_ref[...]
    e = jnp.exp(x - jnp.max(x, axis=0, keepdims=True))
    sm = e / jnp.sum(e, axis=0, keepdims=True)
    acc = sm[0:1]
    rows = [acc]
    for l in range(1, DEPTH):
        acc = acc + sm[l:l + 1]
        rows.append(acc)
    for l in range(DEPTH):
        lb = rows[l] - rows[0]
        loglb_ref[l:l + 1, :] = jnp.log(lb)
        log1m_ref[l:l + 1, :] = jnp.log1p(-lb)


def _lower_bounds(lb_logits):
    shape = jax.ShapeDtypeStruct(lb_logits.shape, _F32)
    return pl.pallas_call(_lower_bound_kernel, out_shape=[shape, shape], name="hgrn_lower_bounds")(lb_logits)


def _chunk_matrices(rows, chunk):
    t = np.arange(rows)[:, None]
    s = np.arange(rows)[None, :]
    same = (t // chunk) == (s // chunk)
    tri = (same & (s <= t)).astype(np.float32)
    ref = (same & ((s % chunk) <= chunk // 2)).astype(np.float32)
    last = same.astype(np.float32)
    return jnp.asarray(np.concatenate([tri, tri - ref, last - tri], axis=0), dtype=_BF16)


def _split3(x):
    hi = x.astype(_BF16)
    r = x - hi.astype(_F32)
    mid = r.astype(_BF16)
    lo = (r - mid.astype(_F32)).astype(_BF16)
    return hi, mid, lo


def _hgrn_gates(q, z, loglb, log1m, lt, rows):
    log_sig = jnp.minimum(z, 0.0) - jnp.log1p(jnp.exp(-jnp.abs(z)))
    b = log1m + log_sig
    log_f = jnp.maximum(loglb, b) + jnp.log1p(jnp.exp(-jnp.abs(loglb - b)))
    kk = -jnp.tanh(0.5 * log_f) * (jnp.exp(log_f) + 1.0)
    hi, mid, lo = _split3(log_f)
    cums = (jnp.dot(lt, hi, preferred_element_type=_F32)
            + jnp.dot(lt, mid, preferred_element_type=_F32)
            + jnp.dot(lt, lo, preferred_element_type=_F32))
    cum = cums[0:rows]
    cum_ref = cums[rows:2 * rows]
    cum_end = cums[2 * rows:3 * rows]
    e_cum = jnp.exp(cum)
    q_intra = q * jnp.exp(cum_ref)
    k_intra = kk * jnp.exp(-cum_ref)
    q_inter = q * e_cum
    k_state = kk * jnp.exp(cum_end)
    return q_intra, k_intra, q_inter, k_state, e_cum


def _hgrn_finish(o, g, w):
    y = o * lax.rsqrt(jnp.mean(o * o, axis=-1, keepdims=True) + EPS) * w
    return (y * (g * _sigmoid(g))).astype(_BF16)


def _hgrn_prompt_kernel(q_ref, f_ref, i_ref, g_ref, loglb_ref, log1m_ref, lt_ref, w_ref,
                        o_ref, s_ref, st):
    n = pl.program_id(1)

    @pl.when(n == 0)
    def _():
        st[...] = jnp.zeros(st.shape, _F32)

    q_intra, k_intra, q_inter, k_state, e_cum = _hgrn_gates(
        q_ref[...], f_ref[...], loglb_ref[...], log1m_ref[...], lt_ref[...], BLK)
    row = lax.broadcasted_iota(jnp.int32, (BLK, BLK), 0)
    col = lax.broadcasted_iota(jnp.int32, (BLK, BLK), 1)
    causal = (row // B_CHUNK == col // B_CHUNK) & (col <= row)
    n_chunks = BLK // B_CHUNK
    w = w_ref[...]
    for h in range(B_HEADS):
        lanes = slice(h * B_KEY_DIM, (h + 1) * B_KEY_DIM)
        v = i_ref[:, lanes]
        att = lax.dot_general(q_intra[:, lanes].astype(_BF16), k_intra[:, lanes].astype(_BF16),
                              (((1,), (1,)), ((), ())), preferred_element_type=_F32)
        att = jnp.where(causal, att, 0.0)
        o = jnp.dot(att.astype(_BF16), v.astype(_BF16), preferred_element_type=_F32)
        v_t = v.T
        ks = k_state[:, lanes].astype(_BF16)
        qi = q_inter[:, lanes].astype(_BF16)
        ec = e_cum[:, lanes]
        state_t = st[h]
        o_inter = []
        for c in range(n_chunks):
            rows = slice(c * B_CHUNK, (c + 1) * B_CHUNK)
            o_inter.append(lax.dot_general(qi[rows], state_t.astype(_BF16), (((1,), (1,)), ((), ())),
                                           preferred_element_type=_F32))
            in_chunk = (col >= c * B_CHUNK) & (col < (c + 1) * B_CHUNK)
            v_chunk = jnp.where(in_chunk, v_t, 0.0).astype(_BF16)
            update = jnp.dot(v_chunk, ks, preferred_element_type=_F32)
            decay = ec[(c + 1) * B_CHUNK - 1:(c + 1) * B_CHUNK, :]
            state_t = state_t * decay + update
        st[h] = state_t
        o = o + jnp.concatenate(o_inter, axis=0)
        o_ref[:, lanes] = _hgrn_finish(o, g_ref[:, lanes], w)

        @pl.when(n == pl.num_programs(1) - 1)
        def _():
            s_ref[0, h] = state_t.T


def _hgrn_prompt(bm, loglb, log1m, w):
    nb = SEQ // BLK
    gate = lambda j: pl.BlockSpec((BLK, B_W), lambda b, n: (b * nb + n, j))
    vec = pl.BlockSpec((1, B_W), lambda b, n: (0, 0))
    return pl.pallas_call(
        _hgrn_prompt_kernel,
        grid=(BATCH, nb),
        in_specs=[gate(0), gate(1), gate(2), gate(3), vec, vec,
                  pl.BlockSpec((3 * BLK, BLK), lambda b, n: (0, 0)),
                  pl.BlockSpec((1, B_VAL_DIM), lambda b, n: (0, 0))],
        out_specs=[pl.BlockSpec((BLK, B_W), lambda b, n: (b * nb + n, 0)),
                   pl.BlockSpec((1, B_HEADS, B_KEY_DIM, B_VAL_DIM), lambda b, n: (b, 0, 0, 0))],
        out_shape=[jax.ShapeDtypeStruct((M_PROMPT, B_W), _BF16),
                   jax.ShapeDtypeStruct((BATCH, B_HEADS, B_KEY_DIM, B_VAL_DIM), _F32)],
        scratch_shapes=[pltpu.VMEM((B_HEADS, B_VAL_DIM, B_KEY_DIM), _F32)],
        compiler_params=_params(2),
        name="hgrn_prompt",
    )(bm, bm, bm, bm, loglb, log1m, _chunk_matrices(BLK, B_CHUNK), w)


def _hgrn_sample_kernel(q_ref, f_ref, i_ref, g_ref, loglb_ref, log1m_ref, lt_ref, w_ref, s0_ref,
                        o_ref, s_ref):
    rows = HGRN_SAMPLE_GROUP * DEC_SEQ

    def padded(ref):
        return jnp.concatenate([ref[...], jnp.zeros((BLK - rows, B_W), _F32)], axis=0)

    v_all = padded(i_ref)
    g_all = padded(g_ref)
    q_intra, k_intra, q_inter, k_state, e_cum = _hgrn_gates(
        padded(q_ref), padded(f_ref), loglb_ref[...], log1m_ref[...], lt_ref[...], BLK)
    row = lax.broadcasted_iota(jnp.int32, (BLK, BLK), 0)
    col = lax.broadcasted_iota(jnp.int32, (BLK, BLK), 1)
    causal = (row // DEC_SEQ == col // DEC_SEQ) & (col <= row)
    w = w_ref[...]
    for h in range(B_HEADS):
        lanes = slice(h * B_KEY_DIM, (h + 1) * B_KEY_DIM)
        v = v_all[:, lanes].astype(_BF16)
        att = lax.dot_general(q_intra[:, lanes].astype(_BF16), k_intra[:, lanes].astype(_BF16),
                              (((1,), (1,)), ((), ())), preferred_element_type=_F32)
        att = jnp.where(causal, att, 0.0)
        o = jnp.dot(att.astype(_BF16), v, preferred_element_type=_F32)
        qi = q_inter[:, lanes].astype(_BF16)
        ec_t = e_cum[:, lanes].T
        ks_t = k_state[:, lanes].T
        for s in range(HGRN_SAMPLE_GROUP):
            state = s0_ref[s, h]
            o_s = jnp.dot(qi, state.astype(_BF16), preferred_element_type=_F32)
            o = o + jnp.where(row // DEC_SEQ == s, o_s, 0.0)
            ks_seq = jnp.where(col // DEC_SEQ == s, ks_t, 0.0).astype(_BF16)
            update = jnp.dot(ks_seq, v, preferred_element_type=_F32)
            decay = ec_t[:, (s + 1) * DEC_SEQ - 1:(s + 1) * DEC_SEQ]
            s_ref[s, h] = state * decay + update
        o_ref[:, lanes] = _hgrn_finish(o, g_all[:, lanes], w)[0:rows]


def _hgrn_sample(bm, state, loglb, log1m, w):
    rows = HGRN_SAMPLE_GROUP * DEC_SEQ
    first = M_PROMPT // rows
    gate = lambda j: pl.BlockSpec((rows, B_W), lambda s: (first + s, j))
    vec = pl.BlockSpec((1, B_W), lambda s: (0, 0))
    st = pl.BlockSpec((HGRN_SAMPLE_GROUP, B_HEADS, B_KEY_DIM, B_VAL_DIM), lambda s: (s, 0, 0, 0))
    return pl.pallas_call(
        _hgrn_sample_kernel,
        grid=(DEC_BATCH // HGRN_SAMPLE_GROUP,),
        in_specs=[gate(0), gate(1), gate(2), gate(3), vec, vec,
                  pl.BlockSpec((3 * BLK, BLK), lambda s: (0, 0)),
                  pl.BlockSpec((1, B_VAL_DIM), lambda s: (0, 0)), st],
        out_specs=[pl.BlockSpec((rows, B_W), lambda s: (s, 0)), st],
        out_shape=[jax.ShapeDtypeStruct((M_SAMPLE, B_W), _BF16),
                   jax.ShapeDtypeStruct(state.shape, _F32)],
        compiler_params=_params(1),
        name="hgrn_sample",
    )(bm, bm, bm, bm, loglb, log1m, _chunk_matrices(BLK, DEC_SEQ), w, state)


def _sgu_kernel(u_ref, v_ref, w_ref, b_ref, g_ref, o_ref, vn_ref):
    i = pl.program_id(0)
    is_sample = i == pl.num_programs(0) - 1
    row = lax.broadcasted_iota(jnp.int32, (BLK, BLK), 0)
    col = lax.broadcasted_iota(jnp.int32, (BLK, BLK), 1)
    same_seq = jnp.logical_or(jnp.logical_not(is_sample), row // DEC_SEQ == col // DEC_SEQ)
    causal = (col <= row) & same_seq
    gain = g_ref[...]
    bias = b_ref[0]
    for g in range(C_GROUPS):
        lanes = slice(g * C_GROUP_DIM, (g + 1) * C_GROUP_DIM)
        v = v_ref[:, lanes]
        vn = v * lax.rsqrt(jnp.mean(v * v, axis=-1, keepdims=True) + EPS) * gain
        w = jnp.where(causal, w_ref[0, g], 0.0).astype(_BF16)
        z = jnp.dot(w, vn.astype(_BF16), preferred_element_type=_F32) + bias[:, g:g + 1]
        o_ref[:, lanes] = (u_ref[:, lanes] * z).astype(_BF16)

        @pl.when(is_sample)
        def _():
            vn_ref[:, lanes] = vn


def _sgu(cm, w_spatial, b_spatial, gain):
    nblk = M_ALL // BLK
    reps = BLK // DEC_SEQ
    w_sample = jnp.tile(w_spatial[:, :DEC_SEQ, :DEC_SEQ], (1, reps, reps))
    b_sample = jnp.tile(b_spatial[:, :DEC_SEQ], (1, reps))
    w2 = jnp.stack([w_spatial, w_sample])
    b2 = jnp.stack([b_spatial.T, b_sample.T])
    which = lambda i: i // (nblk - 1)
    return pl.pallas_call(
        _sgu_kernel,
        grid=(nblk,),
        in_specs=[pl.BlockSpec((BLK, C_W), lambda i: (i, 0)),
                  pl.BlockSpec((BLK, C_W), lambda i: (i, 1)),
                  pl.BlockSpec((1, C_GROUPS, C_CHUNK, C_CHUNK), lambda i: (which(i), 0, 0, 0)),
                  pl.BlockSpec((1, C_CHUNK, C_GROUPS), lambda i: (which(i), 0, 0)),
                  pl.BlockSpec((1, C_GROUP_DIM), lambda i: (0, 0))],
        out_specs=[pl.BlockSpec((BLK, C_W), lambda i: (i, 0)),
                   pl.BlockSpec((BLK, C_W), lambda i: (0, 0))],
        out_shape=[jax.ShapeDtypeStruct((M_ALL, C_W), _BF16),
                   jax.ShapeDtypeStruct((M_SAMPLE, C_W), _F32)],
        compiler_params=_params(1),
        name="sgu",
    )(cm, cm, w2, b2, gain)


def kernel(x_prompt, x_sample, cache_k, cache_v, state_hgrn, norm_mix, w_in, q_norm, k_norm, sinks,
           lb_logits, hgrn_out_norm, sgu_v_norm, w_spatial, b_spatial, w_branch_a, w_branch_b,
           w_branch_c, w_out, norm_ffn, w_ffn_up, w_ffn_down):
    x = jnp.concatenate([x_prompt.reshape(M_PROMPT, D_MODEL), x_sample.reshape(M_SAMPLE, D_MODEL)], axis=0)
    loglb, log1m = _lower_bounds(lb_logits)
    tables_prompt = _rope_tables(0, SEQ)
    tables_sample = _rope_tables(PAST_LEN, SAMPLE_PAD)

    kp_l, vp_l, sp_l, ks_l, vs_l, ss_l, cs_l = [], [], [], [], [], [], []
    for l in range(DEPTH):
        h = _rmsnorm(x, norm_mix[l])
        am = _matmul_cols(h, w_in[l], OFF_A, W_A)
        bm = _matmul_cols(h, w_in[l], OFF_B, W_B)
        cm = _matmul_cols(h, w_in[l], OFF_C, W_C)
        gates = _matmul_cols(h, w_in[l], OFF_G, W_G)

        gq = jnp.tile(q_norm[l], A_HEADS).reshape(1, A_Q_W)
        gk = jnp.tile(k_norm[l], A_KV_HEADS).reshape(1, A_KV_W)
        oa_p, kc_p, vc_p = _attn_prompt(am, sinks[l], gq, gk, tables_prompt)
        oa_s, kc_s, vc_s = _attn_sample(am[M_PROMPT:], cache_k[l], cache_v[l], sinks[l], gq, gk,
                                        tables_sample)

        lb_a = loglb[l].reshape(1, B_W)
        lb_b = log1m[l].reshape(1, B_W)
        w_hg = hgrn_out_norm[l].reshape(1, B_VAL_DIM)
        ob_p, st_p = _hgrn_prompt(bm, lb_a, lb_b, w_hg)
        ob_s, st_s = _hgrn_sample(bm, state_hgrn[l], lb_a, lb_b, w_hg)

        oc, vn_s = _sgu(cm, w_spatial[l], b_spatial[l], sgu_v_norm[l].reshape(1, C_GROUP_DIM))

        oa = jnp.concatenate([oa_p, oa_s], axis=0)
        ob = jnp.concatenate([ob_p, ob_s], axis=0)
        merged = _merge(oa, ob, oc, gates, w_branch_a[l], w_branch_b[l], w_branch_c[l])
        x = _matmul_residual(merged, w_out[l], x, TN)
        h2 = _rmsnorm(x, norm_ffn[l])
        act = _ffn_up(h2, w_ffn_up[l])
        x = _matmul_residual(act, w_ffn_down[l], x, TN_DOWN)

        kp_l.append(kc_p.reshape(BATCH, WINDOW, A_KV_HEADS, A_HEAD_DIM))
        vp_l.append(vc_p.reshape(BATCH, WINDOW, A_KV_HEADS, A_HEAD_DIM))
        sp_l.append(st_p)
        ks_l.append(kc_s.reshape(DEC_BATCH, WINDOW, A_KV_HEADS, A_HEAD_DIM))
        vs_l.append(vc_s.reshape(DEC_BATCH, WINDOW, A_KV_HEADS, A_HEAD_DIM))
        ss_l.append(st_s)
        cs_l.append(vn_s.reshape(DEC_BATCH, DEC_SEQ, C_GROUPS, C_GROUP_DIM))

    y_prompt = x[:M_PROMPT].reshape(BATCH, SEQ, D_MODEL)
    y_sample = x[M_PROMPT:].reshape(DEC_BATCH, DEC_SEQ, D_MODEL)
    return (y_prompt, y_sample, jnp.stack(kp_l), jnp.stack(vp_l), jnp.stack(sp_l),
            jnp.stack(ks_l), jnp.stack(vs_l), jnp.stack(ss_l), jnp.stack(cs_l))
```

```python
import numpy as np
import jax
import jax.numpy as jnp
from jax import lax
from jax.experimental import pallas as pl
from jax.experimental.pallas import tpu as pltpu

D_MODEL = 2048
BATCH = 4
SEQ = 2048
DEPTH = 4
DEC_BATCH = 32
DEC_SEQ = 4
PAST_LEN = 16384

A_HEADS = 16
A_KV_HEADS = 4
A_HEAD_DIM = 64
A_GROUP = A_HEADS // A_KV_HEADS
WINDOW = 128
ROT_DIM = A_HEAD_DIM // 4
ROT_HALF = ROT_DIM // 2
ROPE_THETA = 500000.0
B_HEADS = 8
B_KEY_DIM = 128
B_VAL_DIM = 128
B_CHUNK = 16
C_GROUPS = 8
C_GROUP_DIM = 128
C_CHUNK = 128
A_Q_W = A_HEADS * A_HEAD_DIM
A_KV_W = A_KV_HEADS * A_HEAD_DIM
B_W = B_HEADS * B_KEY_DIM
C_W = C_GROUPS * C_GROUP_DIM
FFN_DIM = ((8 * D_MODEL + 3 * 256 - 1) // (3 * 256)) * 256
EPS = 1e-6

M_PROMPT = BATCH * SEQ
M_SAMPLE = DEC_BATCH * DEC_SEQ
M_ALL = M_PROMPT + M_SAMPLE

OFF_A = 0
W_A = A_Q_W + 2 * A_KV_W
OFF_B = OFF_A + W_A
W_B = 4 * B_W
OFF_C = OFF_B + W_B
W_C = 2 * C_W
OFF_G = OFF_C + W_C
W_G = 3 * D_MODEL

LANES = 128
BLK = 128
TM = M_ALL // 4
TM_DOWN = M_ALL // 8
TM_NORM = M_ALL // 16
TN = 512
TN_NARROW = 256
SAMPLE_PAD = 16
ATTN_SAMPLE_GROUP = 4
HGRN_SAMPLE_GROUP = 8
VMEM_LIMIT = 56 * 1024 * 1024

_BF16 = jnp.bfloat16
_F32 = jnp.float32
_NT = (((1,), (1,)), ((), ()))
_TN = (((0,), (0,)), ((), ()))


def _params(n_grid):
    return pltpu.CompilerParams(dimension_semantics=("arbitrary",) * n_grid,
                                vmem_limit_bytes=VMEM_LIMIT)


def _sigmoid(x):
    return 1.0 / (1.0 + jnp.exp(-x))


def _layer_vec(width, layer):
    return pl.BlockSpec((None, 1, width), lambda *_: (layer, 0, 0))


def _rmsnorm_kernel(x_ref, g_ref, o_ref):
    x = x_ref[...]
    y = x * lax.rsqrt(jnp.mean(x * x, axis=-1, keepdims=True) + EPS)
    o_ref[...] = (y * g_ref[...]).astype(_BF16)


def _rmsnorm(x, g, layer):
    m, d = x.shape
    return pl.pallas_call(
        _rmsnorm_kernel,
        grid=(m // TM_NORM,),
        in_specs=[pl.BlockSpec((TM_NORM, d), lambda i: (i, 0)), _layer_vec(d, layer)],
        out_specs=pl.BlockSpec((TM_NORM, d), lambda i: (i, 0)),
        out_shape=jax.ShapeDtypeStruct((m, d), _BF16),
        compiler_params=_params(1),
        name="rmsnorm",
    )(x, g)


def _mm_kernel(a_ref, w_ref, o_ref):
    o_ref[...] = jnp.dot(a_ref[...], w_ref[...].astype(_BF16), preferred_element_type=_F32)


def _matmul_cols(a, w, layer, col_off, n_cols):
    m, k = a.shape
    off = col_off // TN
    return pl.pallas_call(
        _mm_kernel,
        grid=(m // TM, n_cols // TN),
        in_specs=[pl.BlockSpec((TM, k), lambda i, j: (i, 0)),
                  pl.BlockSpec((None, k, TN), lambda i, j: (layer, 0, j + off))],
        out_specs=pl.BlockSpec((TM, TN), lambda i, j: (i, j)),
        out_shape=jax.ShapeDtypeStruct((m, n_cols), _F32),
        compiler_params=_params(2),
        name="proj_in",
    )(a, w)


def _mm_res_kernel(a_ref, w_ref, r_ref, o_ref):
    o_ref[...] = r_ref[...] + jnp.dot(a_ref[...], w_ref[...].astype(_BF16),
                                      preferred_element_type=_F32)


def _matmul_residual(a, w, layer, r, tm, tn):
    m, k = a.shape
    n = w.shape[2]
    return pl.pallas_call(
        _mm_res_kernel,
        grid=(m // tm, n // tn),
        in_specs=[pl.BlockSpec((tm, k), lambda i, j: (i, 0)),
                  pl.BlockSpec((None, k, tn), lambda i, j: (layer, 0, j)),
                  pl.BlockSpec((tm, tn), lambda i, j: (i, j))],
        out_specs=pl.BlockSpec((tm, tn), lambda i, j: (i, j)),
        out_shape=jax.ShapeDtypeStruct((m, n), _F32),
        compiler_params=_params(2),
        name="proj_residual",
    )(a, w, r)


def _ffn_up_kernel(a_ref, wg_ref, wu_ref, o_ref):
    a = a_ref[...]
    g = jnp.dot(a, wg_ref[...].astype(_BF16), preferred_element_type=_F32)
    u = jnp.dot(a, wu_ref[...].astype(_BF16), preferred_element_type=_F32)
    o_ref[...] = (g * _sigmoid(g) * u).astype(_BF16)


def _ffn_up(a, w_up, layer):
    m, k = a.shape
    tn = TN_NARROW
    nj = FFN_DIM // tn
    return pl.pallas_call(
        _ffn_up_kernel,
        grid=(m // TM, nj),
        in_specs=[pl.BlockSpec((TM, k), lambda i, j: (i, 0)),
                  pl.BlockSpec((None, k, tn), lambda i, j: (layer, 0, j)),
                  pl.BlockSpec((None, k, tn), lambda i, j: (layer, 0, j + nj))],
        out_specs=pl.BlockSpec((TM, tn), lambda i, j: (i, j)),
        out_shape=jax.ShapeDtypeStruct((m, FFN_DIM), _BF16),
        compiler_params=_params(2),
        name="ffn_up",
    )(a, w_up, w_up)


def _merge_kernel(oa_ref, ob_ref, oc_ref, wa_ref, wb_ref, wc_ref, ga_ref, gb_ref, gc_ref, o_ref):
    ya = jnp.dot(oa_ref[...], wa_ref[...].astype(_BF16), preferred_element_type=_F32)
    yb = jnp.dot(ob_ref[...], wb_ref[...].astype(_BF16), preferred_element_type=_F32)
    yc = jnp.dot(oc_ref[...], wc_ref[...].astype(_BF16), preferred_element_type=_F32)
    merged = _sigmoid(ga_ref[...]) * ya + _sigmoid(gb_ref[...]) * yb + _sigmoid(gc_ref[...]) * yc
    o_ref[...] = merged.astype(_BF16)


def _merge(oa, ob, oc, gates, wa, wb, wc, layer):
    m, k = oa.shape
    tn = TN_NARROW
    nj = D_MODEL // tn
    branch = pl.BlockSpec((TM, k), lambda i, j: (i, 0))
    weight = pl.BlockSpec((None, k, tn), lambda i, j: (layer, 0, j))
    return pl.pallas_call(
        _merge_kernel,
        grid=(m // TM, nj),
        in_specs=[branch, branch, branch, weight, weight, weight,
                  pl.BlockSpec((TM, tn), lambda i, j: (i, j)),
                  pl.BlockSpec((TM, tn), lambda i, j: (i, j + nj)),
                  pl.BlockSpec((TM, tn), lambda i, j: (i, j + 2 * nj))],
        out_specs=pl.BlockSpec((TM, tn), lambda i, j: (i, j)),
        out_shape=jax.ShapeDtypeStruct((m, D_MODEL), _BF16),
        compiler_params=_params(2),
        name="merge",
    )(oa, ob, oc, wa, wb, wc, gates, gates, gates)


def _rope_tables(p0, rows):
    pos = (p0 + jnp.arange(rows, dtype=jnp.int32)).astype(_F32)
    inv_freq = jnp.power(jnp.float32(ROPE_THETA), -jnp.arange(ROT_HALF, dtype=_F32) / ROT_HALF)
    ang = pos[:, None] * inv_freq[None, :]
    cos, sin = jnp.cos(ang), jnp.sin(ang)
    rest = A_HEAD_DIM - ROT_DIM
    zeros = jnp.zeros((rows, ROT_HALF), _F32)
    pad = jnp.zeros((rows, rest), _F32)
    c = jnp.concatenate([cos, cos, jnp.ones((rows, rest), _F32)], axis=1)
    s1 = jnp.concatenate([-sin, zeros, pad], axis=1)
    s2 = jnp.concatenate([zeros, sin, pad], axis=1)
    return tuple(jnp.tile(t, (1, LANES // A_HEAD_DIM)) for t in (c, s1, s2))


def _rope(x, c, s1, s2):
    width = x.shape[1]
    reps = width // c.shape[1]
    c, s1, s2 = (jnp.concatenate([t] * reps, axis=1) for t in (c, s1, s2))
    ahead = pltpu.roll(x, width - ROT_HALF, axis=1)
    behind = pltpu.roll(x, ROT_HALF, axis=1)
    return x * c + ahead * s1 + behind * s2


def _head_rstd(x):
    return lax.rsqrt(jnp.mean(x * x, axis=-1, keepdims=True) + EPS)


def _head_lanes(h):
    return slice(h * A_HEAD_DIM, (h + 1) * A_HEAD_DIM)


def _norm_rope_keys(k, gk, c, s1, s2):
    k_rot = _rope(k * gk, c, s1, s2)
    return jnp.concatenate([k_rot[:, _head_lanes(h)] * _head_rstd(k[:, _head_lanes(h)])
                            for h in range(A_KV_HEADS)], axis=1)


def _attend(sinks_ref, q_raw, q_rot, k_prev, v_prev, k_cur, v_cur, use_prev, o_ref):
    hd = A_HEAD_DIM
    t = q_raw.shape[0]
    rows = A_GROUP * t
    row = lax.broadcasted_iota(jnp.int32, (rows, WINDOW), 0) % t
    col = lax.broadcasted_iota(jnp.int32, (rows, WINDOW), 1)
    before = col > row
    ones = jnp.ones((WINDOW, LANES - hd), _F32)
    for h in range(A_KV_HEADS):
        kl = _head_lanes(h)
        kp = k_prev[:, kl].astype(_BF16)
        kc = k_cur[:, kl].astype(_BF16)
        vp = jnp.concatenate([v_prev[:, kl], ones], axis=1).astype(_BF16)
        vc = jnp.concatenate([v_cur[:, kl], ones], axis=1).astype(_BF16)
        heads = [h * A_GROUP + g for g in range(A_GROUP)]
        q4 = jnp.concatenate(
            [q_rot[:, _head_lanes(hh)] * (_head_rstd(q_raw[:, _head_lanes(hh)]) * (hd ** -0.5))
             for hh in heads], axis=0).astype(_BF16)
        sink = jnp.concatenate([jnp.full((t, 1), sinks_ref[0, hh], _F32) for hh in heads], axis=0)
        s_prev = lax.dot_general(q4, kp, _NT, preferred_element_type=_F32)
        s_cur = lax.dot_general(q4, kc, _NT, preferred_element_type=_F32)
        if use_prev is not None:
            s_prev = jnp.where(use_prev, s_prev, -jnp.inf)
        s = jnp.where(before, s_prev, s_cur)
        m = jnp.maximum(jnp.max(s, axis=-1, keepdims=True), sink)
        p = jnp.exp(s - m)
        acc = (jnp.dot(jnp.where(before, p, 0.0).astype(_BF16), vp, preferred_element_type=_F32)
               + jnp.dot(jnp.where(before, 0.0, p).astype(_BF16), vc, preferred_element_type=_F32))
        den = acc[:, hd:hd + 1] + jnp.exp(sink - m)
        o = acc[:, :hd] / den
        for g, hh in enumerate(heads):
            o_ref[:, _head_lanes(hh)] = o[g * t:(g + 1) * t].astype(o_ref.dtype)


def _attn_prompt_kernel(sinks_ref, q_ref, kv_ref, c_ref, s1_ref, s2_ref, gq_ref, gk_ref,
                        o_ref, kc_ref, vc_ref, kprev, vprev):
    n = pl.program_id(1)

    @pl.when(n == 0)
    def _():
        kprev[...] = jnp.zeros(kprev.shape, _F32)
        vprev[...] = jnp.zeros(vprev.shape, _F32)

    q = q_ref[...]
    kv = kv_ref[...]
    k = kv[:, :A_KV_W]
    v = kv[:, A_KV_W:]
    c, s1, s2 = c_ref[...], s1_ref[...], s2_ref[...]
    q_rot = _rope(q * gq_ref[...], c, s1, s2)
    k_cur = _norm_rope_keys(k, gk_ref[...], c, s1, s2)
    _attend(sinks_ref, q, q_rot, kprev[...], vprev[...], k_cur, v, n > 0, o_ref)
    kprev[...] = k_cur
    vprev[...] = v
    kc_ref[0] = k_cur
    vc_ref[0] = v


def _attn_prompt(a, sinks, gq, gk, tables, layer):
    nb = SEQ // BLK
    c, s1, s2 = tables
    table = pl.BlockSpec((BLK, LANES), lambda b, n: (n, 0))
    cache = pl.BlockSpec((1, BLK, A_KV_W), lambda b, n: (b, 0, 0))
    return pl.pallas_call(
        _attn_prompt_kernel,
        grid=(BATCH, nb),
        in_specs=[pl.BlockSpec((None, 1, A_HEADS), lambda b, n: (layer, 0, 0), memory_space=pltpu.SMEM),
                  pl.BlockSpec((BLK, A_Q_W), lambda b, n: (b * nb + n, 0)),
                  pl.BlockSpec((BLK, 2 * A_KV_W), lambda b, n: (b * nb + n, A_Q_W // (2 * A_KV_W))),
                  table, table, table,
                  _layer_vec(A_Q_W, layer), _layer_vec(A_KV_W, layer)],
        out_specs=[pl.BlockSpec((BLK, A_Q_W), lambda b, n: (b * nb + n, 0)), cache, cache],
        out_shape=[jax.ShapeDtypeStruct((M_PROMPT, A_Q_W), _BF16),
                   jax.ShapeDtypeStruct((BATCH, WINDOW, A_KV_W), _F32),
                   jax.ShapeDtypeStruct((BATCH, WINDOW, A_KV_W), _F32)],
        scratch_shapes=[pltpu.VMEM((BLK, A_KV_W), _F32), pltpu.VMEM((BLK, A_KV_W), _F32)],
        compiler_params=_params(2),
        name="attn_prompt",
    )(sinks, a, a, c, s1, s2, gq, gk)


def _attn_sample_kernel(sinks_ref, qkv_ref, ck_ref, cv_ref, c_ref, s1_ref, s2_ref, gq_ref, gk_ref,
                        o_ref, kc_ref, vc_ref, kbuf, vbuf):
    c, s1, s2 = c_ref[...], s1_ref[...], s2_ref[...]
    tail = jnp.zeros((WINDOW - SAMPLE_PAD, A_KV_W), _F32)
    for s in range(ATTN_SAMPLE_GROUP):
        qkv = qkv_ref[s]
        q = qkv[:, :A_Q_W]
        k = qkv[:, A_Q_W:A_Q_W + A_KV_W]
        v = qkv[:, A_Q_W + A_KV_W:]
        q_rot = _rope(q * gq_ref[...], c, s1, s2)
        k_new = _norm_rope_keys(k, gk_ref[...], c, s1, s2)
        k_old = ck_ref[s]
        v_old = cv_ref[s]
        _attend(sinks_ref, q, q_rot, k_old, v_old, jnp.concatenate([k_new, tail], axis=0),
                jnp.concatenate([v, tail], axis=0), None, o_ref.at[s])
        kbuf[s, 0:WINDOW, :] = k_old
        kbuf[s, WINDOW:, :] = k_new
        vbuf[s, 0:WINDOW, :] = v_old
        vbuf[s, WINDOW:, :] = v
        kc_ref[s] = kbuf[s, DEC_SEQ:DEC_SEQ + WINDOW, :]
        vc_ref[s] = vbuf[s, DEC_SEQ:DEC_SEQ + WINDOW, :]


def _attn_sample(a_sample, cache_k, cache_v, sinks, gq, gk, tables, layer):
    c, s1, s2 = tables
    grp = ATTN_SAMPLE_GROUP
    qkv = jnp.pad(a_sample.reshape(DEC_BATCH, DEC_SEQ, W_A),
                  ((0, 0), (0, SAMPLE_PAD - DEC_SEQ), (0, 0)))
    table = pl.BlockSpec((SAMPLE_PAD, LANES), lambda b: (0, 0))
    cache_in = pl.BlockSpec((None, grp, WINDOW, A_KV_W), lambda b: (layer, b, 0, 0))
    cache_out = pl.BlockSpec((grp, WINDOW, A_KV_W), lambda b: (b, 0, 0))
    o, kc, vc = pl.pallas_call(
        _attn_sample_kernel,
        grid=(DEC_BATCH // grp,),
        in_specs=[pl.BlockSpec((None, 1, A_HEADS), lambda b: (layer, 0, 0), memory_space=pltpu.SMEM),
                  pl.BlockSpec((grp, SAMPLE_PAD, W_A), lambda b: (b, 0, 0)),
                  cache_in, cache_in, table, table, table,
                  _layer_vec(A_Q_W, layer), _layer_vec(A_KV_W, layer)],
        out_specs=[pl.BlockSpec((grp, SAMPLE_PAD, A_Q_W), lambda b: (b, 0, 0)), cache_out, cache_out],
        out_shape=[jax.ShapeDtypeStruct((DEC_BATCH, SAMPLE_PAD, A_Q_W), _BF16),
                   jax.ShapeDtypeStruct((DEC_BATCH, WINDOW, A_KV_W), _F32),
                   jax.ShapeDtypeStruct((DEC_BATCH, WINDOW, A_KV_W), _F32)],
        scratch_shapes=[pltpu.VMEM((grp, WINDOW + SAMPLE_PAD, A_KV_W), _F32),
                        pltpu.VMEM((grp, WINDOW + SAMPLE_PAD, A_KV_W), _F32)],
        compiler_params=_params(1),
        name="attn_sample",
    )(sinks, qkv, cache_k, cache_v, c, s1, s2, gq, gk)
    return o[:, :DEC_SEQ].reshape(M_SAMPLE, A_Q_W), kc, vc


def _lower_bound_kernel(logits_ref, loglb_ref, log1m_ref):
    x = logits_ref[...]
    e = jnp.exp(x - jnp.max(x, axis=0, keepdims=True))
    sm = e / jnp.sum(e, axis=0, keepdims=True)
    acc = sm[0:1]
    rows = [acc]
    for l in range(1, DEPTH):
        acc = acc + sm[l:l + 1]
        rows.append(acc)
    for l in range(DEPTH):
        lb = rows[l] - rows[0]
        loglb_ref[l:l + 1, :] = jnp.log(lb)
        log1m_ref[l:l + 1, :] = jnp.log1p(-lb)


def _lower_bounds(lb_logits):
    shape = jax.ShapeDtypeStruct(lb_logits.shape, _F32)
    return pl.pallas_call(_lower_bound_kernel, out_shape=[shape, shape], name="hgrn_lower_bounds")(lb_logits)


def _chunk_matrices(rows, chunk):
    t = np.arange(rows)[:, None]
    s = np.arange(rows)[None, :]
    same = (t // chunk) == (s // chunk)
    tri = (same & (s <= t)).astype(np.float32)
    ref = (same & ((s % chunk) <= chunk // 2)).astype(np.float32)
    last = same.astype(np.float32)
    return jnp.asarray(np.concatenate([tri, tri - ref, last - tri], axis=0), dtype=_BF16)


def _split3(x):
    hi = x.astype(_BF16)
    r = x - hi.astype(_F32)
    mid = r.astype(_BF16)
    lo = (r - mid.astype(_F32)).astype(_BF16)
    return hi, mid, lo


def _hgrn_gates(q, z, loglb, log1m, lt, rows):
    log_sig = jnp.minimum(z, 0.0) - jnp.log1p(jnp.exp(-jnp.abs(z)))
    b = log1m + log_sig
    log_f = jnp.maximum(loglb, b) + jnp.log1p(jnp.exp(-jnp.abs(loglb - b)))
    kk = -jnp.tanh(0.5 * log_f) * (jnp.exp(log_f) + 1.0)
    hi, mid, lo = _split3(log_f)
    cums = (jnp.dot(lt, hi, preferred_element_type=_F32)
            + jnp.dot(lt, mid, preferred_element_type=_F32)
            + jnp.dot(lt, lo, preferred_element_type=_F32))
    cum = cums[0:rows]
    cum_ref = cums[rows:2 * rows]
    cum_end = cums[2 * rows:3 * rows]
    e_cum = jnp.exp(cum)
    q_intra = q * jnp.exp(cum_ref)
    k_intra = kk * jnp.exp(-cum_ref)
    q_inter = q * e_cum
    k_state = kk * jnp.exp(cum_end)
    return q_intra, k_intra, q_inter, k_state, e_cum


def _hgrn_finish(o, g, w):
    y = o * lax.rsqrt(jnp.mean(o * o, axis=-1, keepdims=True) + EPS) * w
    return (y * (g * _sigmoid(g))).astype(_BF16)


def _hgrn_intra(q_intra, k_intra, v, causal):
    att = lax.dot_general(q_intra.astype(_BF16), k_intra.astype(_BF16), _NT, preferred_element_type=_F32)
    att = jnp.where(causal, att, 0.0)
    return jnp.dot(att.astype(_BF16), v, preferred_element_type=_F32)


def _hgrn_prompt_kernel(q_ref, f_ref, i_ref, g_ref, loglb_ref, log1m_ref, lt_ref, w_ref,
                        o_ref, s_ref, st):
    n = pl.program_id(1)

    @pl.when(n == 0)
    def _():
        st[...] = jnp.zeros(st.shape, _F32)

    q_intra, k_intra, q_inter, k_state, e_cum = _hgrn_gates(
        q_ref[...], f_ref[...], loglb_ref[...], log1m_ref[...], lt_ref[...], BLK)
    row = lax.broadcasted_iota(jnp.int32, (BLK, BLK), 0)
    col = lax.broadcasted_iota(jnp.int32, (BLK, BLK), 1)
    causal = (row // B_CHUNK == col // B_CHUNK) & (col <= row)
    chunks = [slice(c * B_CHUNK, (c + 1) * B_CHUNK) for c in range(BLK // B_CHUNK)]
    w = w_ref[...]
    for h in range(B_HEADS):
        lanes = slice(h * B_KEY_DIM, (h + 1) * B_KEY_DIM)
        v = i_ref[:, lanes].astype(_BF16)
        o = _hgrn_intra(q_intra[:, lanes], k_intra[:, lanes], v, causal)
        ks = k_state[:, lanes].astype(_BF16)
        qi = q_inter[:, lanes].astype(_BF16)
        ec = e_cum[:, lanes]
        updates = [lax.dot_general(v[rows], ks[rows], _TN, preferred_element_type=_F32) for rows in chunks]
        state_t = st[h]
        before = []
        for rows, update in zip(chunks, updates):
            before.append(state_t.astype(_BF16))
            state_t = state_t * ec[rows.stop - 1:rows.stop, :] + update
        st[h] = state_t
        o_inter = [lax.dot_general(qi[rows], s_t, _NT, preferred_element_type=_F32)
                   for rows, s_t in zip(chunks, before)]
        o_ref[:, lanes] = _hgrn_finish(o + jnp.concatenate(o_inter, axis=0), g_ref[:, lanes], w)

    @pl.when(n == pl.num_programs(1) - 1)
    def _():
        for h in range(B_HEADS):
            s_ref[0, h] = st[h].T


def _hgrn_prompt(bm, loglb, log1m, w, layer):
    nb = SEQ // BLK
    gate = lambda j: pl.BlockSpec((BLK, B_W), lambda b, n: (b * nb + n, j))
    return pl.pallas_call(
        _hgrn_prompt_kernel,
        grid=(BATCH, nb),
        in_specs=[gate(0), gate(1), gate(2), gate(3), _layer_vec(B_W, layer), _layer_vec(B_W, layer),
                  pl.BlockSpec((3 * BLK, BLK), lambda b, n: (0, 0)),
                  _layer_vec(B_VAL_DIM, layer)],
        out_specs=[pl.BlockSpec((BLK, B_W), lambda b, n: (b * nb + n, 0)),
                   pl.BlockSpec((1, B_HEADS, B_KEY_DIM, B_VAL_DIM), lambda b, n: (b, 0, 0, 0))],
        out_shape=[jax.ShapeDtypeStruct((M_PROMPT, B_W), _BF16),
                   jax.ShapeDtypeStruct((BATCH, B_HEADS, B_KEY_DIM, B_VAL_DIM), _F32)],
        scratch_shapes=[pltpu.VMEM((B_HEADS, B_VAL_DIM, B_KEY_DIM), _F32)],
        compiler_params=_params(2),
        name="hgrn_prompt",
    )(bm, bm, bm, bm, loglb, log1m, _chunk_matrices(BLK, B_CHUNK), w)


def _hgrn_sample_kernel(q_ref, f_ref, i_ref, g_ref, loglb_ref, log1m_ref, lt_ref, w_ref, s0_ref,
                        o_ref, s_ref):
    rows = HGRN_SAMPLE_GROUP * DEC_SEQ

    def padded(ref):
        return jnp.concatenate([ref[...], jnp.zeros((BLK - rows, B_W), _F32)], axis=0)

    v_all = padded(i_ref)
    g_all = padded(g_ref)
    q_intra, k_intra, q_inter, k_state, e_cum = _hgrn_gates(
        padded(q_ref), padded(f_ref), loglb_ref[...], log1m_ref[...], lt_ref[...], BLK)
    row = lax.broadcasted_iota(jnp.int32, (BLK, BLK), 0)
    col = lax.broadcasted_iota(jnp.int32, (BLK, BLK), 1)
    causal = (row // DEC_SEQ == col // DEC_SEQ) & (col <= row)
    w = w_ref[...]
    for h in range(B_HEADS):
        lanes = slice(h * B_KEY_DIM, (h + 1) * B_KEY_DIM)
        v = v_all[:, lanes].astype(_BF16)
        o = _hgrn_intra(q_intra[:, lanes], k_intra[:, lanes], v, causal)
        qi = q_inter[:, lanes].astype(_BF16)
        ec_t = e_cum[:, lanes].T
        ks_t = k_state[:, lanes].T
        for s in range(HGRN_SAMPLE_GROUP):
            state = s0_ref[s, h]
            o_s = jnp.dot(qi, state.astype(_BF16), preferred_element_type=_F32)
            o = o + jnp.where(row // DEC_SEQ == s, o_s, 0.0)
            ks_seq = jnp.where(col // DEC_SEQ == s, ks_t, 0.0).astype(_BF16)
            update = jnp.dot(ks_seq, v, preferred_element_type=_F32)
            decay = ec_t[:, (s + 1) * DEC_SEQ - 1:(s + 1) * DEC_SEQ]
            s_ref[s, h] = state * decay + update
        o_ref[:, lanes] = _hgrn_finish(o, g_all[:, lanes], w)[0:rows]


def _hgrn_sample(bm, state, loglb, log1m, w, layer):
    rows = HGRN_SAMPLE_GROUP * DEC_SEQ
    first = M_PROMPT // rows
    gate = lambda j: pl.BlockSpec((rows, B_W), lambda s: (first + s, j))
    state_shape = (HGRN_SAMPLE_GROUP, B_HEADS, B_KEY_DIM, B_VAL_DIM)
    return pl.pallas_call(
        _hgrn_sample_kernel,
        grid=(DEC_BATCH // HGRN_SAMPLE_GROUP,),
        in_specs=[gate(0), gate(1), gate(2), gate(3), _layer_vec(B_W, layer), _layer_vec(B_W, layer),
                  pl.BlockSpec((3 * BLK, BLK), lambda s: (0, 0)),
                  _layer_vec(B_VAL_DIM, layer),
                  pl.BlockSpec((None,) + state_shape, lambda s: (layer, s, 0, 0, 0))],
        out_specs=[pl.BlockSpec((rows, B_W), lambda s: (s, 0)),
                   pl.BlockSpec(state_shape, lambda s: (s, 0, 0, 0))],
        out_shape=[jax.ShapeDtypeStruct((M_SAMPLE, B_W), _BF16),
                   jax.ShapeDtypeStruct(state.shape[1:], _F32)],
        compiler_params=_params(1),
        name="hgrn_sample",
    )(bm, bm, bm, bm, loglb, log1m, _chunk_matrices(BLK, DEC_SEQ), w, state)


def _sgu_kernel(u_ref, v_ref, w_ref, b_ref, g_ref, o_ref, vn_ref):
    is_sample = pl.program_id(0) == pl.num_programs(0) - 1
    row = lax.broadcasted_iota(jnp.int32, (BLK, BLK), 0)
    col = lax.broadcasted_iota(jnp.int32, (BLK, BLK), 1)
    same_seq = jnp.logical_or(jnp.logical_not(is_sample), row // DEC_SEQ == col // DEC_SEQ)
    causal = (col <= row) & same_seq
    gain = g_ref[...]
    bias = b_ref[...]
    for g in range(C_GROUPS):
        lanes = slice(g * C_GROUP_DIM, (g + 1) * C_GROUP_DIM)
        v = v_ref[:, lanes]
        vn = v * lax.rsqrt(jnp.mean(v * v, axis=-1, keepdims=True) + EPS) * gain
        w = jnp.where(causal, w_ref[g], 0.0).astype(_BF16)
        z = jnp.dot(w, vn.astype(_BF16), preferred_element_type=_F32) + bias[:, g:g + 1]
        o_ref[:, lanes] = (u_ref[:, lanes] * z).astype(_BF16)
        vn_ref[:, lanes] = vn


def _sgu(cm, w2, b2, gain, layer):
    nblk = M_ALL // BLK
    which = lambda i: i // (nblk - 1)
    return pl.pallas_call(
        _sgu_kernel,
        grid=(nblk,),
        in_specs=[pl.BlockSpec((BLK, C_W), lambda i: (i, 0)),
                  pl.BlockSpec((BLK, C_W), lambda i: (i, 1)),
                  pl.BlockSpec((None, None, C_GROUPS, C_CHUNK, C_CHUNK), lambda i: (layer, which(i), 0, 0, 0)),
                  pl.BlockSpec((None, None, C_CHUNK, C_GROUPS), lambda i: (layer, which(i), 0, 0)),
                  _layer_vec(C_GROUP_DIM, layer)],
        out_specs=[pl.BlockSpec((BLK, C_W), lambda i: (i, 0)),
                   pl.BlockSpec((BLK, C_W), lambda i: (0, 0))],
        out_shape=[jax.ShapeDtypeStruct((M_ALL, C_W), _BF16),
                   jax.ShapeDtypeStruct((M_SAMPLE, C_W), _F32)],
        compiler_params=_params(1),
        name="sgu",
    )(cm, cm, w2, b2, gain)


def _sgu_params(w_spatial, b_spatial):
    reps = BLK // DEC_SEQ
    w_sample = jnp.tile(w_spatial[:, :, :DEC_SEQ, :DEC_SEQ], (1, 1, reps, reps))
    b_sample = jnp.tile(b_spatial[:, :, :DEC_SEQ], (1, 1, reps))
    w2 = jnp.stack([w_spatial, w_sample], axis=1)
    b2 = jnp.stack([jnp.swapaxes(b_spatial, 1, 2), jnp.swapaxes(b_sample, 1, 2)], axis=1)
    return w2, b2


def kernel(x_prompt, x_sample, cache_k, cache_v, state_hgrn, norm_mix, w_in, q_norm, k_norm, sinks,
           lb_logits, hgrn_out_norm, sgu_v_norm, w_spatial, b_spatial, w_branch_a, w_branch_b,
           w_branch_c, w_out, norm_ffn, w_ffn_up, w_ffn_down):
    x = jnp.concatenate([x_prompt.reshape(M_PROMPT, D_MODEL), x_sample.reshape(M_SAMPLE, D_MODEL)], axis=0)
    loglb, log1m = _lower_bounds(lb_logits)
    loglb = loglb.reshape(DEPTH, 1, B_W)
    log1m = log1m.reshape(DEPTH, 1, B_W)
    tables_prompt = _rope_tables(0, SEQ)
    tables_sample = _rope_tables(PAST_LEN, SAMPLE_PAD)
    gq = jnp.tile(q_norm, (1, A_HEADS)).reshape(DEPTH, 1, A_Q_W)
    gk = jnp.tile(k_norm, (1, A_KV_HEADS)).reshape(DEPTH, 1, A_KV_W)
    w_hg = hgrn_out_norm.reshape(DEPTH, 1, B_VAL_DIM)
    w_sg = sgu_v_norm.reshape(DEPTH, 1, C_GROUP_DIM)
    g_mix = norm_mix.reshape(DEPTH, 1, D_MODEL)
    g_ffn = norm_ffn.reshape(DEPTH, 1, D_MODEL)
    sinks = sinks.reshape(DEPTH, 1, A_HEADS)
    w2, b2 = _sgu_params(w_spatial, b_spatial)
    ck_all = cache_k.reshape(DEPTH, DEC_BATCH, WINDOW, A_KV_W)
    cv_all = cache_v.reshape(DEPTH, DEC_BATCH, WINDOW, A_KV_W)

    kp_l, vp_l, sp_l, ks_l, vs_l, ss_l, cs_l = [], [], [], [], [], [], []
    for l in range(DEPTH):
        h = _rmsnorm(x, g_mix, l)
        am = _matmul_cols(h, w_in, l, OFF_A, W_A)
        bm = _matmul_cols(h, w_in, l, OFF_B, W_B)
        cm = _matmul_cols(h, w_in, l, OFF_C, W_C)
        gates = _matmul_cols(h, w_in, l, OFF_G, W_G)

        oa_p, kc_p, vc_p = _attn_prompt(am, sinks, gq, gk, tables_prompt, l)
        oa_s, kc_s, vc_s = _attn_sample(am[M_PROMPT:], ck_all, cv_all, sinks, gq, gk, tables_sample, l)
        ob_p, st_p = _hgrn_prompt(bm, loglb, log1m, w_hg, l)
        ob_s, st_s = _hgrn_sample(bm, state_hgrn, loglb, log1m, w_hg, l)
        oc, vn_s = _sgu(cm, w2, b2, w_sg, l)

        oa = jnp.concatenate([oa_p, oa_s], axis=0)
        ob = jnp.concatenate([ob_p, ob_s], axis=0)
        merged = _merge(oa, ob, oc, gates, w_branch_a, w_branch_b, w_branch_c, l)
        x = _matmul_residual(merged, w_out, l, x, TM, TN)
        h2 = _rmsnorm(x, g_ffn, l)
        act = _ffn_up(h2, w_ffn_up, l)
        x = _matmul_residual(act, w_ffn_down, l, x, TM_DOWN, TN_NARROW)

        kp_l.append(kc_p.reshape(BATCH, WINDOW, A_KV_HEADS, A_HEAD_DIM))
        vp_l.append(vc_p.reshape(BATCH, WINDOW, A_KV_HEADS, A_HEAD_DIM))
        sp_l.append(st_p)
        ks_l.append(kc_s.reshape(DEC_BATCH, WINDOW, A_KV_HEADS, A_HEAD_DIM))
        vs_l.append(vc_s.reshape(DEC_BATCH, WINDOW, A_KV_HEADS, A_HEAD_DIM))
        ss_l.append(st_s)
        cs_l.append(vn_s.reshape(DEC_BATCH, DEC_SEQ, C_GROUPS, C_GROUP_DIM))

    y_prompt = x[:M_PROMPT].reshape(BATCH, SEQ, D_MODEL)
    y_sample = x[M_PROMPT:].reshape(DEC_BATCH, DEC_SEQ, D_MODEL)
    return (y_prompt, y_sample, jnp.stack(kp_l), jnp.stack(vp_l), jnp.stack(sp_l),
            jnp.stack(ks_l), jnp.stack(vs_l), jnp.stack(ss_l), jnp.stack(cs_l))
```

```python
import numpy as np
import jax
import jax.numpy as jnp
from jax import lax
from jax.experimental import pallas as pl
from jax.experimental.pallas import tpu as pltpu

D_MODEL = 2048
BATCH = 4
SEQ = 2048
DEPTH = 4
DEC_BATCH = 32
DEC_SEQ = 4
PAST_LEN = 16384

A_HEADS = 16
A_KV_HEADS = 4
A_HEAD_DIM = 64
A_GROUP = A_HEADS // A_KV_HEADS
WINDOW = 128
ROT_DIM = A_HEAD_DIM // 4
ROT_HALF = ROT_DIM // 2
ROPE_THETA = 500000.0
B_HEADS = 8
B_KEY_DIM = 128
B_VAL_DIM = 128
B_CHUNK = 16
C_GROUPS = 8
C_GROUP_DIM = 128
C_CHUNK = 128
A_Q_W = A_HEADS * A_HEAD_DIM
A_KV_W = A_KV_HEADS * A_HEAD_DIM
B_W = B_HEADS * B_KEY_DIM
C_W = C_GROUPS * C_GROUP_DIM
FFN_DIM = ((8 * D_MODEL + 3 * 256 - 1) // (3 * 256)) * 256
EPS = 1e-6

M_PROMPT = BATCH * SEQ
M_SAMPLE = DEC_BATCH * DEC_SEQ
M_ALL = M_PROMPT + M_SAMPLE

OFF_A = 0
W_A = A_Q_W + 2 * A_KV_W
OFF_B = OFF_A + W_A
W_B = 4 * B_W
OFF_C = OFF_B + W_B
W_C = 2 * C_W
OFF_G = OFF_C + W_C
W_G = 3 * D_MODEL

LANES = 128
BLK = 128
TM = M_ALL // 4
TM_DOWN = M_ALL // 8
TM_NORM = M_ALL // 16
TN = 512
TN_NARROW = 256
SAMPLE_PAD = 16
ATTN_SAMPLE_GROUP = 4
HGRN_SAMPLE_GROUP = 8
VMEM_LIMIT = 56 * 1024 * 1024

_BF16 = jnp.bfloat16
_F32 = jnp.float32
_NT = (((1,), (1,)), ((), ()))
_TN = (((0,), (0,)), ((), ()))


def _params(n_grid):
    return pltpu.CompilerParams(dimension_semantics=("arbitrary",) * n_grid,
                                vmem_limit_bytes=VMEM_LIMIT)


def _sigmoid(x):
    return 1.0 / (1.0 + jnp.exp(-x))


def _layer_vec(width, layer):
    return pl.BlockSpec((None, 1, width), lambda *_: (layer, 0, 0))


def _rmsnorm_kernel(x_ref, g_ref, o_ref):
    x = x_ref[...]
    y = x * lax.rsqrt(jnp.mean(x * x, axis=-1, keepdims=True) + EPS)
    o_ref[...] = (y * g_ref[...]).astype(_BF16)


def _rmsnorm(x, g, layer):
    m, d = x.shape
    return pl.pallas_call(
        _rmsnorm_kernel,
        grid=(m // TM_NORM,),
        in_specs=[pl.BlockSpec((TM_NORM, d), lambda i: (i, 0)), _layer_vec(d, layer)],
        out_specs=pl.BlockSpec((TM_NORM, d), lambda i: (i, 0)),
        out_shape=jax.ShapeDtypeStruct((m, d), _BF16),
        compiler_params=_params(1),
        name="rmsnorm",
    )(x, g)


def _mm_kernel(a_ref, w_ref, o_ref):
    o_ref[...] = jnp.dot(a_ref[...], w_ref[...].astype(_BF16), preferred_element_type=_F32)


def _matmul_cols(a, w, layer, col_off, n_cols):
    m, k = a.shape
    off = col_off // TN
    return pl.pallas_call(
        _mm_kernel,
        grid=(m // TM, n_cols // TN),
        in_specs=[pl.BlockSpec((TM, k), lambda i, j: (i, 0)),
                  pl.BlockSpec((None, k, TN), lambda i, j: (layer, 0, j + off))],
        out_specs=pl.BlockSpec((TM, TN), lambda i, j: (i, j)),
        out_shape=jax.ShapeDtypeStruct((m, n_cols), _F32),
        compiler_params=_params(2),
        name="proj_in",
    )(a, w)


def _mm_res_kernel(a_ref, w_ref, r_ref, o_ref):
    o_ref[...] = r_ref[...] + jnp.dot(a_ref[...], w_ref[...].astype(_BF16),
                                      preferred_element_type=_F32)


def _matmul_residual(a, w, layer, r, tm, tn):
    m, k = a.shape
    n = w.shape[2]
    return pl.pallas_call(
        _mm_res_kernel,
        grid=(m // tm, n // tn),
        in_specs=[pl.BlockSpec((tm, k), lambda i, j: (i, 0)),
                  pl.BlockSpec((None, k, tn), lambda i, j: (layer, 0, j)),
                  pl.BlockSpec((tm, tn), lambda i, j: (i, j))],
        out_specs=pl.BlockSpec((tm, tn), lambda i, j: (i, j)),
        out_shape=jax.ShapeDtypeStruct((m, n), _F32),
        compiler_params=_params(2),
        name="proj_residual",
    )(a, w, r)


def _ffn_up_kernel(a_ref, wg_ref, wu_ref, o_ref):
    a = a_ref[...]
    g = jnp.dot(a, wg_ref[...].astype(_BF16), preferred_element_type=_F32)
    u = jnp.dot(a, wu_ref[...].astype(_BF16), preferred_element_type=_F32)
    o_ref[...] = (g * _sigmoid(g) * u).astype(_BF16)


def _ffn_up(a, w_up, layer):
    m, k = a.shape
    tn = TN_NARROW
    nj = FFN_DIM // tn
    return pl.pallas_call(
        _ffn_up_kernel,
        grid=(m // TM, nj),
        in_specs=[pl.BlockSpec((TM, k), lambda i, j: (i, 0)),
                  pl.BlockSpec((None, k, tn), lambda i, j: (layer, 0, j)),
                  pl.BlockSpec((None, k, tn), lambda i, j: (layer, 0, j + nj))],
        out_specs=pl.BlockSpec((TM, tn), lambda i, j: (i, j)),
        out_shape=jax.ShapeDtypeStruct((m, FFN_DIM), _BF16),
        compiler_params=_params(2),
        name="ffn_up",
    )(a, w_up, w_up)


def _merge_kernel(oa_ref, ob_ref, oc_ref, wa_ref, wb_ref, wc_ref, ga_ref, gb_ref, gc_ref, o_ref):
    ya = jnp.dot(oa_ref[...], wa_ref[...].astype(_BF16), preferred_element_type=_F32)
    yb = jnp.dot(ob_ref[...], wb_ref[...].astype(_BF16), preferred_element_type=_F32)
    yc = jnp.dot(oc_ref[...], wc_ref[...].astype(_BF16), preferred_element_type=_F32)
    merged = _sigmoid(ga_ref[...]) * ya + _sigmoid(gb_ref[...]) * yb + _sigmoid(gc_ref[...]) * yc
    o_ref[...] = merged.astype(_BF16)


def _merge(oa, ob, oc, gates, wa, wb, wc, layer):
    m, k = oa.shape
    tn = TN_NARROW
    nj = D_MODEL // tn
    branch = pl.BlockSpec((TM, k), lambda i, j: (i, 0))
    weight = pl.BlockSpec((None, k, tn), lambda i, j: (layer, 0, j))
    return pl.pallas_call(
        _merge_kernel,
        grid=(m // TM, nj),
        in_specs=[branch, branch, branch, weight, weight, weight,
                  pl.BlockSpec((TM, tn), lambda i, j: (i, j)),
                  pl.BlockSpec((TM, tn), lambda i, j: (i, j + nj)),
                  pl.BlockSpec((TM, tn), lambda i, j: (i, j + 2 * nj))],
        out_specs=pl.BlockSpec((TM, tn), lambda i, j: (i, j)),
        out_shape=jax.ShapeDtypeStruct((m, D_MODEL), _BF16),
        compiler_params=_params(2),
        name="merge",
    )(oa, ob, oc, wa, wb, wc, gates, gates, gates)


def _rope_tables(p0, rows):
    pos = (p0 + jnp.arange(rows, dtype=jnp.int32)).astype(_F32)
    inv_freq = jnp.power(jnp.float32(ROPE_THETA), -jnp.arange(ROT_HALF, dtype=_F32) / ROT_HALF)
    ang = pos[:, None] * inv_freq[None, :]
    cos, sin = jnp.cos(ang), jnp.sin(ang)
    rest = A_HEAD_DIM - ROT_DIM
    zeros = jnp.zeros((rows, ROT_HALF), _F32)
    pad = jnp.zeros((rows, rest), _F32)
    c = jnp.concatenate([cos, cos, jnp.ones((rows, rest), _F32)], axis=1)
    s1 = jnp.concatenate([-sin, zeros, pad], axis=1)
    s2 = jnp.concatenate([zeros, sin, pad], axis=1)
    return tuple(jnp.tile(t, (1, LANES // A_HEAD_DIM)) for t in (c, s1, s2))


def _rope(x, c, s1, s2):
    width = x.shape[1]
    reps = width // c.shape[1]
    c, s1, s2 = (jnp.concatenate([t] * reps, axis=1) for t in (c, s1, s2))
    ahead = pltpu.roll(x, width - ROT_HALF, axis=1)
    behind = pltpu.roll(x, ROT_HALF, axis=1)
    return x * c + ahead * s1 + behind * s2


def _head_mean_matrix():
    i = np.arange(LANES)
    same = (i[:, None] // A_HEAD_DIM) == (i[None, :] // A_HEAD_DIM)
    return jnp.asarray(same.astype(np.float32) / A_HEAD_DIM, dtype=_BF16)


def _qk_prep(x, gain, head_mean, c, s1, s2, scale):
    sq = x * x
    hi = sq.astype(_BF16)
    lo = (sq - hi.astype(_F32)).astype(_BF16)
    ms = jnp.concatenate(
        [jnp.dot(hi[:, l:l + LANES], head_mean, preferred_element_type=_F32)
         + jnp.dot(lo[:, l:l + LANES], head_mean, preferred_element_type=_F32)
         for l in range(0, x.shape[1], LANES)], axis=1)
    return _rope(x * gain, c, s1, s2) * (lax.rsqrt(ms + EPS) * scale)


def _low_half(shape):
    return lax.broadcasted_iota(jnp.int32, shape, 1) < A_HEAD_DIM


def _both_halves(col, half):
    low = _low_half(col.shape)
    sel = jnp.where(low if half == 0 else jnp.logical_not(low), col, 0.0)
    return sel + pltpu.roll(sel, A_HEAD_DIM, axis=1)


def _kv_pairs(k, v, h):
    pair, half = divmod(h, 2)
    lanes = slice(pair * LANES, (pair + 1) * LANES)
    return _both_halves(k[:, lanes], half), _both_halves(v[:, lanes], half)


def _mxu_tiles(k_pair, v_pair):
    v_pair = v_pair.astype(_BF16)
    return k_pair.astype(_BF16), jnp.concatenate([v_pair, jnp.ones(v_pair.shape, _BF16)], axis=1)


def _kv_tiles(k, v, h):
    return _mxu_tiles(*_kv_pairs(k, v, h))


def _query_rows(qn, h):
    low = _low_half((qn.shape[0], LANES))
    parts = []
    for g in range(A_GROUP):
        pair, half = divmod(h * A_GROUP + g, 2)
        keep = low if half == 0 else jnp.logical_not(low)
        parts.append(jnp.where(keep, qn[:, pair * LANES:(pair + 1) * LANES], 0.0))
    return jnp.concatenate(parts, axis=0).astype(_BF16)


def _sink_rows(sinks_ref, h, t):
    return jnp.concatenate([jnp.full((t, LANES), sinks_ref[0, h * A_GROUP + g], _F32)
                            for g in range(A_GROUP)], axis=0)


def _store_heads(o_ref, o, h, t):
    low = _low_half((t, LANES))
    for j in range(A_GROUP // 2):
        even = o[(2 * j) * t:(2 * j + 1) * t]
        odd = o[(2 * j + 1) * t:(2 * j + 2) * t]
        pair = (h * A_GROUP) // 2 + j
        o_ref[:, pair * LANES:(pair + 1) * LANES] = jnp.where(low, even, odd).astype(o_ref.dtype)


def _attn_prompt_kernel(sinks_ref, q_ref, kv_ref, c_ref, s1_ref, s2_ref, gq_ref, gk_ref, hm_ref,
                        o_ref, kc_ref, vc_ref, kprev, vprev):
    n = pl.program_id(1)

    @pl.when(n == 0)
    def _():
        kprev[...] = jnp.zeros(kprev.shape, _F32)
        vprev[...] = jnp.zeros(vprev.shape, _F32)

    kv = kv_ref[...]
    v = kv[:, A_KV_W:]
    c, s1, s2 = c_ref[...], s1_ref[...], s2_ref[...]
    qn = _qk_prep(q_ref[...], gq_ref[...], hm_ref[...], c, s1, s2, A_HEAD_DIM ** -0.5)
    kn = _qk_prep(kv[:, :A_KV_W], gk_ref[...], hm_ref[...], c, s1, s2, 1.0)
    kc_ref[0] = kn
    vc_ref[0] = v

    rows = A_GROUP * BLK
    row = lax.broadcasted_iota(jnp.int32, (rows, WINDOW), 0) % BLK
    col = lax.broadcasted_iota(jnp.int32, (rows, WINDOW), 1)
    before = col > row
    has_prev = n > 0
    for h in range(A_KV_HEADS):
        k_pair, v_pair = _kv_pairs(kn, v, h)
        k_tile, v_tile = _mxu_tiles(k_pair, v_pair)
        k_before, v_before = _mxu_tiles(kprev[h], vprev[h])
        q4 = _query_rows(qn, h)
        s_prev = lax.dot_general(q4, k_before, _NT, preferred_element_type=_F32)
        s_cur = lax.dot_general(q4, k_tile, _NT, preferred_element_type=_F32)
        s = jnp.where(before, jnp.where(has_prev, s_prev, -jnp.inf), s_cur)
        sink = _sink_rows(sinks_ref, h, BLK)
        m = jnp.maximum(jnp.broadcast_to(jnp.max(s, axis=-1, keepdims=True), s.shape), sink)
        p = jnp.exp(s - m)
        acc = (jnp.dot(jnp.where(before, p, 0.0).astype(_BF16), v_before, preferred_element_type=_F32)
               + jnp.dot(jnp.where(before, 0.0, p).astype(_BF16), v_tile, preferred_element_type=_F32))
        o = acc[:, :LANES] / (acc[:, LANES:] + jnp.exp(sink - m))
        _store_heads(o_ref, o, h, BLK)
        kprev[h] = k_pair
        vprev[h] = v_pair


def _attn_prompt(a, sinks, gq, gk, tables, layer):
    nb = SEQ // BLK
    c, s1, s2 = tables
    table = pl.BlockSpec((BLK, LANES), lambda b, n: (n, 0))
    cache = pl.BlockSpec((1, BLK, A_KV_W), lambda b, n: (b, 0, 0))
    const = lambda w: pl.BlockSpec((w, w), lambda b, n: (0, 0))
    return pl.pallas_call(
        _attn_prompt_kernel,
        grid=(BATCH, nb),
        in_specs=[pl.BlockSpec((None, 1, A_HEADS), lambda b, n: (layer, 0, 0), memory_space=pltpu.SMEM),
                  pl.BlockSpec((BLK, A_Q_W), lambda b, n: (b * nb + n, 0)),
                  pl.BlockSpec((BLK, 2 * A_KV_W), lambda b, n: (b * nb + n, A_Q_W // (2 * A_KV_W))),
                  table, table, table,
                  _layer_vec(A_Q_W, layer), _layer_vec(A_KV_W, layer), const(LANES)],
        out_specs=[pl.BlockSpec((BLK, A_Q_W), lambda b, n: (b * nb + n, 0)), cache, cache],
        out_shape=[jax.ShapeDtypeStruct((M_ALL, A_Q_W), _BF16),
                   jax.ShapeDtypeStruct((BATCH, WINDOW, A_KV_W), _F32),
                   jax.ShapeDtypeStruct((BATCH, WINDOW, A_KV_W), _F32)],
        scratch_shapes=[pltpu.VMEM((A_KV_HEADS, BLK, LANES), _F32),
                        pltpu.VMEM((A_KV_HEADS, BLK, LANES), _F32)],
        compiler_params=_params(2),
        name="attn_prompt",
    )(sinks, a, a, c, s1, s2, gq, gk, _head_mean_matrix())


def _attn_sample_kernel(sinks_ref, qkv_ref, ck_ref, cv_ref, c_ref, s1_ref, s2_ref, gq_ref, gk_ref,
                        hm_ref, o_full_ref, o_ref, kc_ref, vc_ref, kbuf, vbuf, knew, vnew):
    del o_full_ref
    grp = ATTN_SAMPLE_GROUP
    t = grp * DEC_SEQ
    qkv = qkv_ref[...]
    v = qkv[:, A_Q_W + A_KV_W:]
    c, s1, s2 = c_ref[...], s1_ref[...], s2_ref[...]
    qn = _qk_prep(qkv[:, :A_Q_W], gq_ref[...], hm_ref[...], c, s1, s2, A_HEAD_DIM ** -0.5)
    kn = _qk_prep(qkv[:, A_Q_W:A_Q_W + A_KV_W], gk_ref[...], hm_ref[...], c, s1, s2, 1.0)

    knew[...] = jnp.zeros(knew.shape, _F32)
    vnew[...] = jnp.zeros(vnew.shape, _F32)
    knew[0:t, :] = kn
    vnew[0:t, :] = v
    for s in range(grp):
        kbuf[s, 0:WINDOW, :] = ck_ref[s]
        vbuf[s, 0:WINDOW, :] = cv_ref[s]
        kbuf[s, WINDOW:, :] = knew[s * DEC_SEQ:s * DEC_SEQ + 8, :]
        vbuf[s, WINDOW:, :] = vnew[s * DEC_SEQ:s * DEC_SEQ + 8, :]
        kc_ref[s] = kbuf[s, DEC_SEQ:DEC_SEQ + WINDOW, :]
        vc_ref[s] = vbuf[s, DEC_SEQ:DEC_SEQ + WINDOW, :]

    rows = A_GROUP * t
    r_old = lax.broadcasted_iota(jnp.int32, (rows, grp * WINDOW), 0) % t
    c_old = lax.broadcasted_iota(jnp.int32, (rows, grp * WINDOW), 1)
    see_old = (c_old // WINDOW == r_old // DEC_SEQ) & (c_old % WINDOW > r_old % DEC_SEQ)
    r_new = lax.broadcasted_iota(jnp.int32, (rows, WINDOW), 0) % t
    c_new = lax.broadcasted_iota(jnp.int32, (rows, WINDOW), 1)
    see_new = (c_new < t) & (c_new // DEC_SEQ == r_new // DEC_SEQ) & (c_new % DEC_SEQ <= r_new % DEC_SEQ)
    pad_k = jnp.zeros((WINDOW - t, LANES), _BF16)
    pad_v = jnp.zeros((WINDOW - t, 2 * LANES), _BF16)
    for h in range(A_KV_HEADS):
        old = [_kv_tiles(ck_ref[s], cv_ref[s], h) for s in range(grp)]
        k_old = jnp.concatenate([kt for kt, _ in old], axis=0)
        v_old = jnp.concatenate([vt for _, vt in old], axis=0)
        k_new, v_new = _kv_tiles(kn, v, h)
        k_new = jnp.concatenate([k_new, pad_k], axis=0)
        v_new = jnp.concatenate([v_new, pad_v], axis=0)
        q4 = _query_rows(qn, h)
        s_old = jnp.where(see_old, lax.dot_general(q4, k_old, _NT, preferred_element_type=_F32), -jnp.inf)
        s_new = jnp.where(see_new, lax.dot_general(q4, k_new, _NT, preferred_element_type=_F32), -jnp.inf)
        sink = _sink_rows(sinks_ref, h, t)
        top = jnp.maximum(jnp.max(s_old, axis=-1, keepdims=True), jnp.max(s_new, axis=-1, keepdims=True))
        m = jnp.maximum(jnp.broadcast_to(top, sink.shape), sink)
        p_old = jnp.exp(s_old - jnp.concatenate([m] * grp, axis=1))
        p_new = jnp.exp(s_new - m)
        acc = (jnp.dot(p_old.astype(_BF16), v_old, preferred_element_type=_F32)
               + jnp.dot(p_new.astype(_BF16), v_new, preferred_element_type=_F32))
        o = acc[:, :LANES] / (acc[:, LANES:] + jnp.exp(sink - m))
        _store_heads(o_ref, o, h, t)


def _attn_sample(a, o_full, cache_k, cache_v, sinks, gq, gk, tables, layer):
    c, s1, s2 = tables
    grp = ATTN_SAMPLE_GROUP
    t = grp * DEC_SEQ
    first = M_PROMPT // t
    table = pl.BlockSpec((t, LANES), lambda b: (0, 0))
    cache_in = pl.BlockSpec((None, grp, WINDOW, A_KV_W), lambda b: (layer, b, 0, 0))
    cache_out = pl.BlockSpec((grp, WINDOW, A_KV_W), lambda b: (b, 0, 0))
    const = lambda w: pl.BlockSpec((w, w), lambda b: (0, 0))
    return pl.pallas_call(
        _attn_sample_kernel,
        grid=(DEC_BATCH // grp,),
        in_specs=[pl.BlockSpec((None, 1, A_HEADS), lambda b: (layer, 0, 0), memory_space=pltpu.SMEM),
                  pl.BlockSpec((t, W_A), lambda b: (first + b, 0)),
                  cache_in, cache_in, table, table, table,
                  _layer_vec(A_Q_W, layer), _layer_vec(A_KV_W, layer), const(LANES),
                  pl.BlockSpec(memory_space=pl.ANY)],
        out_specs=[pl.BlockSpec((t, A_Q_W), lambda b: (first + b, 0)), cache_out, cache_out],
        out_shape=[jax.ShapeDtypeStruct((M_ALL, A_Q_W), _BF16),
                   jax.ShapeDtypeStruct((DEC_BATCH, WINDOW, A_KV_W), _F32),
                   jax.ShapeDtypeStruct((DEC_BATCH, WINDOW, A_KV_W), _F32)],
        scratch_shapes=[pltpu.VMEM((grp, WINDOW + 8, A_KV_W), _F32),
                        pltpu.VMEM((grp, WINDOW + 8, A_KV_W), _F32),
                        pltpu.VMEM((t + 8, A_KV_W), _F32), pltpu.VMEM((t + 8, A_KV_W), _F32)],
        input_output_aliases={10: 0},
        compiler_params=_params(1),
        name="attn_sample",
    )(sinks, a, cache_k, cache_v, c, s1, s2, gq, gk, _head_mean_matrix(),
      o_full)


def _lower_bound_kernel(logits_ref, loglb_ref, log1m_ref):
    x = logits_ref[...]
    e = jnp.exp(x - jnp.max(x, axis=0, keepdims=True))
    sm = e / jnp.sum(e, axis=0, keepdims=True)
    acc = sm[0:1]
    rows = [acc]
    for l in range(1, DEPTH):
        acc = acc + sm[l:l + 1]
        rows.append(acc)
    for l in range(DEPTH):
        lb = rows[l] - rows[0]
        loglb_ref[l:l + 1, :] = jnp.log(lb)
        log1m_ref[l:l + 1, :] = jnp.log1p(-lb)


def _lower_bounds(lb_logits):
    shape = jax.ShapeDtypeStruct(lb_logits.shape, _F32)
    return pl.pallas_call(_lower_bound_kernel, out_shape=[shape, shape], name="hgrn_lower_bounds")(lb_logits)


def _chunk_matrices(rows, chunk):
    t = np.arange(rows)[:, None]
    s = np.arange(rows)[None, :]
    same = (t // chunk) == (s // chunk)
    tri = (same & (s <= t)).astype(np.float32)
    ref = (same & ((s % chunk) <= chunk // 2)).astype(np.float32)
    last = same.astype(np.float32)
    return jnp.asarray(np.concatenate([tri, tri - ref, last - tri], axis=0), dtype=_BF16)


def _split3(x):
    hi = x.astype(_BF16)
    r = x - hi.astype(_F32)
    mid = r.astype(_BF16)
    lo = (r - mid.astype(_F32)).astype(_BF16)
    return hi, mid, lo


def _hgrn_gates(q, z, loglb, log1m, lt, rows):
    log_sig = jnp.minimum(z, 0.0) - jnp.log(1.0 + jnp.exp(-jnp.abs(z)))
    b = log1m + log_sig
    log_f = jnp.maximum(loglb, b) + jnp.log(1.0 + jnp.exp(-jnp.abs(loglb - b)))
    kk = -jnp.tanh(0.5 * log_f) * (jnp.exp(log_f) + 1.0)
    hi, mid, lo = _split3(log_f)
    cums = (jnp.dot(lt, hi, preferred_element_type=_F32)
            + jnp.dot(lt, mid, preferred_element_type=_F32)
            + jnp.dot(lt, lo, preferred_element_type=_F32))
    cum = cums[0:rows]
    cum_ref = cums[rows:2 * rows]
    cum_end = cums[2 * rows:3 * rows]
    e_cum = jnp.exp(cum)
    q_intra = q * jnp.exp(cum_ref)
    k_intra = kk * jnp.exp(-cum_ref)
    q_inter = q * e_cum
    k_state = kk * jnp.exp(cum_end)
    return q_intra, k_intra, q_inter, k_state, e_cum


def _hgrn_finish(o, g, w):
    y = o * lax.rsqrt(jnp.mean(o * o, axis=-1, keepdims=True) + EPS) * w
    return (y * (g * _sigmoid(g))).astype(_BF16)


def _hgrn_intra(q_intra, k_intra, v, causal):
    att = lax.dot_general(q_intra.astype(_BF16), k_intra.astype(_BF16), _NT, preferred_element_type=_F32)
    att = jnp.where(causal, att, 0.0)
    return jnp.dot(att.astype(_BF16), v, preferred_element_type=_F32)


def _hgrn_prompt_kernel(q_ref, f_ref, i_ref, g_ref, loglb_ref, log1m_ref, lt_ref, w_ref,
                        o_ref, s_ref, st):
    n = pl.program_id(1)

    @pl.when(n == 0)
    def _():
        st[...] = jnp.zeros(st.shape, _F32)

    q_intra, k_intra, q_inter, k_state, e_cum = _hgrn_gates(
        q_ref[...], f_ref[...], loglb_ref[...], log1m_ref[...], lt_ref[...], BLK)
    row = lax.broadcasted_iota(jnp.int32, (BLK, BLK), 0)
    col = lax.broadcasted_iota(jnp.int32, (BLK, BLK), 1)
    causal = (row // B_CHUNK == col // B_CHUNK) & (col <= row)
    chunks = [slice(c * B_CHUNK, (c + 1) * B_CHUNK) for c in range(BLK // B_CHUNK)]
    w = w_ref[...]
    heads = [slice(h * B_KEY_DIM, (h + 1) * B_KEY_DIM) for h in range(B_HEADS)]
    intra, updates = [], []
    for lanes in heads:
        v = i_ref[:, lanes].astype(_BF16)
        ks = k_state[:, lanes].astype(_BF16)
        intra.append(_hgrn_intra(q_intra[:, lanes], k_intra[:, lanes], v, causal))
        updates.append([lax.dot_general(v[rows], ks[rows], _TN, preferred_element_type=_F32)
                        for rows in chunks])
    before = []
    for h, lanes in enumerate(heads):
        ec = e_cum[:, lanes]
        state_t = st[h]
        seen = []
        for rows, update in zip(chunks, updates[h]):
            seen.append(state_t.astype(_BF16))
            state_t = state_t * ec[rows.stop - 1:rows.stop, :] + update
        st[h] = state_t
        before.append(seen)
    for h, lanes in enumerate(heads):
        qi = q_inter[:, lanes].astype(_BF16)
        o_inter = [lax.dot_general(qi[rows], s_t, _NT, preferred_element_type=_F32)
                   for rows, s_t in zip(chunks, before[h])]
        o_ref[:, lanes] = _hgrn_finish(intra[h] + jnp.concatenate(o_inter, axis=0), g_ref[:, lanes], w)

    @pl.when(n == pl.num_programs(1) - 1)
    def _():
        for h in range(B_HEADS):
            s_ref[0, h] = st[h].T


def _hgrn_prompt(bm, loglb, log1m, w, layer):
    nb = SEQ // BLK
    gate = lambda j: pl.BlockSpec((BLK, B_W), lambda b, n: (b * nb + n, j))
    return pl.pallas_call(
        _hgrn_prompt_kernel,
        grid=(BATCH, nb),
        in_specs=[gate(0), gate(1), gate(2), gate(3), _layer_vec(B_W, layer), _layer_vec(B_W, layer),
                  pl.BlockSpec((3 * BLK, BLK), lambda b, n: (0, 0)),
                  _layer_vec(B_VAL_DIM, layer)],
        out_specs=[pl.BlockSpec((BLK, B_W), lambda b, n: (b * nb + n, 0)),
                   pl.BlockSpec((1, B_HEADS, B_KEY_DIM, B_VAL_DIM), lambda b, n: (b, 0, 0, 0))],
        out_shape=[jax.ShapeDtypeStruct((M_ALL, B_W), _BF16),
                   jax.ShapeDtypeStruct((BATCH, B_HEADS, B_KEY_DIM, B_VAL_DIM), _F32)],
        scratch_shapes=[pltpu.VMEM((B_HEADS, B_VAL_DIM, B_KEY_DIM), _F32)],
        compiler_params=_params(2),
        name="hgrn_prompt",
    )(bm, bm, bm, bm, loglb, log1m, _chunk_matrices(BLK, B_CHUNK), w)


def _hgrn_sample_kernel(q_ref, f_ref, i_ref, g_ref, loglb_ref, log1m_ref, lt_ref, w_ref, s0_ref,
                        o_full_ref, o_ref, s_ref):
    del o_full_ref
    rows = HGRN_SAMPLE_GROUP * DEC_SEQ

    def padded(ref):
        return jnp.concatenate([ref[...], jnp.zeros((BLK - rows, B_W), _F32)], axis=0)

    v_all = padded(i_ref)
    g_all = padded(g_ref)
    q_intra, k_intra, q_inter, k_state, e_cum = _hgrn_gates(
        padded(q_ref), padded(f_ref), loglb_ref[...], log1m_ref[...], lt_ref[...], BLK)
    row = lax.broadcasted_iota(jnp.int32, (BLK, BLK), 0)
    col = lax.broadcasted_iota(jnp.int32, (BLK, BLK), 1)
    causal = (row // DEC_SEQ == col // DEC_SEQ) & (col <= row)
    w = w_ref[...]
    for h in range(B_HEADS):
        lanes = slice(h * B_KEY_DIM, (h + 1) * B_KEY_DIM)
        v = v_all[:, lanes].astype(_BF16)
        o = _hgrn_intra(q_intra[:, lanes], k_intra[:, lanes], v, causal)
        qi = q_inter[:, lanes].astype(_BF16)
        ec_t = e_cum[:, lanes].T
        ks_t = k_state[:, lanes].T
        for s in range(HGRN_SAMPLE_GROUP):
            state = s0_ref[s, h]
            o_s = jnp.dot(qi, state.astype(_BF16), preferred_element_type=_F32)
            o = o + jnp.where(row // DEC_SEQ == s, o_s, 0.0)
            ks_seq = jnp.where(col // DEC_SEQ == s, ks_t, 0.0).astype(_BF16)
            update = jnp.dot(ks_seq, v, preferred_element_type=_F32)
            decay = ec_t[:, (s + 1) * DEC_SEQ - 1:(s + 1) * DEC_SEQ]
            s_ref[s, h] = state * decay + update
        o_ref[:, lanes] = _hgrn_finish(o, g_all[:, lanes], w)[0:rows]


def _hgrn_sample(bm, o_full, state, loglb, log1m, w, layer):
    rows = HGRN_SAMPLE_GROUP * DEC_SEQ
    first = M_PROMPT // rows
    gate = lambda j: pl.BlockSpec((rows, B_W), lambda s: (first + s, j))
    state_shape = (HGRN_SAMPLE_GROUP, B_HEADS, B_KEY_DIM, B_VAL_DIM)
    return pl.pallas_call(
        _hgrn_sample_kernel,
        grid=(DEC_BATCH // HGRN_SAMPLE_GROUP,),
        in_specs=[gate(0), gate(1), gate(2), gate(3), _layer_vec(B_W, layer), _layer_vec(B_W, layer),
                  pl.BlockSpec((3 * BLK, BLK), lambda s: (0, 0)),
                  _layer_vec(B_VAL_DIM, layer),
                  pl.BlockSpec((None,) + state_shape, lambda s: (layer, s, 0, 0, 0)),
                  pl.BlockSpec(memory_space=pl.ANY)],
        out_specs=[pl.BlockSpec((rows, B_W), lambda s: (first + s, 0)),
                   pl.BlockSpec(state_shape, lambda s: (s, 0, 0, 0))],
        out_shape=[jax.ShapeDtypeStruct((M_ALL, B_W), _BF16),
                   jax.ShapeDtypeStruct(state.shape[1:], _F32)],
        input_output_aliases={9: 0},
        compiler_params=_params(1),
        name="hgrn_sample",
    )(bm, bm, bm, bm, loglb, log1m, _chunk_matrices(BLK, DEC_SEQ), w, state, o_full)


def _sgu_kernel(u_ref, v_ref, w_ref, b_ref, g_ref, o_ref, vn_ref):
    is_sample = pl.program_id(0) == pl.num_programs(0) - 1
    row = lax.broadcasted_iota(jnp.int32, (BLK, BLK), 0)
    col = lax.broadcasted_iota(jnp.int32, (BLK, BLK), 1)
    same_seq = jnp.logical_or(jnp.logical_not(is_sample), row // DEC_SEQ == col // DEC_SEQ)
    causal = (col <= row) & same_seq
    gain = g_ref[...]
    bias = b_ref[...]
    for g in range(C_GROUPS):
        lanes = slice(g * C_GROUP_DIM, (g + 1) * C_GROUP_DIM)
        v = v_ref[:, lanes]
        vn = v * lax.rsqrt(jnp.mean(v * v, axis=-1, keepdims=True) + EPS) * gain
        w = jnp.where(causal, w_ref[g], 0.0).astype(_BF16)
        z = jnp.dot(w, vn.astype(_BF16), preferred_element_type=_F32) + bias[:, g:g + 1]
        o_ref[:, lanes] = (u_ref[:, lanes] * z).astype(_BF16)
        vn_ref[:, lanes] = vn


def _sgu(cm, w2, b2, gain, layer):
    nblk = M_ALL // BLK
    which = lambda i: i // (nblk - 1)
    return pl.pallas_call(
        _sgu_kernel,
        grid=(nblk,),
        in_specs=[pl.BlockSpec((BLK, C_W), lambda i: (i, 0)),
                  pl.BlockSpec((BLK, C_W), lambda i: (i, 1)),
                  pl.BlockSpec((None, None, C_GROUPS, C_CHUNK, C_CHUNK), lambda i: (layer, which(i), 0, 0, 0)),
                  pl.BlockSpec((None, None, C_CHUNK, C_GROUPS), lambda i: (layer, which(i), 0, 0)),
                  _layer_vec(C_GROUP_DIM, layer)],
        out_specs=[pl.BlockSpec((BLK, C_W), lambda i: (i, 0)),
                   pl.BlockSpec((BLK, C_W), lambda i: (0, 0))],
        out_shape=[jax.ShapeDtypeStruct((M_ALL, C_W), _BF16),
                   jax.ShapeDtypeStruct((M_SAMPLE, C_W), _F32)],
        compiler_params=_params(1),
        name="sgu",
    )(cm, cm, w2, b2, gain)


def _sgu_params(w_spatial, b_spatial):
    reps = BLK // DEC_SEQ
    w_sample = jnp.tile(w_spatial[:, :, :DEC_SEQ, :DEC_SEQ], (1, 1, reps, reps))
    b_sample = jnp.tile(b_spatial[:, :, :DEC_SEQ], (1, 1, reps))
    w2 = jnp.stack([w_spatial, w_sample], axis=1)
    b2 = jnp.stack([jnp.swapaxes(b_spatial, 1, 2), jnp.swapaxes(b_sample, 1, 2)], axis=1)
    return w2, b2


def kernel(x_prompt, x_sample, cache_k, cache_v, state_hgrn, norm_mix, w_in, q_norm, k_norm, sinks,
           lb_logits, hgrn_out_norm, sgu_v_norm, w_spatial, b_spatial, w_branch_a, w_branch_b,
           w_branch_c, w_out, norm_ffn, w_ffn_up, w_ffn_down):
    x = jnp.concatenate([x_prompt.reshape(M_PROMPT, D_MODEL), x_sample.reshape(M_SAMPLE, D_MODEL)], axis=0)
    loglb, log1m = _lower_bounds(lb_logits)
    loglb = loglb.reshape(DEPTH, 1, B_W)
    log1m = log1m.reshape(DEPTH, 1, B_W)
    tables_prompt = _rope_tables(0, SEQ)
    tables_sample = tuple(jnp.tile(t, (ATTN_SAMPLE_GROUP, 1)) for t in _rope_tables(PAST_LEN, DEC_SEQ))
    gq = jnp.tile(q_norm, (1, A_HEADS)).reshape(DEPTH, 1, A_Q_W)
    gk = jnp.tile(k_norm, (1, A_KV_HEADS)).reshape(DEPTH, 1, A_KV_W)
    w_hg = hgrn_out_norm.reshape(DEPTH, 1, B_VAL_DIM)
    w_sg = sgu_v_norm.reshape(DEPTH, 1, C_GROUP_DIM)
    g_mix = norm_mix.reshape(DEPTH, 1, D_MODEL)
    g_ffn = norm_ffn.reshape(DEPTH, 1, D_MODEL)
    sinks = sinks.reshape(DEPTH, 1, A_HEADS)
    w2, b2 = _sgu_params(w_spatial, b_spatial)
    ck_all = cache_k.reshape(DEPTH, DEC_BATCH, WINDOW, A_KV_W)
    cv_all = cache_v.reshape(DEPTH, DEC_BATCH, WINDOW, A_KV_W)

    kp_l, vp_l, sp_l, ks_l, vs_l, ss_l, cs_l = [], [], [], [], [], [], []
    for l in range(DEPTH):
        h = _rmsnorm(x, g_mix, l)
        am = _matmul_cols(h, w_in, l, OFF_A, W_A)
        bm = _matmul_cols(h, w_in, l, OFF_B, W_B)
        cm = _matmul_cols(h, w_in, l, OFF_C, W_C)
        gates = _matmul_cols(h, w_in, l, OFF_G, W_G)

        oa, kc_p, vc_p = _attn_prompt(am, sinks, gq, gk, tables_prompt, l)
        oa, kc_s, vc_s = _attn_sample(am, oa, ck_all, cv_all, sinks, gq, gk, tables_sample, l)
        ob, st_p = _hgrn_prompt(bm, loglb, log1m, w_hg, l)
        ob, st_s = _hgrn_sample(bm, ob, state_hgrn, loglb, log1m, w_hg, l)
        oc, vn_s = _sgu(cm, w2, b2, w_sg, l)
        merged = _merge(oa, ob, oc, gates, w_branch_a, w_branch_b, w_branch_c, l)
        x = _matmul_residual(merged, w_out, l, x, TM, TN)
        h2 = _rmsnorm(x, g_ffn, l)
        act = _ffn_up(h2, w_ffn_up, l)
        x = _matmul_residual(act, w_ffn_down, l, x, TM_DOWN, TN_NARROW)

        kp_l.append(kc_p.reshape(BATCH, WINDOW, A_KV_HEADS, A_HEAD_DIM))
        vp_l.append(vc_p.reshape(BATCH, WINDOW, A_KV_HEADS, A_HEAD_DIM))
        sp_l.append(st_p)
        ks_l.append(kc_s.reshape(DEC_BATCH, WINDOW, A_KV_HEADS, A_HEAD_DIM))
        vs_l.append(vc_s.reshape(DEC_BATCH, WINDOW, A_KV_HEADS, A_HEAD_DIM))
        ss_l.append(st_s)
        cs_l.append(vn_s.reshape(DEC_BATCH, DEC_SEQ, C_GROUPS, C_GROUP_DIM))

    y_prompt = x[:M_PROMPT].reshape(BATCH, SEQ, D_MODEL)
    y_sample = x[M_PROMPT:].reshape(DEC_BATCH, DEC_SEQ, D_MODEL)
    return (y_prompt, y_sample, jnp.stack(kp_l), jnp.stack(vp_l), jnp.stack(sp_l),
            jnp.stack(ks_l), jnp.stack(vs_l), jnp.stack(ss_l), jnp.stack(cs_l))
```

```python
import numpy as np
import jax
import jax.numpy as jnp
from jax import lax
from jax.experimental import pallas as pl
from jax.experimental.pallas import tpu as pltpu

D_MODEL = 2048
BATCH = 4
SEQ = 2048
DEPTH = 4
DEC_BATCH = 32
DEC_SEQ = 4
PAST_LEN = 16384

A_HEADS = 16
A_KV_HEADS = 4
A_HEAD_DIM = 64
A_GROUP = A_HEADS // A_KV_HEADS
WINDOW = 128
ROT_DIM = A_HEAD_DIM // 4
ROT_HALF = ROT_DIM // 2
ROPE_THETA = 500000.0
B_HEADS = 8
B_KEY_DIM = 128
B_VAL_DIM = 128
B_CHUNK = 16
C_GROUPS = 8
C_GROUP_DIM = 128
C_CHUNK = 128
A_Q_W = A_HEADS * A_HEAD_DIM
A_KV_W = A_KV_HEADS * A_HEAD_DIM
B_W = B_HEADS * B_KEY_DIM
C_W = C_GROUPS * C_GROUP_DIM
FFN_DIM = ((8 * D_MODEL + 3 * 256 - 1) // (3 * 256)) * 256
EPS = 1e-6

M_PROMPT = BATCH * SEQ
M_SAMPLE = DEC_BATCH * DEC_SEQ
M_ALL = M_PROMPT + M_SAMPLE

OFF_A = 0
W_A = A_Q_W + 2 * A_KV_W
OFF_B = OFF_A + W_A
W_B = 4 * B_W
OFF_C = OFF_B + W_B
W_C = 2 * C_W
OFF_G = OFF_C + W_C
W_G = 3 * D_MODEL

LANES = 128
BLK = 128
TM = M_ALL // 4
TM_DOWN = M_ALL // 8
TM_NORM = M_ALL // 16
TN = 512
TN_NARROW = 256
SAMPLE_PAD = 16
ATTN_SAMPLE_GROUP = 4
HGRN_SAMPLE_GROUP = 8
VMEM_LIMIT = 56 * 1024 * 1024

_BF16 = jnp.bfloat16
_F32 = jnp.float32
_NT = (((1,), (1,)), ((), ()))
_TN = (((0,), (0,)), ((), ()))


def _params(n_grid):
    return pltpu.CompilerParams(dimension_semantics=("arbitrary",) * n_grid,
                                vmem_limit_bytes=VMEM_LIMIT)


def _sigmoid(x):
    return 0.5 * jnp.tanh(0.5 * x) + 0.5


def _layer_vec(width, layer):
    return pl.BlockSpec((None, 1, width), lambda *_: (layer, 0, 0))


def _rmsnorm_kernel(x_ref, g_ref, o_ref):
    x = x_ref[...]
    y = x * lax.rsqrt(jnp.mean(x * x, axis=-1, keepdims=True) + EPS)
    o_ref[...] = (y * g_ref[...]).astype(_BF16)


def _rmsnorm(x, g, layer):
    m, d = x.shape
    return pl.pallas_call(
        _rmsnorm_kernel,
        grid=(m // TM_NORM,),
        in_specs=[pl.BlockSpec((TM_NORM, d), lambda i: (i, 0)), _layer_vec(d, layer)],
        out_specs=pl.BlockSpec((TM_NORM, d), lambda i: (i, 0)),
        out_shape=jax.ShapeDtypeStruct((m, d), _BF16),
        compiler_params=_params(1),
        name="rmsnorm",
    )(x, g)


def _mm_kernel(a_ref, w_ref, o_ref):
    o_ref[...] = jnp.dot(a_ref[...], w_ref[...].astype(_BF16), preferred_element_type=_F32)


def _matmul_cols(a, w, layer, col_off, n_cols):
    m, k = a.shape
    off = col_off // TN
    return pl.pallas_call(
        _mm_kernel,
        grid=(m // TM, n_cols // TN),
        in_specs=[pl.BlockSpec((TM, k), lambda i, j: (i, 0)),
                  pl.BlockSpec((None, k, TN), lambda i, j: (layer, 0, j + off))],
        out_specs=pl.BlockSpec((TM, TN), lambda i, j: (i, j)),
        out_shape=jax.ShapeDtypeStruct((m, n_cols), _F32),
        compiler_params=_params(2),
        name="proj_in",
    )(a, w)


def _mm_res_kernel(a_ref, w_ref, r_ref, o_ref):
    o_ref[...] = r_ref[...] + jnp.dot(a_ref[...], w_ref[...].astype(_BF16),
                                      preferred_element_type=_F32)


def _matmul_residual(a, w, layer, r, tm, tn):
    m, k = a.shape
    n = w.shape[2]
    return pl.pallas_call(
        _mm_res_kernel,
        grid=(m // tm, n // tn),
        in_specs=[pl.BlockSpec((tm, k), lambda i, j: (i, 0)),
                  pl.BlockSpec((None, k, tn), lambda i, j: (layer, 0, j)),
                  pl.BlockSpec((tm, tn), lambda i, j: (i, j))],
        out_specs=pl.BlockSpec((tm, tn), lambda i, j: (i, j)),
        out_shape=jax.ShapeDtypeStruct((m, n), _F32),
        compiler_params=_params(2),
        name="proj_residual",
    )(a, w, r)


def _ffn_up_kernel(a_ref, wg_ref, wu_ref, o_ref):
    a = a_ref[...]
    g = jnp.dot(a, wg_ref[...].astype(_BF16), preferred_element_type=_F32)
    u = jnp.dot(a, wu_ref[...].astype(_BF16), preferred_element_type=_F32)
    o_ref[...] = (g * _sigmoid(g) * u).astype(_BF16)


def _ffn_up(a, w_up, layer):
    m, k = a.shape
    tn = TN_NARROW
    nj = FFN_DIM // tn
    return pl.pallas_call(
        _ffn_up_kernel,
        grid=(m // TM, nj),
        in_specs=[pl.BlockSpec((TM, k), lambda i, j: (i, 0)),
                  pl.BlockSpec((None, k, tn), lambda i, j: (layer, 0, j)),
                  pl.BlockSpec((None, k, tn), lambda i, j: (layer, 0, j + nj))],
        out_specs=pl.BlockSpec((TM, tn), lambda i, j: (i, j)),
        out_shape=jax.ShapeDtypeStruct((m, FFN_DIM), _BF16),
        compiler_params=_params(2),
        name="ffn_up",
    )(a, w_up, w_up)


def _merge_kernel(oa_ref, ob_ref, oc_ref, wa_ref, wb_ref, wc_ref, ga_ref, gb_ref, gc_ref, o_ref):
    ya = jnp.dot(oa_ref[...], wa_ref[...].astype(_BF16), preferred_element_type=_F32)
    yb = jnp.dot(ob_ref[...], wb_ref[...].astype(_BF16), preferred_element_type=_F32)
    yc = jnp.dot(oc_ref[...], wc_ref[...].astype(_BF16), preferred_element_type=_F32)
    merged = _sigmoid(ga_ref[...]) * ya + _sigmoid(gb_ref[...]) * yb + _sigmoid(gc_ref[...]) * yc
    o_ref[...] = merged.astype(_BF16)


def _merge(oa, ob, oc, gates, wa, wb, wc, layer):
    m, k = oa.shape
    tn = TN_NARROW
    nj = D_MODEL // tn
    branch = pl.BlockSpec((TM, k), lambda i, j: (i, 0))
    weight = pl.BlockSpec((None, k, tn), lambda i, j: (layer, 0, j))
    return pl.pallas_call(
        _merge_kernel,
        grid=(m // TM, nj),
        in_specs=[branch, branch, branch, weight, weight, weight,
                  pl.BlockSpec((TM, tn), lambda i, j: (i, j)),
                  pl.BlockSpec((TM, tn), lambda i, j: (i, j + nj)),
                  pl.BlockSpec((TM, tn), lambda i, j: (i, j + 2 * nj))],
        out_specs=pl.BlockSpec((TM, tn), lambda i, j: (i, j)),
        out_shape=jax.ShapeDtypeStruct((m, D_MODEL), _BF16),
        compiler_params=_params(2),
        name="merge",
    )(oa, ob, oc, wa, wb, wc, gates, gates, gates)


def _rope_tables(p0, rows):
    pos = (p0 + jnp.arange(rows, dtype=jnp.int32)).astype(_F32)
    inv_freq = jnp.power(jnp.float32(ROPE_THETA), -jnp.arange(ROT_HALF, dtype=_F32) / ROT_HALF)
    ang = pos[:, None] * inv_freq[None, :]
    cos, sin = jnp.cos(ang), jnp.sin(ang)
    rest = A_HEAD_DIM - ROT_DIM
    zeros = jnp.zeros((rows, ROT_HALF), _F32)
    pad = jnp.zeros((rows, rest), _F32)
    c = jnp.concatenate([cos, cos, jnp.ones((rows, rest), _F32)], axis=1)
    s1 = jnp.concatenate([-sin, zeros, pad], axis=1)
    s2 = jnp.concatenate([zeros, sin, pad], axis=1)
    return tuple(jnp.tile(t, (1, LANES // A_HEAD_DIM)) for t in (c, s1, s2))


def _rope(x, c, s1, s2):
    width = x.shape[1]
    reps = width // c.shape[1]
    c, s1, s2 = (jnp.concatenate([t] * reps, axis=1) for t in (c, s1, s2))
    ahead = pltpu.roll(x, width - ROT_HALF, axis=1)
    behind = pltpu.roll(x, ROT_HALF, axis=1)
    return x * c + ahead * s1 + behind * s2


def _head_mean_matrix():
    i = np.arange(LANES)
    same = (i[:, None] // A_HEAD_DIM) == (i[None, :] // A_HEAD_DIM)
    return jnp.asarray(same.astype(np.float32) / A_HEAD_DIM, dtype=_BF16)


def _qk_prep(x, gain, head_mean, c, s1, s2, scale):
    sq = x * x
    hi = sq.astype(_BF16)
    lo = (sq - hi.astype(_F32)).astype(_BF16)
    ms = jnp.concatenate(
        [jnp.dot(hi[:, l:l + LANES], head_mean, preferred_element_type=_F32)
         + jnp.dot(lo[:, l:l + LANES], head_mean, preferred_element_type=_F32)
         for l in range(0, x.shape[1], LANES)], axis=1)
    return _rope(x * gain, c, s1, s2) * (lax.rsqrt(ms + EPS) * scale)


def _low_half(shape):
    return lax.broadcasted_iota(jnp.int32, shape, 1) < A_HEAD_DIM


def _both_halves(col, half):
    low = _low_half(col.shape)
    sel = jnp.where(low if half == 0 else jnp.logical_not(low), col, 0.0)
    return sel + pltpu.roll(sel, A_HEAD_DIM, axis=1)


def _kv_pairs(k, v, h):
    pair, half = divmod(h, 2)
    lanes = slice(pair * LANES, (pair + 1) * LANES)
    return _both_halves(k[:, lanes], half), _both_halves(v[:, lanes], half)


def _mxu_tiles(k_pair, v_pair):
    v_pair = v_pair.astype(_BF16)
    return k_pair.astype(_BF16), jnp.concatenate([v_pair, jnp.ones(v_pair.shape, _BF16)], axis=1)


def _kv_tiles(k, v, h):
    return _mxu_tiles(*_kv_pairs(k, v, h))


def _query_rows(qn, h):
    low = _low_half((qn.shape[0], LANES))
    parts = []
    for g in range(A_GROUP):
        pair, half = divmod(h * A_GROUP + g, 2)
        keep = low if half == 0 else jnp.logical_not(low)
        parts.append(jnp.where(keep, qn[:, pair * LANES:(pair + 1) * LANES], 0.0))
    return jnp.concatenate(parts, axis=0).astype(_BF16)


def _sink_rows(sinks_ref, h, t):
    return jnp.concatenate([jnp.full((t, LANES), sinks_ref[0, h * A_GROUP + g], _F32)
                            for g in range(A_GROUP)], axis=0)


def _store_heads(o_ref, o, h, t):
    low = _low_half((t, LANES))
    for j in range(A_GROUP // 2):
        even = o[(2 * j) * t:(2 * j + 1) * t]
        odd = o[(2 * j + 1) * t:(2 * j + 2) * t]
        pair = (h * A_GROUP) // 2 + j
        o_ref[:, pair * LANES:(pair + 1) * LANES] = jnp.where(low, even, odd).astype(o_ref.dtype)


def _attn_prompt_kernel(sinks_ref, q_ref, kv_ref, c_ref, s1_ref, s2_ref, gq_ref, gk_ref, hm_ref,
                        o_ref, kc_ref, vc_ref, q_hand, k_hand, v_hand):
    b = pl.program_id(0)
    n = pl.program_id(1)
    nb = pl.num_programs(1) - 2

    @pl.when(n == 0)
    def _():
        q_hand[...] = jnp.zeros(q_hand.shape, _F32)
        k_hand[...] = jnp.zeros(k_hand.shape, _F32)
        v_hand[...] = jnp.zeros(v_hand.shape, _F32)

    def step(new, cur, old):
        qn = q_hand[cur]
        kv = kv_ref[...]
        v_new = kv[:, A_KV_W:]
        c, s1, s2 = c_ref[...], s1_ref[...], s2_ref[...]
        q_hand[new] = _qk_prep(q_ref[...], gq_ref[...], hm_ref[...], c, s1, s2, A_HEAD_DIM ** -0.5)
        k_new = _qk_prep(kv[:, :A_KV_W], gk_ref[...], hm_ref[...], c, s1, s2, 1.0)
        kc_ref[0] = k_new
        vc_ref[0] = v_new
        for h in range(A_KV_HEADS):
            k_hand[new, h], v_hand[new, h] = _kv_pairs(k_new, v_new, h)

        rows = A_GROUP * BLK
        row = lax.broadcasted_iota(jnp.int32, (rows, WINDOW), 0) % BLK
        col = lax.broadcasted_iota(jnp.int32, (rows, WINDOW), 1)
        before = col > row
        has_prev = n > 1
        for h in range(A_KV_HEADS):
            k_tile, v_tile = _mxu_tiles(k_hand[cur, h], v_hand[cur, h])
            k_before, v_before = _mxu_tiles(k_hand[old, h], v_hand[old, h])
            q4 = _query_rows(qn, h)
            s_prev = lax.dot_general(q4, k_before, _NT, preferred_element_type=_F32)
            s_cur = lax.dot_general(q4, k_tile, _NT, preferred_element_type=_F32)
            s = jnp.where(before, jnp.where(has_prev, s_prev, -jnp.inf), s_cur)
            sink = _sink_rows(sinks_ref, h, BLK)
            m = jnp.maximum(jnp.broadcast_to(jnp.max(s, axis=-1, keepdims=True), s.shape), sink)
            p = jnp.exp(s - m)
            acc = (jnp.dot(jnp.where(before, p, 0.0).astype(_BF16), v_before, preferred_element_type=_F32)
                   + jnp.dot(jnp.where(before, 0.0, p).astype(_BF16), v_tile, preferred_element_type=_F32))
            o = acc[:, :LANES] / (acc[:, LANES:] + jnp.exp(sink - m))
            _store_heads(o_ref, o, h, BLK)

    ring = k_hand.shape[0]
    for new in range(ring):
        pl.when((n <= nb) & (n % ring == new))(
            lambda new=new: step(new, (new + ring - 1) % ring, (new + ring - 2) % ring))

    @pl.when((n == nb + 1) & (b == pl.num_programs(0) - 1))
    def _():
        o_ref[...] = jnp.zeros(o_ref.shape, o_ref.dtype)


def _attn_prompt(a, sinks, gq, gk, tables, layer):
    nb = SEQ // BLK
    c, s1, s2 = tables
    ahead = lambda b, n: b * nb + jnp.minimum(n, nb - 1)
    table = pl.BlockSpec((BLK, LANES), lambda b, n: (jnp.minimum(n, nb - 1), 0))
    cache = pl.BlockSpec((1, BLK, A_KV_W), lambda b, n: (b, 0, 0))
    const = lambda w: pl.BlockSpec((w, w), lambda b, n: (0, 0))

    def out_block(b, n):
        tail = (n == nb + 1) & (b == BATCH - 1)
        return (jnp.where(tail, BATCH * nb, b * nb + jnp.clip(n - 1, 0, nb - 1)), 0)

    return pl.pallas_call(
        _attn_prompt_kernel,
        grid=(BATCH, nb + 2),
        in_specs=[pl.BlockSpec((None, 1, A_HEADS), lambda b, n: (layer, 0, 0), memory_space=pltpu.SMEM),
                  pl.BlockSpec((BLK, A_Q_W), lambda b, n: (ahead(b, n), 0)),
                  pl.BlockSpec((BLK, 2 * A_KV_W), lambda b, n: (ahead(b, n), A_Q_W // (2 * A_KV_W))),
                  table, table, table,
                  _layer_vec(A_Q_W, layer), _layer_vec(A_KV_W, layer), const(LANES)],
        out_specs=[pl.BlockSpec((BLK, A_Q_W), out_block), cache, cache],
        out_shape=[jax.ShapeDtypeStruct((M_ALL, A_Q_W), _BF16),
                   jax.ShapeDtypeStruct((BATCH, WINDOW, A_KV_W), _F32),
                   jax.ShapeDtypeStruct((BATCH, WINDOW, A_KV_W), _F32)],
        scratch_shapes=[pltpu.VMEM((3, BLK, A_Q_W), _F32),
                        pltpu.VMEM((3, A_KV_HEADS, BLK, LANES), _F32),
                        pltpu.VMEM((3, A_KV_HEADS, BLK, LANES), _F32)],
        compiler_params=_params(2),
        name="attn_prompt",
    )(sinks, a, a, c, s1, s2, gq, gk, _head_mean_matrix())


def _attn_sample_kernel(sinks_ref, qkv_ref, ck_ref, cv_ref, c_ref, s1_ref, s2_ref, gq_ref, gk_ref,
                        hm_ref, o_full_ref, o_ref, kc_ref, vc_ref, kbuf, vbuf, knew, vnew):
    del o_full_ref
    grp = ATTN_SAMPLE_GROUP
    t = grp * DEC_SEQ
    qkv = qkv_ref[...]
    v = qkv[:, A_Q_W + A_KV_W:]
    c, s1, s2 = c_ref[...], s1_ref[...], s2_ref[...]
    qn = _qk_prep(qkv[:, :A_Q_W], gq_ref[...], hm_ref[...], c, s1, s2, A_HEAD_DIM ** -0.5)
    kn = _qk_prep(qkv[:, A_Q_W:A_Q_W + A_KV_W], gk_ref[...], hm_ref[...], c, s1, s2, 1.0)

    knew[...] = jnp.zeros(knew.shape, _F32)
    vnew[...] = jnp.zeros(vnew.shape, _F32)
    knew[0:t, :] = kn
    vnew[0:t, :] = v
    for s in range(grp):
        kbuf[s, 0:WINDOW, :] = ck_ref[s]
        vbuf[s, 0:WINDOW, :] = cv_ref[s]
        kbuf[s, WINDOW:, :] = knew[s * DEC_SEQ:s * DEC_SEQ + 8, :]
        vbuf[s, WINDOW:, :] = vnew[s * DEC_SEQ:s * DEC_SEQ + 8, :]
        kc_ref[s] = kbuf[s, DEC_SEQ:DEC_SEQ + WINDOW, :]
        vc_ref[s] = vbuf[s, DEC_SEQ:DEC_SEQ + WINDOW, :]

    rows = A_GROUP * t
    r_old = lax.broadcasted_iota(jnp.int32, (rows, grp * WINDOW), 0) % t
    c_old = lax.broadcasted_iota(jnp.int32, (rows, grp * WINDOW), 1)
    see_old = (c_old // WINDOW == r_old // DEC_SEQ) & (c_old % WINDOW > r_old % DEC_SEQ)
    r_new = lax.broadcasted_iota(jnp.int32, (rows, WINDOW), 0) % t
    c_new = lax.broadcasted_iota(jnp.int32, (rows, WINDOW), 1)
    see_new = (c_new < t) & (c_new // DEC_SEQ == r_new // DEC_SEQ) & (c_new % DEC_SEQ <= r_new % DEC_SEQ)
    pad_k = jnp.zeros((WINDOW - t, LANES), _BF16)
    pad_v = jnp.zeros((WINDOW - t, 2 * LANES), _BF16)
    for h in range(A_KV_HEADS):
        old = [_kv_tiles(ck_ref[s], cv_ref[s], h) for s in range(grp)]
        k_old = jnp.concatenate([kt for kt, _ in old], axis=0)
        v_old = jnp.concatenate([vt for _, vt in old], axis=0)
        k_new, v_new = _kv_tiles(kn, v, h)
        k_new = jnp.concatenate([k_new, pad_k], axis=0)
        v_new = jnp.concatenate([v_new, pad_v], axis=0)
        q4 = _query_rows(qn, h)
        s_old = jnp.where(see_old, lax.dot_general(q4, k_old, _NT, preferred_element_type=_F32), -jnp.inf)
        s_new = jnp.where(see_new, lax.dot_general(q4, k_new, _NT, preferred_element_type=_F32), -jnp.inf)
        sink = _sink_rows(sinks_ref, h, t)
        top = jnp.maximum(jnp.max(s_old, axis=-1, keepdims=True), jnp.max(s_new, axis=-1, keepdims=True))
        m = jnp.maximum(jnp.broadcast_to(top, sink.shape), sink)
        p_old = jnp.exp(s_old - jnp.concatenate([m] * grp, axis=1))
        p_new = jnp.exp(s_new - m)
        acc = (jnp.dot(p_old.astype(_BF16), v_old, preferred_element_type=_F32)
               + jnp.dot(p_new.astype(_BF16), v_new, preferred_element_type=_F32))
        o = acc[:, :LANES] / (acc[:, LANES:] + jnp.exp(sink - m))
        _store_heads(o_ref, o, h, t)


def _attn_sample(a, o_full, cache_k, cache_v, sinks, gq, gk, tables, layer):
    c, s1, s2 = tables
    grp = ATTN_SAMPLE_GROUP
    t = grp * DEC_SEQ
    first = M_PROMPT // t
    table = pl.BlockSpec((t, LANES), lambda b: (0, 0))
    cache_in = pl.BlockSpec((None, grp, WINDOW, A_KV_W), lambda b: (layer, b, 0, 0))
    cache_out = pl.BlockSpec((grp, WINDOW, A_KV_W), lambda b: (b, 0, 0))
    const = lambda w: pl.BlockSpec((w, w), lambda b: (0, 0))
    return pl.pallas_call(
        _attn_sample_kernel,
        grid=(DEC_BATCH // grp,),
        in_specs=[pl.BlockSpec((None, 1, A_HEADS), lambda b: (layer, 0, 0), memory_space=pltpu.SMEM),
                  pl.BlockSpec((t, W_A), lambda b: (first + b, 0)),
                  cache_in, cache_in, table, table, table,
                  _layer_vec(A_Q_W, layer), _layer_vec(A_KV_W, layer), const(LANES),
                  pl.BlockSpec(memory_space=pl.ANY)],
        out_specs=[pl.BlockSpec((t, A_Q_W), lambda b: (first + b, 0)), cache_out, cache_out],
        out_shape=[jax.ShapeDtypeStruct((M_ALL, A_Q_W), _BF16),
                   jax.ShapeDtypeStruct((DEC_BATCH, WINDOW, A_KV_W), _F32),
                   jax.ShapeDtypeStruct((DEC_BATCH, WINDOW, A_KV_W), _F32)],
        scratch_shapes=[pltpu.VMEM((grp, WINDOW + 8, A_KV_W), _F32),
                        pltpu.VMEM((grp, WINDOW + 8, A_KV_W), _F32),
                        pltpu.VMEM((t + 8, A_KV_W), _F32), pltpu.VMEM((t + 8, A_KV_W), _F32)],
        input_output_aliases={10: 0},
        compiler_params=_params(1),
        name="attn_sample",
    )(sinks, a, cache_k, cache_v, c, s1, s2, gq, gk, _head_mean_matrix(),
      o_full)


def _lower_bound_kernel(logits_ref, loglb_ref, log1m_ref):
    x = logits_ref[...]
    e = jnp.exp(x - jnp.max(x, axis=0, keepdims=True))
    sm = e / jnp.sum(e, axis=0, keepdims=True)
    acc = sm[0:1]
    rows = [acc]
    for l in range(1, DEPTH):
        acc = acc + sm[l:l + 1]
        rows.append(acc)
    for l in range(DEPTH):
        lb = rows[l] - rows[0]
        loglb_ref[l:l + 1, :] = jnp.log(lb)
        log1m_ref[l:l + 1, :] = jnp.log1p(-lb)


def _lower_bounds(lb_logits):
    shape = jax.ShapeDtypeStruct(lb_logits.shape, _F32)
    return pl.pallas_call(_lower_bound_kernel, out_shape=[shape, shape], name="hgrn_lower_bounds")(lb_logits)


def _chunk_matrices(rows, chunk):
    t = np.arange(rows)[:, None]
    s = np.arange(rows)[None, :]
    same = (t // chunk) == (s // chunk)
    tri = (same & (s <= t)).astype(np.float32)
    ref = (same & ((s % chunk) <= chunk // 2)).astype(np.float32)
    last = same.astype(np.float32)
    return jnp.asarray(np.concatenate([tri, tri - ref, last - tri], axis=0), dtype=_BF16)


def _split2(x):
    hi = x.astype(_BF16)
    return hi, (x - hi.astype(_F32)).astype(_BF16)


def _hgrn_gates(q, z, loglb, log1m, lt, rows):
    log_sig = jnp.minimum(z, 0.0) - jnp.log(1.0 + jnp.exp(-jnp.abs(z)))
    b = log1m + log_sig
    log_f = jnp.maximum(loglb, b) + jnp.log(1.0 + jnp.exp(-jnp.abs(loglb - b)))
    kk = -jnp.tanh(0.5 * log_f) * (jnp.exp(log_f) + 1.0)
    hi, lo = _split2(log_f)
    cums = jnp.dot(lt, hi, preferred_element_type=_F32) + jnp.dot(lt, lo, preferred_element_type=_F32)
    cum = cums[0:rows]
    cum_ref = cums[rows:2 * rows]
    cum_end = cums[2 * rows:3 * rows]
    e_cum = jnp.exp(cum)
    q_intra = q * jnp.exp(cum_ref)
    k_intra = kk * jnp.exp(-cum_ref)
    q_inter = q * e_cum
    k_state = kk * jnp.exp(cum_end)
    return q_intra, k_intra, q_inter, k_state, e_cum


def _hgrn_finish(o, g, w):
    y = o * lax.rsqrt(jnp.mean(o * o, axis=-1, keepdims=True) + EPS) * w
    return (y * (g * _sigmoid(g))).astype(_BF16)


def _hgrn_intra(q_intra, k_intra, v, causal):
    att = lax.dot_general(q_intra.astype(_BF16), k_intra.astype(_BF16), _NT, preferred_element_type=_F32)
    att = jnp.where(causal, att, 0.0)
    return jnp.dot(att.astype(_BF16), v, preferred_element_type=_F32)


def _hgrn_prompt_kernel(q_ref, f_ref, i_ref, g_ref, loglb_ref, log1m_ref, lt_ref, w_ref,
                        o_ref, s_ref, st, hand):
    b = pl.program_id(0)
    n = pl.program_id(1)
    nb = pl.num_programs(1) - 2

    @pl.when(n == 0)
    def _():
        st[...] = jnp.zeros(st.shape, _F32)
        hand[...] = jnp.zeros(hand.shape, _F32)

    def step(slot, done):
        row = lax.broadcasted_iota(jnp.int32, (BLK, BLK), 0)
        col = lax.broadcasted_iota(jnp.int32, (BLK, BLK), 1)
        causal = (row // B_CHUNK == col // B_CHUNK) & (col <= row)
        chunks = [slice(c * B_CHUNK, (c + 1) * B_CHUNK) for c in range(BLK // B_CHUNK)]
        w = w_ref[...]
        lt = lt_ref[...]
        heads = [slice(h * B_KEY_DIM, (h + 1) * B_KEY_DIM) for h in range(B_HEADS)]
        intra, updates = [], []
        for lanes in heads:
            for j, t in enumerate(_hgrn_gates(q_ref[:, lanes], f_ref[:, lanes], loglb_ref[:, lanes],
                                              log1m_ref[:, lanes], lt, BLK)):
                hand[slot, j, :, lanes] = t
            v = i_ref[:, lanes].astype(_BF16)
            ks = hand[done, 3, :, lanes].astype(_BF16)
            intra.append(_hgrn_intra(hand[done, 0, :, lanes], hand[done, 1, :, lanes], v, causal))
            updates.append([lax.dot_general(v[rows], ks[rows], _TN, preferred_element_type=_F32)
                            for rows in chunks])
        before = []
        for h, lanes in enumerate(heads):
            state_t = st[h]
            seen = []
            for rows, update in zip(chunks, updates[h]):
                seen.append(state_t.astype(_BF16))
                state_t = state_t * hand[done, 4, rows.stop - 1:rows.stop, lanes] + update
            st[h] = state_t
            before.append(seen)
        for h, lanes in enumerate(heads):
            qi = hand[done, 2, :, lanes].astype(_BF16)
            o_inter = [lax.dot_general(qi[rows], s_t, _NT, preferred_element_type=_F32)
                       for rows, s_t in zip(chunks, before[h])]
            o_ref[:, lanes] = _hgrn_finish(intra[h] + jnp.concatenate(o_inter, axis=0), g_ref[:, lanes], w)

    for parity in range(2):
        pl.when((n <= nb) & (n % 2 == parity))(lambda parity=parity: step(parity, 1 - parity))

    @pl.when(n == nb)
    def _():
        for h in range(B_HEADS):
            s_ref[0, h] = st[h].T

    @pl.when((n == nb + 1) & (b == pl.num_programs(0) - 1))
    def _():
        o_ref[...] = jnp.zeros(o_ref.shape, o_ref.dtype)


def _hgrn_prompt(bm, loglb, log1m, w, layer):
    nb = SEQ // BLK

    def ahead(j):
        return pl.BlockSpec((BLK, B_W), lambda b, n: (b * nb + jnp.minimum(n, nb - 1), j))

    def behind(j):
        return pl.BlockSpec((BLK, B_W), lambda b, n: (b * nb + jnp.clip(n - 1, 0, nb - 1), j))

    def out_block(b, n):
        tail = (n == nb + 1) & (b == BATCH - 1)
        return (jnp.where(tail, BATCH * nb, b * nb + jnp.clip(n - 1, 0, nb - 1)), 0)

    return pl.pallas_call(
        _hgrn_prompt_kernel,
        grid=(BATCH, nb + 2),
        in_specs=[ahead(0), ahead(1), behind(2), behind(3), _layer_vec(B_W, layer), _layer_vec(B_W, layer),
                  pl.BlockSpec((3 * BLK, BLK), lambda b, n: (0, 0)),
                  _layer_vec(B_VAL_DIM, layer)],
        out_specs=[pl.BlockSpec((BLK, B_W), out_block),
                   pl.BlockSpec((1, B_HEADS, B_KEY_DIM, B_VAL_DIM), lambda b, n: (b, 0, 0, 0))],
        out_shape=[jax.ShapeDtypeStruct((M_ALL, B_W), _BF16),
                   jax.ShapeDtypeStruct((BATCH, B_HEADS, B_KEY_DIM, B_VAL_DIM), _F32)],
        scratch_shapes=[pltpu.VMEM((B_HEADS, B_VAL_DIM, B_KEY_DIM), _F32),
                        pltpu.VMEM((2, 5, BLK, B_W), _F32)],
        compiler_params=_params(2),
        name="hgrn_prompt",
    )(bm, bm, bm, bm, loglb, log1m, _chunk_matrices(BLK, B_CHUNK), w)


def _hgrn_sample_kernel(q_ref, f_ref, i_ref, g_ref, loglb_ref, log1m_ref, lt_ref, w_ref, s0_ref,
                        o_full_ref, o_ref, s_ref):
    del o_full_ref
    rows = HGRN_SAMPLE_GROUP * DEC_SEQ

    def padded(ref):
        return jnp.concatenate([ref[...], jnp.zeros((BLK - rows, B_W), _F32)], axis=0)

    v_all = padded(i_ref)
    g_all = padded(g_ref)
    q_intra, k_intra, q_inter, k_state, e_cum = _hgrn_gates(
        padded(q_ref), padded(f_ref), loglb_ref[...], log1m_ref[...], lt_ref[...], BLK)
    row = lax.broadcasted_iota(jnp.int32, (BLK, BLK), 0)
    col = lax.broadcasted_iota(jnp.int32, (BLK, BLK), 1)
    causal = (row // DEC_SEQ == col // DEC_SEQ) & (col <= row)
    w = w_ref[...]
    for h in range(B_HEADS):
        lanes = slice(h * B_KEY_DIM, (h + 1) * B_KEY_DIM)
        v = v_all[:, lanes].astype(_BF16)
        o = _hgrn_intra(q_intra[:, lanes], k_intra[:, lanes], v, causal)
        qi = q_inter[:, lanes].astype(_BF16)
        ec_t = e_cum[:, lanes].T
        ks_t = k_state[:, lanes].T
        for s in range(HGRN_SAMPLE_GROUP):
            state = s0_ref[s, h]
            o_s = jnp.dot(qi, state.astype(_BF16), preferred_element_type=_F32)
            o = o + jnp.where(row // DEC_SEQ == s, o_s, 0.0)
            ks_seq = jnp.where(col // DEC_SEQ == s, ks_t, 0.0).astype(_BF16)
            update = jnp.dot(ks_seq, v, preferred_element_type=_F32)
            decay = ec_t[:, (s + 1) * DEC_SEQ - 1:(s + 1) * DEC_SEQ]
            s_ref[s, h] = state * decay + update
        o_ref[:, lanes] = _hgrn_finish(o, g_all[:, lanes], w)[0:rows]


def _hgrn_sample(bm, o_full, state, loglb, log1m, w, layer):
    rows = HGRN_SAMPLE_GROUP * DEC_SEQ
    first = M_PROMPT // rows
    gate = lambda j: pl.BlockSpec((rows, B_W), lambda s: (first + s, j))
    state_shape = (HGRN_SAMPLE_GROUP, B_HEADS, B_KEY_DIM, B_VAL_DIM)
    return pl.pallas_call(
        _hgrn_sample_kernel,
        grid=(DEC_BATCH // HGRN_SAMPLE_GROUP,),
        in_specs=[gate(0), gate(1), gate(2), gate(3), _layer_vec(B_W, layer), _layer_vec(B_W, layer),
                  pl.BlockSpec((3 * BLK, BLK), lambda s: (0, 0)),
                  _layer_vec(B_VAL_DIM, layer),
                  pl.BlockSpec((None,) + state_shape, lambda s: (layer, s, 0, 0, 0)),
                  pl.BlockSpec(memory_space=pl.ANY)],
        out_specs=[pl.BlockSpec((rows, B_W), lambda s: (first + s, 0)),
                   pl.BlockSpec(state_shape, lambda s: (s, 0, 0, 0))],
        out_shape=[jax.ShapeDtypeStruct((M_ALL, B_W), _BF16),
                   jax.ShapeDtypeStruct(state.shape[1:], _F32)],
        input_output_aliases={9: 0},
        compiler_params=_params(1),
        name="hgrn_sample",
    )(bm, bm, bm, bm, loglb, log1m, _chunk_matrices(BLK, DEC_SEQ), w, state, o_full)


def _sgu_kernel(u_ref, v_ref, w_ref, b_ref, g_ref, o_ref, vn_ref):
    is_sample = pl.program_id(0) == pl.num_programs(0) - 1
    row = lax.broadcasted_iota(jnp.int32, (BLK, BLK), 0)
    col = lax.broadcasted_iota(jnp.int32, (BLK, BLK), 1)
    same_seq = jnp.logical_or(jnp.logical_not(is_sample), row // DEC_SEQ == col // DEC_SEQ)
    causal = (col <= row) & same_seq
    gain = g_ref[...]
    bias = b_ref[...]
    for g in range(C_GROUPS):
        lanes = slice(g * C_GROUP_DIM, (g + 1) * C_GROUP_DIM)
        v = v_ref[:, lanes]
        vn = v * lax.rsqrt(jnp.mean(v * v, axis=-1, keepdims=True) + EPS) * gain
        w = jnp.where(causal, w_ref[g], 0.0).astype(_BF16)
        z = jnp.dot(w, vn.astype(_BF16), preferred_element_type=_F32) + bias[:, g:g + 1]
        o_ref[:, lanes] = (u_ref[:, lanes] * z).astype(_BF16)
        vn_ref[:, lanes] = vn


def _sgu(cm, w2, b2, gain, layer):
    nblk = M_ALL // BLK
    which = lambda i: i // (nblk - 1)
    return pl.pallas_call(
        _sgu_kernel,
        grid=(nblk,),
        in_specs=[pl.BlockSpec((BLK, C_W), lambda i: (i, 0)),
                  pl.BlockSpec((BLK, C_W), lambda i: (i, 1)),
                  pl.BlockSpec((None, None, C_GROUPS, C_CHUNK, C_CHUNK), lambda i: (layer, which(i), 0, 0, 0)),
                  pl.BlockSpec((None, None, C_CHUNK, C_GROUPS), lambda i: (layer, which(i), 0, 0)),
                  _layer_vec(C_GROUP_DIM, layer)],
        out_specs=[pl.BlockSpec((BLK, C_W), lambda i: (i, 0)),
                   pl.BlockSpec((BLK, C_W), lambda i: (0, 0))],
        out_shape=[jax.ShapeDtypeStruct((M_ALL, C_W), _BF16),
                   jax.ShapeDtypeStruct((M_SAMPLE, C_W), _F32)],
        compiler_params=_params(1),
        name="sgu",
    )(cm, cm, w2, b2, gain)


def _sgu_params(w_spatial, b_spatial):
    reps = BLK // DEC_SEQ
    w_sample = jnp.tile(w_spatial[:, :, :DEC_SEQ, :DEC_SEQ], (1, 1, reps, reps))
    b_sample = jnp.tile(b_spatial[:, :, :DEC_SEQ], (1, 1, reps))
    w2 = jnp.stack([w_spatial, w_sample], axis=1)
    b2 = jnp.stack([jnp.swapaxes(b_spatial, 1, 2), jnp.swapaxes(b_sample, 1, 2)], axis=1)
    return w2, b2


def kernel(x_prompt, x_sample, cache_k, cache_v, state_hgrn, norm_mix, w_in, q_norm, k_norm, sinks,
           lb_logits, hgrn_out_norm, sgu_v_norm, w_spatial, b_spatial, w_branch_a, w_branch_b,
           w_branch_c, w_out, norm_ffn, w_ffn_up, w_ffn_down):
    x = jnp.concatenate([x_prompt.reshape(M_PROMPT, D_MODEL), x_sample.reshape(M_SAMPLE, D_MODEL)], axis=0)
    loglb, log1m = _lower_bounds(lb_logits)
    loglb = loglb.reshape(DEPTH, 1, B_W)
    log1m = log1m.reshape(DEPTH, 1, B_W)
    tables_prompt = _rope_tables(0, SEQ)
    tables_sample = tuple(jnp.tile(t, (ATTN_SAMPLE_GROUP, 1)) for t in _rope_tables(PAST_LEN, DEC_SEQ))
    gq = jnp.tile(q_norm, (1, A_HEADS)).reshape(DEPTH, 1, A_Q_W)
    gk = jnp.tile(k_norm, (1, A_KV_HEADS)).reshape(DEPTH, 1, A_KV_W)
    w_hg = hgrn_out_norm.reshape(DEPTH, 1, B_VAL_DIM)
    w_sg = sgu_v_norm.reshape(DEPTH, 1, C_GROUP_DIM)
    g_mix = norm_mix.reshape(DEPTH, 1, D_MODEL)
    g_ffn = norm_ffn.reshape(DEPTH, 1, D_MODEL)
    sinks = sinks.reshape(DEPTH, 1, A_HEADS)
    w2, b2 = _sgu_params(w_spatial, b_spatial)
    ck_all = cache_k.reshape(DEPTH, DEC_BATCH, WINDOW, A_KV_W)
    cv_all = cache_v.reshape(DEPTH, DEC_BATCH, WINDOW, A_KV_W)

    kp_l, vp_l, sp_l, ks_l, vs_l, ss_l, cs_l = [], [], [], [], [], [], []
    for l in range(DEPTH):
        h = _rmsnorm(x, g_mix, l)
        am = _matmul_cols(h, w_in, l, OFF_A, W_A)
        bm = _matmul_cols(h, w_in, l, OFF_B, W_B)
        cm = _matmul_cols(h, w_in, l, OFF_C, W_C)
        gates = _matmul_cols(h, w_in, l, OFF_G, W_G)

        oa, kc_p, vc_p = _attn_prompt(am, sinks, gq, gk, tables_prompt, l)
        oa, kc_s, vc_s = _attn_sample(am, oa, ck_all, cv_all, sinks, gq, gk, tables_sample, l)
        ob, st_p = _hgrn_prompt(bm, loglb, log1m, w_hg, l)
        ob, st_s = _hgrn_sample(bm, ob, state_hgrn, loglb, log1m, w_hg, l)
        oc, vn_s = _sgu(cm, w2, b2, w_sg, l)
        merged = _merge(oa, ob, oc, gates, w_branch_a, w_branch_b, w_branch_c, l)
        x = _matmul_residual(merged, w_out, l, x, TM, TN)
        h2 = _rmsnorm(x, g_ffn, l)
        act = _ffn_up(h2, w_ffn_up, l)
        x = _matmul_residual(act, w_ffn_down, l, x, TM_DOWN, TN_NARROW)

        kp_l.append(kc_p.reshape(BATCH, WINDOW, A_KV_HEADS, A_HEAD_DIM))
        vp_l.append(vc_p.reshape(BATCH, WINDOW, A_KV_HEADS, A_HEAD_DIM))
        sp_l.append(st_p)
        ks_l.append(kc_s.reshape(DEC_BATCH, WINDOW, A_KV_HEADS, A_HEAD_DIM))
        vs_l.append(vc_s.reshape(DEC_BATCH, WINDOW, A_KV_HEADS, A_HEAD_DIM))
        ss_l.append(st_s)
        cs_l.append(vn_s.reshape(DEC_BATCH, DEC_SEQ, C_GROUPS, C_GROUP_DIM))

    y_prompt = x[:M_PROMPT].reshape(BATCH, SEQ, D_MODEL)
    y_sample = x[M_PROMPT:].reshape(DEC_BATCH, DEC_SEQ, D_MODEL)
    return (y_prompt, y_sample, jnp.stack(kp_l), jnp.stack(vp_l), jnp.stack(sp_l),
            jnp.stack(ks_l), jnp.stack(vs_l), jnp.stack(ss_l), jnp.stack(cs_l))
```

```python
import numpy as np
import jax
import jax.numpy as jnp
from jax import lax
from jax.experimental import pallas as pl
from jax.experimental.pallas import tpu as pltpu

D_MODEL = 2048
BATCH = 4
SEQ = 2048
DEPTH = 4
DEC_BATCH = 32
DEC_SEQ = 4
PAST_LEN = 16384

A_HEADS = 16
A_KV_HEADS = 4
A_HEAD_DIM = 64
A_GROUP = A_HEADS // A_KV_HEADS
WINDOW = 128
ROT_DIM = A_HEAD_DIM // 4
ROT_HALF = ROT_DIM // 2
ROPE_THETA = 500000.0
B_HEADS = 8
B_KEY_DIM = 128
B_VAL_DIM = 128
B_CHUNK = 16
C_GROUPS = 8
C_GROUP_DIM = 128
C_CHUNK = 128
A_Q_W = A_HEADS * A_HEAD_DIM
A_KV_W = A_KV_HEADS * A_HEAD_DIM
B_W = B_HEADS * B_KEY_DIM
C_W = C_GROUPS * C_GROUP_DIM
FFN_DIM = ((8 * D_MODEL + 3 * 256 - 1) // (3 * 256)) * 256
EPS = 1e-6

M_PROMPT = BATCH * SEQ
M_SAMPLE = DEC_BATCH * DEC_SEQ
M_ALL = M_PROMPT + M_SAMPLE

OFF_A = 0
W_A = A_Q_W + 2 * A_KV_W
OFF_B = OFF_A + W_A
W_B = 4 * B_W
OFF_C = OFF_B + W_B
W_C = 2 * C_W
OFF_G = OFF_C + W_C
W_G = 3 * D_MODEL

LANES = 128
BLK = 128
TM = M_ALL // 4
TM_DOWN = M_ALL // 8
TM_NORM = M_ALL // 16
TM_FULL = M_ALL // 16
TN = 512
TN_NARROW = 256
SAMPLE_PAD = 16
ATTN_SAMPLE_GROUP = 4
HGRN_SAMPLE_GROUP = 8
VMEM_LIMIT = 56 * 1024 * 1024

_BF16 = jnp.bfloat16
_F32 = jnp.float32
_NT = (((1,), (1,)), ((), ()))
_TN = (((0,), (0,)), ((), ()))


def _params(n_grid):
    return pltpu.CompilerParams(dimension_semantics=("arbitrary",) * n_grid,
                                vmem_limit_bytes=VMEM_LIMIT)


def _sigmoid(x):
    return 0.5 * jnp.tanh(0.5 * x) + 0.5


def _layer_vec(width, layer):
    return pl.BlockSpec((None, 1, width), lambda *_: (layer, 0, 0))


def _rmsnorm_kernel(x_ref, g_ref, o_ref):
    x = x_ref[...]
    y = x * lax.rsqrt(jnp.mean(x * x, axis=-1, keepdims=True) + EPS)
    o_ref[...] = (y * g_ref[...]).astype(_BF16)


def _rmsnorm(x, g, layer):
    m, d = x.shape
    return pl.pallas_call(
        _rmsnorm_kernel,
        grid=(m // TM_NORM,),
        in_specs=[pl.BlockSpec((TM_NORM, d), lambda i: (i, 0)), _layer_vec(d, layer)],
        out_specs=pl.BlockSpec((TM_NORM, d), lambda i: (i, 0)),
        out_shape=jax.ShapeDtypeStruct((m, d), _BF16),
        compiler_params=_params(1),
        name="rmsnorm",
    )(x, g)


def _join_norm_kernel(xp_ref, xs_ref, g_ref, x_ref, h_ref):
    is_sample = pl.program_id(0) == pl.num_programs(0) - 1
    x = jnp.where(is_sample, xs_ref[...], xp_ref[...])
    x_ref[...] = x
    y = x * lax.rsqrt(jnp.mean(x * x, axis=-1, keepdims=True) + EPS)
    h_ref[...] = (y * g_ref[...]).astype(_BF16)


def _join_norm(x_prompt, x_sample, g, layer):
    d = x_prompt.shape[1]
    nblk = M_ALL // BLK
    rows = pl.BlockSpec((BLK, d), lambda i: (i, 0))
    return pl.pallas_call(
        _join_norm_kernel,
        grid=(nblk,),
        in_specs=[pl.BlockSpec((BLK, d), lambda i: (jnp.minimum(i, nblk - 2), 0)),
                  pl.BlockSpec((BLK, d), lambda i: (0, 0)), _layer_vec(d, layer)],
        out_specs=[rows, rows],
        out_shape=[jax.ShapeDtypeStruct((M_ALL, d), _F32), jax.ShapeDtypeStruct((M_ALL, d), _BF16)],
        compiler_params=_params(1),
        name="join_norm",
    )(x_prompt, x_sample, g)


def _mm_kernel(a_ref, w_ref, o_ref):
    o_ref[...] = jnp.dot(a_ref[...], w_ref[...].astype(_BF16), preferred_element_type=_F32)


def _matmul_cols(a, w, layer, col_off, n_cols):
    m, k = a.shape
    off = col_off // TN
    return pl.pallas_call(
        _mm_kernel,
        grid=(m // TM, n_cols // TN),
        in_specs=[pl.BlockSpec((TM, k), lambda i, j: (i, 0)),
                  pl.BlockSpec((None, k, TN), lambda i, j: (layer, 0, j + off))],
        out_specs=pl.BlockSpec((TM, TN), lambda i, j: (i, j)),
        out_shape=jax.ShapeDtypeStruct((m, n_cols), _F32),
        compiler_params=_params(2),
        name="proj_in",
    )(a, w)


def _mm_res_kernel(a_ref, w_ref, r_ref, o_ref):
    o_ref[...] = r_ref[...] + jnp.dot(a_ref[...], w_ref[...].astype(_BF16),
                                      preferred_element_type=_F32)


def _matmul_residual(a, w, layer, r, tm, tn):
    m, k = a.shape
    n = w.shape[2]
    return pl.pallas_call(
        _mm_res_kernel,
        grid=(m // tm, n // tn),
        in_specs=[pl.BlockSpec((tm, k), lambda i, j: (i, 0)),
                  pl.BlockSpec((None, k, tn), lambda i, j: (layer, 0, j)),
                  pl.BlockSpec((tm, tn), lambda i, j: (i, j))],
        out_specs=pl.BlockSpec((tm, tn), lambda i, j: (i, j)),
        out_shape=jax.ShapeDtypeStruct((m, n), _F32),
        compiler_params=_params(2),
        name="proj_residual",
    )(a, w, r)


def _mm_res_norm_kernel(a_ref, w_ref, r_ref, g_ref, o_ref, h_ref):
    x = r_ref[...] + jnp.dot(a_ref[...], w_ref[...], preferred_element_type=_F32)
    o_ref[...] = x
    y = x * lax.rsqrt(jnp.mean(x * x, axis=-1, keepdims=True) + EPS)
    h_ref[...] = (y * g_ref[...]).astype(_BF16)


def _matmul_residual_norm(a, w_bf16, layer, r, g):
    m, k = a.shape
    n = w_bf16.shape[2]
    rows = lambda width: pl.BlockSpec((TM_FULL, width), lambda i: (i, 0))
    return pl.pallas_call(
        _mm_res_norm_kernel,
        grid=(m // TM_FULL,),
        in_specs=[rows(k),
                  pl.BlockSpec((None, k, n), lambda i: (layer, 0, 0), pipeline_mode=pl.Buffered(1)),
                  rows(n), _layer_vec(n, layer)],
        out_specs=[rows(n), rows(n)],
        out_shape=[jax.ShapeDtypeStruct((m, n), _F32), jax.ShapeDtypeStruct((m, n), _BF16)],
        compiler_params=_params(1),
        name="proj_out_norm",
    )(a, w_bf16, r, g)


def _ffn_up_kernel(a_ref, wg_ref, wu_ref, o_ref):
    a = a_ref[...]
    g = jnp.dot(a, wg_ref[...].astype(_BF16), preferred_element_type=_F32)
    u = jnp.dot(a, wu_ref[...].astype(_BF16), preferred_element_type=_F32)
    o_ref[...] = (g * _sigmoid(g) * u).astype(_BF16)


def _ffn_up(a, w_up, layer):
    m, k = a.shape
    tn = TN_NARROW
    nj = FFN_DIM // tn
    return pl.pallas_call(
        _ffn_up_kernel,
        grid=(m // TM, nj),
        in_specs=[pl.BlockSpec((TM, k), lambda i, j: (i, 0)),
                  pl.BlockSpec((None, k, tn), lambda i, j: (layer, 0, j)),
                  pl.BlockSpec((None, k, tn), lambda i, j: (layer, 0, j + nj))],
        out_specs=pl.BlockSpec((TM, tn), lambda i, j: (i, j)),
        out_shape=jax.ShapeDtypeStruct((m, FFN_DIM), _BF16),
        compiler_params=_params(2),
        name="ffn_up",
    )(a, w_up, w_up)


def _merge_kernel(oa_ref, ob_ref, oc_ref, wa_ref, wb_ref, wc_ref, ga_ref, gb_ref, gc_ref, o_ref):
    ya = jnp.dot(oa_ref[...], wa_ref[...].astype(_BF16), preferred_element_type=_F32)
    yb = jnp.dot(ob_ref[...], wb_ref[...].astype(_BF16), preferred_element_type=_F32)
    yc = jnp.dot(oc_ref[...], wc_ref[...].astype(_BF16), preferred_element_type=_F32)
    merged = _sigmoid(ga_ref[...]) * ya + _sigmoid(gb_ref[...]) * yb + _sigmoid(gc_ref[...]) * yc
    o_ref[...] = merged.astype(_BF16)


def _merge(oa, ob, oc, gates, wa, wb, wc, layer):
    m, k = oa.shape
    tn = TN_NARROW
    nj = D_MODEL // tn
    branch = pl.BlockSpec((TM, k), lambda i, j: (i, 0))
    weight = pl.BlockSpec((None, k, tn), lambda i, j: (layer, 0, j))
    return pl.pallas_call(
        _merge_kernel,
        grid=(m // TM, nj),
        in_specs=[branch, branch, branch, weight, weight, weight,
                  pl.BlockSpec((TM, tn), lambda i, j: (i, j)),
                  pl.BlockSpec((TM, tn), lambda i, j: (i, j + nj)),
                  pl.BlockSpec((TM, tn), lambda i, j: (i, j + 2 * nj))],
        out_specs=pl.BlockSpec((TM, tn), lambda i, j: (i, j)),
        out_shape=jax.ShapeDtypeStruct((m, D_MODEL), _BF16),
        compiler_params=_params(2),
        name="merge",
    )(oa, ob, oc, wa, wb, wc, gates, gates, gates)


def _rope_tables(p0, rows):
    pos = (p0 + jnp.arange(rows, dtype=jnp.int32)).astype(_F32)
    inv_freq = jnp.power(jnp.float32(ROPE_THETA), -jnp.arange(ROT_HALF, dtype=_F32) / ROT_HALF)
    ang = pos[:, None] * inv_freq[None, :]
    cos, sin = jnp.cos(ang), jnp.sin(ang)
    rest = A_HEAD_DIM - ROT_DIM
    zeros = jnp.zeros((rows, ROT_HALF), _F32)
    pad = jnp.zeros((rows, rest), _F32)
    c = jnp.concatenate([cos, cos, jnp.ones((rows, rest), _F32)], axis=1)
    s1 = jnp.concatenate([-sin, zeros, pad], axis=1)
    s2 = jnp.concatenate([zeros, sin, pad], axis=1)
    return tuple(jnp.tile(t, (1, LANES // A_HEAD_DIM)) for t in (c, s1, s2))


def _rope(x, c, s1, s2):
    width = x.shape[1]
    reps = width // c.shape[1]
    c, s1, s2 = (jnp.concatenate([t] * reps, axis=1) for t in (c, s1, s2))
    ahead = pltpu.roll(x, width - ROT_HALF, axis=1)
    behind = pltpu.roll(x, ROT_HALF, axis=1)
    return x * c + ahead * s1 + behind * s2


def _head_mean_matrix():
    i = np.arange(LANES)
    same = (i[:, None] // A_HEAD_DIM) == (i[None, :] // A_HEAD_DIM)
    return jnp.asarray(same.astype(np.float32) / A_HEAD_DIM, dtype=_BF16)


def _qk_prep(x, gain, head_mean, c, s1, s2, scale):
    sq = x * x
    hi = sq.astype(_BF16)
    lo = (sq - hi.astype(_F32)).astype(_BF16)
    ms = jnp.concatenate(
        [jnp.dot(hi[:, l:l + LANES], head_mean, preferred_element_type=_F32)
         + jnp.dot(lo[:, l:l + LANES], head_mean, preferred_element_type=_F32)
         for l in range(0, x.shape[1], LANES)], axis=1)
    return _rope(x * gain, c, s1, s2) * (lax.rsqrt(ms + EPS) * scale)


def _low_half(shape):
    return lax.broadcasted_iota(jnp.int32, shape, 1) < A_HEAD_DIM


def _both_halves(col, half):
    low = _low_half(col.shape)
    sel = jnp.where(low if half == 0 else jnp.logical_not(low), col, 0.0)
    return sel + pltpu.roll(sel, A_HEAD_DIM, axis=1)


def _kv_pairs(k, v, h):
    pair, half = divmod(h, 2)
    lanes = slice(pair * LANES, (pair + 1) * LANES)
    return _both_halves(k[:, lanes], half), _both_halves(v[:, lanes], half)


def _mxu_tiles(k_pair, v_pair):
    v_pair = v_pair.astype(_BF16)
    return k_pair.astype(_BF16), jnp.concatenate([v_pair, jnp.ones(v_pair.shape, _BF16)], axis=1)


def _kv_tiles(k, v, h):
    return _mxu_tiles(*_kv_pairs(k, v, h))


def _query_rows(qn, h):
    low = _low_half((qn.shape[0], LANES))
    parts = []
    for g in range(A_GROUP):
        pair, half = divmod(h * A_GROUP + g, 2)
        keep = low if half == 0 else jnp.logical_not(low)
        parts.append(jnp.where(keep, qn[:, pair * LANES:(pair + 1) * LANES], 0.0))
    return jnp.concatenate(parts, axis=0).astype(_BF16)


def _sink_rows(sinks_ref, h, t):
    return jnp.concatenate([jnp.full((t, LANES), sinks_ref[0, h * A_GROUP + g], _F32)
                            for g in range(A_GROUP)], axis=0)


def _store_heads(o_ref, o, h, t):
    low = _low_half((t, LANES))
    for j in range(A_GROUP // 2):
        even = o[(2 * j) * t:(2 * j + 1) * t]
        odd = o[(2 * j + 1) * t:(2 * j + 2) * t]
        pair = (h * A_GROUP) // 2 + j
        o_ref[:, pair * LANES:(pair + 1) * LANES] = jnp.where(low, even, odd).astype(o_ref.dtype)


def _attn_prompt_kernel(sinks_ref, q_ref, kv_ref, c_ref, s1_ref, s2_ref, gq_ref, gk_ref, hm_ref,
                        o_ref, kc_ref, vc_ref, kprev, vprev):
    b = pl.program_id(0)
    n = pl.program_id(1)
    nb = pl.num_programs(1) - 1

    @pl.when(n == 0)
    def _():
        kprev[...] = jnp.zeros(kprev.shape, _F32)
        vprev[...] = jnp.zeros(vprev.shape, _F32)

    @pl.when(n < nb)
    def _():
        kv = kv_ref[...]
        v = kv[:, A_KV_W:]
        c, s1, s2 = c_ref[...], s1_ref[...], s2_ref[...]
        qn = _qk_prep(q_ref[...], gq_ref[...], hm_ref[...], c, s1, s2, A_HEAD_DIM ** -0.5)
        kn = _qk_prep(kv[:, :A_KV_W], gk_ref[...], hm_ref[...], c, s1, s2, 1.0)
        kc_ref[0] = kn
        vc_ref[0] = v

        rows = A_GROUP * BLK
        row = lax.broadcasted_iota(jnp.int32, (rows, WINDOW), 0) % BLK
        col = lax.broadcasted_iota(jnp.int32, (rows, WINDOW), 1)
        before = col > row
        has_prev = n > 0
        for h in range(A_KV_HEADS):
            k_pair, v_pair = _kv_pairs(kn, v, h)
            k_tile, v_tile = _mxu_tiles(k_pair, v_pair)
            k_before, v_before = _mxu_tiles(kprev[h], vprev[h])
            kprev[h] = k_pair
            vprev[h] = v_pair
            q4 = _query_rows(qn, h)
            s_prev = lax.dot_general(q4, k_before, _NT, preferred_element_type=_F32)
            s_cur = lax.dot_general(q4, k_tile, _NT, preferred_element_type=_F32)
            s = jnp.where(before, jnp.where(has_prev, s_prev, -jnp.inf), s_cur)
            sink = _sink_rows(sinks_ref, h, BLK)
            m = jnp.maximum(jnp.broadcast_to(jnp.max(s, axis=-1, keepdims=True), s.shape), sink)
            p = jnp.exp(s - m)
            acc = (jnp.dot(jnp.where(before, p, 0.0).astype(_BF16), v_before, preferred_element_type=_F32)
                   + jnp.dot(jnp.where(before, 0.0, p).astype(_BF16), v_tile, preferred_element_type=_F32))
            o = acc[:, :LANES] / (acc[:, LANES:] + jnp.exp(sink - m))
            _store_heads(o_ref, o, h, BLK)

    @pl.when((n == nb) & (b == pl.num_programs(0) - 1))
    def _():
        o_ref[...] = jnp.zeros(o_ref.shape, o_ref.dtype)


def _attn_prompt(a, sinks, gq, gk, tables, layer):
    nb = SEQ // BLK
    c, s1, s2 = tables
    block = lambda b, n: b * nb + jnp.minimum(n, nb - 1)
    table = pl.BlockSpec((BLK, LANES), lambda b, n: (jnp.minimum(n, nb - 1), 0))
    cache = pl.BlockSpec((1, BLK, A_KV_W), lambda b, n: (b, 0, 0))
    const = lambda w: pl.BlockSpec((w, w), lambda b, n: (0, 0))

    def out_block(b, n):
        tail = (n == nb) & (b == BATCH - 1)
        return (jnp.where(tail, BATCH * nb, block(b, n)), 0)

    return pl.pallas_call(
        _attn_prompt_kernel,
        grid=(BATCH, nb + 1),
        in_specs=[pl.BlockSpec((None, 1, A_HEADS), lambda b, n: (layer, 0, 0), memory_space=pltpu.SMEM),
                  pl.BlockSpec((BLK, A_Q_W), lambda b, n: (block(b, n), 0)),
                  pl.BlockSpec((BLK, 2 * A_KV_W), lambda b, n: (block(b, n), A_Q_W // (2 * A_KV_W))),
                  table, table, table,
                  _layer_vec(A_Q_W, layer), _layer_vec(A_KV_W, layer), const(LANES)],
        out_specs=[pl.BlockSpec((BLK, A_Q_W), out_block), cache, cache],
        out_shape=[jax.ShapeDtypeStruct((M_ALL, A_Q_W), _BF16),
                   jax.ShapeDtypeStruct((BATCH, WINDOW, A_KV_W), _F32),
                   jax.ShapeDtypeStruct((BATCH, WINDOW, A_KV_W), _F32)],
        scratch_shapes=[pltpu.VMEM((A_KV_HEADS, BLK, LANES), _F32),
                        pltpu.VMEM((A_KV_HEADS, BLK, LANES), _F32)],
        compiler_params=_params(2),
        name="attn_prompt",
    )(sinks, a, a, c, s1, s2, gq, gk, _head_mean_matrix())


def _attn_sample_kernel(sinks_ref, qkv_ref, ck_ref, cv_ref, c_ref, s1_ref, s2_ref, gq_ref, gk_ref,
                        hm_ref, o_full_ref, o_ref, kc_ref, vc_ref, kbuf, vbuf, knew, vnew):
    del o_full_ref
    grp = ATTN_SAMPLE_GROUP
    t = grp * DEC_SEQ
    qkv = qkv_ref[...]
    v = qkv[:, A_Q_W + A_KV_W:]
    c, s1, s2 = c_ref[...], s1_ref[...], s2_ref[...]
    qn = _qk_prep(qkv[:, :A_Q_W], gq_ref[...], hm_ref[...], c, s1, s2, A_HEAD_DIM ** -0.5)
    kn = _qk_prep(qkv[:, A_Q_W:A_Q_W + A_KV_W], gk_ref[...], hm_ref[...], c, s1, s2, 1.0)

    knew[...] = jnp.zeros(knew.shape, _F32)
    vnew[...] = jnp.zeros(vnew.shape, _F32)
    knew[0:t, :] = kn
    vnew[0:t, :] = v
    for s in range(grp):
        kbuf[s, 0:WINDOW, :] = ck_ref[s]
        vbuf[s, 0:WINDOW, :] = cv_ref[s]
        kbuf[s, WINDOW:, :] = knew[s * DEC_SEQ:s * DEC_SEQ + 8, :]
        vbuf[s, WINDOW:, :] = vnew[s * DEC_SEQ:s * DEC_SEQ + 8, :]
        kc_ref[s] = kbuf[s, DEC_SEQ:DEC_SEQ + WINDOW, :]
        vc_ref[s] = vbuf[s, DEC_SEQ:DEC_SEQ + WINDOW, :]

    rows = A_GROUP * t
    r_old = lax.broadcasted_iota(jnp.int32, (rows, grp * WINDOW), 0) % t
    c_old = lax.broadcasted_iota(jnp.int32, (rows, grp * WINDOW), 1)
    see_old = (c_old // WINDOW == r_old // DEC_SEQ) & (c_old % WINDOW > r_old % DEC_SEQ)
    r_new = lax.broadcasted_iota(jnp.int32, (rows, WINDOW), 0) % t
    c_new = lax.broadcasted_iota(jnp.int32, (rows, WINDOW), 1)
    see_new = (c_new < t) & (c_new // DEC_SEQ == r_new // DEC_SEQ) & (c_new % DEC_SEQ <= r_new % DEC_SEQ)
    pad_k = jnp.zeros((WINDOW - t, LANES), _BF16)
    pad_v = jnp.zeros((WINDOW - t, 2 * LANES), _BF16)
    for h in range(A_KV_HEADS):
        old = [_kv_tiles(ck_ref[s], cv_ref[s], h) for s in range(grp)]
        k_old = jnp.concatenate([kt for kt, _ in old], axis=0)
        v_old = jnp.concatenate([vt for _, vt in old], axis=0)
        k_new, v_new = _kv_tiles(kn, v, h)
        k_new = jnp.concatenate([k_new, pad_k], axis=0)
        v_new = jnp.concatenate([v_new, pad_v], axis=0)
        q4 = _query_rows(qn, h)
        s_old = jnp.where(see_old, lax.dot_general(q4, k_old, _NT, preferred_element_type=_F32), -jnp.inf)
        s_new = jnp.where(see_new, lax.dot_general(q4, k_new, _NT, preferred_element_type=_F32), -jnp.inf)
        sink = _sink_rows(sinks_ref, h, t)
        top = jnp.maximum(jnp.max(s_old, axis=-1, keepdims=True), jnp.max(s_new, axis=-1, keepdims=True))
        m = jnp.maximum(jnp.broadcast_to(top, sink.shape), sink)
        p_old = jnp.exp(s_old - jnp.concatenate([m] * grp, axis=1))
        p_new = jnp.exp(s_new - m)
        acc = (jnp.dot(p_old.astype(_BF16), v_old, preferred_element_type=_F32)
               + jnp.dot(p_new.astype(_BF16), v_new, preferred_element_type=_F32))
        o = acc[:, :LANES] / (acc[:, LANES:] + jnp.exp(sink - m))
        _store_heads(o_ref, o, h, t)


def _attn_sample(a, o_full, cache_k, cache_v, sinks, gq, gk, tables, layer):
    c, s1, s2 = tables
    grp = ATTN_SAMPLE_GROUP
    t = grp * DEC_SEQ
    first = M_PROMPT // t
    table = pl.BlockSpec((t, LANES), lambda b: (0, 0))
    cache_in = pl.BlockSpec((None, grp, WINDOW, A_KV_W), lambda b: (layer, b, 0, 0))
    cache_out = pl.BlockSpec((grp, WINDOW, A_KV_W), lambda b: (b, 0, 0))
    const = lambda w: pl.BlockSpec((w, w), lambda b: (0, 0))
    return pl.pallas_call(
        _attn_sample_kernel,
        grid=(DEC_BATCH // grp,),
        in_specs=[pl.BlockSpec((None, 1, A_HEADS), lambda b: (layer, 0, 0), memory_space=pltpu.SMEM),
                  pl.BlockSpec((t, W_A), lambda b: (first + b, 0)),
                  cache_in, cache_in, table, table, table,
                  _layer_vec(A_Q_W, layer), _layer_vec(A_KV_W, layer), const(LANES),
                  pl.BlockSpec(memory_space=pl.ANY)],
        out_specs=[pl.BlockSpec((t, A_Q_W), lambda b: (first + b, 0)), cache_out, cache_out],
        out_shape=[jax.ShapeDtypeStruct((M_ALL, A_Q_W), _BF16),
                   jax.ShapeDtypeStruct((DEC_BATCH, WINDOW, A_KV_W), _F32),
                   jax.ShapeDtypeStruct((DEC_BATCH, WINDOW, A_KV_W), _F32)],
        scratch_shapes=[pltpu.VMEM((grp, WINDOW + 8, A_KV_W), _F32),
                        pltpu.VMEM((grp, WINDOW + 8, A_KV_W), _F32),
                        pltpu.VMEM((t + 8, A_KV_W), _F32), pltpu.VMEM((t + 8, A_KV_W), _F32)],
        input_output_aliases={10: 0},
        compiler_params=_params(1),
        name="attn_sample",
    )(sinks, a, cache_k, cache_v, c, s1, s2, gq, gk, _head_mean_matrix(),
      o_full)


def _lower_bound_kernel(logits_ref, loglb_ref, log1m_ref):
    x = logits_ref[...]
    e = jnp.exp(x - jnp.max(x, axis=0, keepdims=True))
    sm = e / jnp.sum(e, axis=0, keepdims=True)
    acc = sm[0:1]
    rows = [acc]
    for l in range(1, DEPTH):
        acc = acc + sm[l:l + 1]
        rows.append(acc)
    for l in range(DEPTH):
        lb = rows[l] - rows[0]
        loglb_ref[l:l + 1, :] = jnp.log(lb)
        log1m_ref[l:l + 1, :] = jnp.log1p(-lb)


def _lower_bounds(lb_logits):
    shape = jax.ShapeDtypeStruct(lb_logits.shape, _F32)
    return pl.pallas_call(_lower_bound_kernel, out_shape=[shape, shape], name="hgrn_lower_bounds")(lb_logits)


def _chunk_matrices(rows, chunk):
    t = np.arange(rows)[:, None]
    s = np.arange(rows)[None, :]
    same = (t // chunk) == (s // chunk)
    tri = (same & (s <= t)).astype(np.float32)
    ref = (same & ((s % chunk) <= chunk // 2)).astype(np.float32)
    last = same.astype(np.float32)
    return jnp.asarray(np.concatenate([tri, tri - ref, last - tri], axis=0), dtype=_BF16)


def _split2(x):
    hi = x.astype(_BF16)
    return hi, (x - hi.astype(_F32)).astype(_BF16)


def _hgrn_gates(q, z, loglb, log1m, lt, rows):
    log_sig = jnp.minimum(z, 0.0) - jnp.log(1.0 + jnp.exp(-jnp.abs(z)))
    b = log1m + log_sig
    log_f = jnp.maximum(loglb, b) + jnp.log(1.0 + jnp.exp(-jnp.abs(loglb - b)))
    kk = -jnp.tanh(0.5 * log_f) * (jnp.exp(log_f) + 1.0)
    hi, lo = _split2(log_f)
    cums = jnp.dot(lt, hi, preferred_element_type=_F32) + jnp.dot(lt, lo, preferred_element_type=_F32)
    cum = cums[0:rows]
    cum_ref = cums[rows:2 * rows]
    cum_end = cums[2 * rows:3 * rows]
    e_cum = jnp.exp(cum)
    q_intra = q * jnp.exp(cum_ref)
    k_intra = kk * jnp.exp(-cum_ref)
    q_inter = q * e_cum
    k_state = kk * jnp.exp(cum_end)
    return q_intra, k_intra, q_inter, k_state, e_cum


def _hgrn_finish(o, g, w):
    y = o * lax.rsqrt(jnp.mean(o * o, axis=-1, keepdims=True) + EPS) * w
    return (y * (g * _sigmoid(g))).astype(_BF16)


def _hgrn_intra(q_intra, k_intra, v, causal):
    att = lax.dot_general(q_intra.astype(_BF16), k_intra.astype(_BF16), _NT, preferred_element_type=_F32)
    att = jnp.where(causal, att, 0.0)
    return jnp.dot(att.astype(_BF16), v, preferred_element_type=_F32)


def _hgrn_prompt_kernel(q_ref, f_ref, i_ref, g_ref, loglb_ref, log1m_ref, lt_ref, w_ref,
                        o_ref, s_ref, st, hand):
    b = pl.program_id(0)
    n = pl.program_id(1)
    nb = pl.num_programs(1) - 2

    @pl.when(n == 0)
    def _():
        st[...] = jnp.zeros(st.shape, _F32)
        hand[...] = jnp.zeros(hand.shape, _F32)

    def step(slot, done):
        row = lax.broadcasted_iota(jnp.int32, (BLK, BLK), 0)
        col = lax.broadcasted_iota(jnp.int32, (BLK, BLK), 1)
        causal = (row // B_CHUNK == col // B_CHUNK) & (col <= row)
        chunks = [slice(c * B_CHUNK, (c + 1) * B_CHUNK) for c in range(BLK // B_CHUNK)]
        w = w_ref[...]
        lt = lt_ref[...]
        heads = [slice(h * B_KEY_DIM, (h + 1) * B_KEY_DIM) for h in range(B_HEADS)]
        intra, updates = [], []
        for lanes in heads:
            for j, t in enumerate(_hgrn_gates(q_ref[:, lanes], f_ref[:, lanes], loglb_ref[:, lanes],
                                              log1m_ref[:, lanes], lt, BLK)):
                hand[slot, j, :, lanes] = t
            v = i_ref[:, lanes].astype(_BF16)
            ks = hand[done, 3, :, lanes].astype(_BF16)
            intra.append(_hgrn_intra(hand[done, 0, :, lanes], hand[done, 1, :, lanes], v, causal))
            updates.append([lax.dot_general(v[rows], ks[rows], _TN, preferred_element_type=_F32)
                            for rows in chunks])
        before = []
        for h, lanes in enumerate(heads):
            state_t = st[h]
            seen = []
            for rows, update in zip(chunks, updates[h]):
                seen.append(state_t.astype(_BF16))
                state_t = state_t * hand[done, 4, rows.stop - 1:rows.stop, lanes] + update
            st[h] = state_t
            before.append(seen)
        for h, lanes in enumerate(heads):
            qi = hand[done, 2, :, lanes].astype(_BF16)
            o_inter = [lax.dot_general(qi[rows], s_t, _NT, preferred_element_type=_F32)
                       for rows, s_t in zip(chunks, before[h])]
            o_ref[:, lanes] = _hgrn_finish(intra[h] + jnp.concatenate(o_inter, axis=0), g_ref[:, lanes], w)

    for parity in range(2):
        pl.when((n <= nb) & (n % 2 == parity))(lambda parity=parity: step(parity, 1 - parity))

    @pl.when(n == nb)
    def _():
        for h in range(B_HEADS):
            s_ref[0, h] = st[h].T

    @pl.when((n == nb + 1) & (b == pl.num_programs(0) - 1))
    def _():
        o_ref[...] = jnp.zeros(o_ref.shape, o_ref.dtype)


def _hgrn_prompt(bm, loglb, log1m, w, layer):
    nb = SEQ // BLK

    def ahead(j):
        return pl.BlockSpec((BLK, B_W), lambda b, n: (b * nb + jnp.minimum(n, nb - 1), j))

    def behind(j):
        return pl.BlockSpec((BLK, B_W), lambda b, n: (b * nb + jnp.clip(n - 1, 0, nb - 1), j))

    def out_block(b, n):
        tail = (n == nb + 1) & (b == BATCH - 1)
        return (jnp.where(tail, BATCH * nb, b * nb + jnp.clip(n - 1, 0, nb - 1)), 0)

    return pl.pallas_call(
        _hgrn_prompt_kernel,
        grid=(BATCH, nb + 2),
        in_specs=[ahead(0), ahead(1), behind(2), behind(3), _layer_vec(B_W, layer), _layer_vec(B_W, layer),
                  pl.BlockSpec((3 * BLK, BLK), lambda b, n: (0, 0)),
                  _layer_vec(B_VAL_DIM, layer)],
        out_specs=[pl.BlockSpec((BLK, B_W), out_block),
                   pl.BlockSpec((1, B_HEADS, B_KEY_DIM, B_VAL_DIM), lambda b, n: (b, 0, 0, 0))],
        out_shape=[jax.ShapeDtypeStruct((M_ALL, B_W), _BF16),
                   jax.ShapeDtypeStruct((BATCH, B_HEADS, B_KEY_DIM, B_VAL_DIM), _F32)],
        scratch_shapes=[pltpu.VMEM((B_HEADS, B_VAL_DIM, B_KEY_DIM), _F32),
                        pltpu.VMEM((2, 5, BLK, B_W), _F32)],
        compiler_params=_params(2),
        name="hgrn_prompt",
    )(bm, bm, bm, bm, loglb, log1m, _chunk_matrices(BLK, B_CHUNK), w)


def _hgrn_sample_kernel(q_ref, f_ref, i_ref, g_ref, loglb_ref, log1m_ref, lt_ref, w_ref, s0_ref,
                        o_full_ref, o_ref, s_ref):
    del o_full_ref
    rows = HGRN_SAMPLE_GROUP * DEC_SEQ

    def padded(ref):
        return jnp.concatenate([ref[...], jnp.zeros((BLK - rows, B_W), _F32)], axis=0)

    v_all = padded(i_ref)
    g_all = padded(g_ref)
    q_intra, k_intra, q_inter, k_state, e_cum = _hgrn_gates(
        padded(q_ref), padded(f_ref), loglb_ref[...], log1m_ref[...], lt_ref[...], BLK)
    row = lax.broadcasted_iota(jnp.int32, (BLK, BLK), 0)
    col = lax.broadcasted_iota(jnp.int32, (BLK, BLK), 1)
    causal = (row // DEC_SEQ == col // DEC_SEQ) & (col <= row)
    w = w_ref[...]
    for h in range(B_HEADS):
        lanes = slice(h * B_KEY_DIM, (h + 1) * B_KEY_DIM)
        v = v_all[:, lanes].astype(_BF16)
        o = _hgrn_intra(q_intra[:, lanes], k_intra[:, lanes], v, causal)
        qi = q_inter[:, lanes].astype(_BF16)
        ec_t = e_cum[:, lanes].T
        ks_t = k_state[:, lanes].T
        for s in range(HGRN_SAMPLE_GROUP):
            state = s0_ref[s, h]
            o_s = jnp.dot(qi, state.astype(_BF16), preferred_element_type=_F32)
            o = o + jnp.where(row // DEC_SEQ == s, o_s, 0.0)
            ks_seq = jnp.where(col // DEC_SEQ == s, ks_t, 0.0).astype(_BF16)
            update = jnp.dot(ks_seq, v, preferred_element_type=_F32)
            decay = ec_t[:, (s + 1) * DEC_SEQ - 1:(s + 1) * DEC_SEQ]
            s_ref[s, h] = state * decay + update
        o_ref[:, lanes] = _hgrn_finish(o, g_all[:, lanes], w)[0:rows]


def _hgrn_sample(bm, o_full, state, loglb, log1m, w, layer):
    rows = HGRN_SAMPLE_GROUP * DEC_SEQ
    first = M_PROMPT // rows
    gate = lambda j: pl.BlockSpec((rows, B_W), lambda s: (first + s, j))
    state_shape = (HGRN_SAMPLE_GROUP, B_HEADS, B_KEY_DIM, B_VAL_DIM)
    return pl.pallas_call(
        _hgrn_sample_kernel,
        grid=(DEC_BATCH // HGRN_SAMPLE_GROUP,),
        in_specs=[gate(0), gate(1), gate(2), gate(3), _layer_vec(B_W, layer), _layer_vec(B_W, layer),
                  pl.BlockSpec((3 * BLK, BLK), lambda s: (0, 0)),
                  _layer_vec(B_VAL_DIM, layer),
                  pl.BlockSpec((None,) + state_shape, lambda s: (layer, s, 0, 0, 0)),
                  pl.BlockSpec(memory_space=pl.ANY)],
        out_specs=[pl.BlockSpec((rows, B_W), lambda s: (first + s, 0)),
                   pl.BlockSpec(state_shape, lambda s: (s, 0, 0, 0))],
        out_shape=[jax.ShapeDtypeStruct((M_ALL, B_W), _BF16),
                   jax.ShapeDtypeStruct(state.shape[1:], _F32)],
        input_output_aliases={9: 0},
        compiler_params=_params(1),
        name="hgrn_sample",
    )(bm, bm, bm, bm, loglb, log1m, _chunk_matrices(BLK, DEC_SEQ), w, state, o_full)


def _sgu_kernel(u_ref, v_ref, w_ref, b_ref, g_ref, o_ref, vn_ref):
    is_sample = pl.program_id(0) == pl.num_programs(0) - 1
    row = lax.broadcasted_iota(jnp.int32, (BLK, BLK), 0)
    col = lax.broadcasted_iota(jnp.int32, (BLK, BLK), 1)
    same_seq = jnp.logical_or(jnp.logical_not(is_sample), row // DEC_SEQ == col // DEC_SEQ)
    causal = (col <= row) & same_seq
    gain = g_ref[...]
    bias = b_ref[...]
    for g in range(C_GROUPS):
        lanes = slice(g * C_GROUP_DIM, (g + 1) * C_GROUP_DIM)
        v = v_ref[:, lanes]
        vn = v * lax.rsqrt(jnp.mean(v * v, axis=-1, keepdims=True) + EPS) * gain
        w = jnp.where(causal, w_ref[g], 0.0).astype(_BF16)
        z = jnp.dot(w, vn.astype(_BF16), preferred_element_type=_F32) + bias[:, g:g + 1]
        o_ref[:, lanes] = (u_ref[:, lanes] * z).astype(_BF16)
        vn_ref[:, lanes] = vn


def _sgu(cm, w2, b2, gain, layer):
    nblk = M_ALL // BLK
    which = lambda i: i // (nblk - 1)
    return pl.pallas_call(
        _sgu_kernel,
        grid=(nblk,),
        in_specs=[pl.BlockSpec((BLK, C_W), lambda i: (i, 0)),
                  pl.BlockSpec((BLK, C_W), lambda i: (i, 1)),
                  pl.BlockSpec((None, None, C_GROUPS, C_CHUNK, C_CHUNK), lambda i: (layer, which(i), 0, 0, 0)),
                  pl.BlockSpec((None, None, C_CHUNK, C_GROUPS), lambda i: (layer, which(i), 0, 0)),
                  _layer_vec(C_GROUP_DIM, layer)],
        out_specs=[pl.BlockSpec((BLK, C_W), lambda i: (i, 0)),
                   pl.BlockSpec((BLK, C_W), lambda i: (0, 0))],
        out_shape=[jax.ShapeDtypeStruct((M_ALL, C_W), _BF16),
                   jax.ShapeDtypeStruct((M_SAMPLE, C_W), _F32)],
        compiler_params=_params(1),
        name="sgu",
    )(cm, cm, w2, b2, gain)


def _sgu_params(w_spatial, b_spatial):
    reps = BLK // DEC_SEQ
    w_sample = jnp.tile(w_spatial[:, :, :DEC_SEQ, :DEC_SEQ], (1, 1, reps, reps))
    b_sample = jnp.tile(b_spatial[:, :, :DEC_SEQ], (1, 1, reps))
    w2 = jnp.stack([w_spatial, w_sample], axis=1)
    b2 = jnp.stack([jnp.swapaxes(b_spatial, 1, 2), jnp.swapaxes(b_sample, 1, 2)], axis=1)
    return w2, b2


def kernel(x_prompt, x_sample, cache_k, cache_v, state_hgrn, norm_mix, w_in, q_norm, k_norm, sinks,
           lb_logits, hgrn_out_norm, sgu_v_norm, w_spatial, b_spatial, w_branch_a, w_branch_b,
           w_branch_c, w_out, norm_ffn, w_ffn_up, w_ffn_down):
    loglb, log1m = _lower_bounds(lb_logits)
    loglb = loglb.reshape(DEPTH, 1, B_W)
    log1m = log1m.reshape(DEPTH, 1, B_W)
    tables_prompt = _rope_tables(0, SEQ)
    tables_sample = tuple(jnp.tile(t, (ATTN_SAMPLE_GROUP, 1)) for t in _rope_tables(PAST_LEN, DEC_SEQ))
    gq = jnp.tile(q_norm, (1, A_HEADS)).reshape(DEPTH, 1, A_Q_W)
    gk = jnp.tile(k_norm, (1, A_KV_HEADS)).reshape(DEPTH, 1, A_KV_W)
    w_hg = hgrn_out_norm.reshape(DEPTH, 1, B_VAL_DIM)
    w_sg = sgu_v_norm.reshape(DEPTH, 1, C_GROUP_DIM)
    g_mix = norm_mix.reshape(DEPTH, 1, D_MODEL)
    g_ffn = norm_ffn.reshape(DEPTH, 1, D_MODEL)
    sinks = sinks.reshape(DEPTH, 1, A_HEADS)
    w_out_bf16 = w_out.astype(_BF16)
    w2, b2 = _sgu_params(w_spatial, b_spatial)
    ck_all = cache_k.reshape(DEPTH, DEC_BATCH, WINDOW, A_KV_W)
    cv_all = cache_v.reshape(DEPTH, DEC_BATCH, WINDOW, A_KV_W)

    kp_l, vp_l, sp_l, ks_l, vs_l, ss_l, cs_l = [], [], [], [], [], [], []
    for l in range(DEPTH):
        if l == 0:
            x, h = _join_norm(x_prompt.reshape(M_PROMPT, D_MODEL), x_sample.reshape(M_SAMPLE, D_MODEL), g_mix, l)
        else:
            h = _rmsnorm(x, g_mix, l)
        am = _matmul_cols(h, w_in, l, OFF_A, W_A)
        bm = _matmul_cols(h, w_in, l, OFF_B, W_B)
        cm = _matmul_cols(h, w_in, l, OFF_C, W_C)
        gates = _matmul_cols(h, w_in, l, OFF_G, W_G)

        oa, kc_p, vc_p = _attn_prompt(am, sinks, gq, gk, tables_prompt, l)
        oa, kc_s, vc_s = _attn_sample(am, oa, ck_all, cv_all, sinks, gq, gk, tables_sample, l)
        ob, st_p = _hgrn_prompt(bm, loglb, log1m, w_hg, l)
        ob, st_s = _hgrn_sample(bm, ob, state_hgrn, loglb, log1m, w_hg, l)
        oc, vn_s = _sgu(cm, w2, b2, w_sg, l)
        merged = _merge(oa, ob, oc, gates, w_branch_a, w_branch_b, w_branch_c, l)
        x, h2 = _matmul_residual_norm(merged, w_out_bf16, l, x, g_ffn)
        act = _ffn_up(h2, w_ffn_up, l)
        x = _matmul_residual(act, w_ffn_down, l, x, TM_DOWN, TN_NARROW)

        kp_l.append(kc_p.reshape(BATCH, WINDOW, A_KV_HEADS, A_HEAD_DIM))
        vp_l.append(vc_p.reshape(BATCH, WINDOW, A_KV_HEADS, A_HEAD_DIM))
        sp_l.append(st_p)
        ks_l.append(kc_s.reshape(DEC_BATCH, WINDOW, A_KV_HEADS, A_HEAD_DIM))
        vs_l.append(vc_s.reshape(DEC_BATCH, WINDOW, A_KV_HEADS, A_HEAD_DIM))
        ss_l.append(st_s)
        cs_l.append(vn_s.reshape(DEC_BATCH, DEC_SEQ, C_GROUPS, C_GROUP_DIM))

    y_prompt = x[:M_PROMPT].reshape(BATCH, SEQ, D_MODEL)
    y_sample = x[M_PROMPT:].reshape(DEC_BATCH, DEC_SEQ, D_MODEL)
    return (y_prompt, y_sample, jnp.stack(kp_l), jnp.stack(vp_l), jnp.stack(sp_l),
            jnp.stack(ks_l), jnp.stack(vs_l), jnp.stack(ss_l), jnp.stack(cs_l))
```

```python
import numpy as np
import jax
import jax.numpy as jnp
from jax import lax
from jax.experimental import pallas as pl
from jax.experimental.pallas import tpu as pltpu

D_MODEL = 2048
BATCH = 4
SEQ = 2048
DEPTH = 4
DEC_BATCH = 32
DEC_SEQ = 4
PAST_LEN = 16384

A_HEADS = 16
A_KV_HEADS = 4
A_HEAD_DIM = 64
A_GROUP = A_HEADS // A_KV_HEADS
WINDOW = 128
ROT_DIM = A_HEAD_DIM // 4
ROT_HALF = ROT_DIM // 2
ROPE_THETA = 500000.0
B_HEADS = 8
B_KEY_DIM = 128
B_VAL_DIM = 128
B_CHUNK = 16
C_GROUPS = 8
C_GROUP_DIM = 128
C_CHUNK = 128
A_Q_W = A_HEADS * A_HEAD_DIM
A_KV_W = A_KV_HEADS * A_HEAD_DIM
B_W = B_HEADS * B_KEY_DIM
C_W = C_GROUPS * C_GROUP_DIM
FFN_DIM = ((8 * D_MODEL + 3 * 256 - 1) // (3 * 256)) * 256
EPS = 1e-6

M_PROMPT = BATCH * SEQ
M_SAMPLE = DEC_BATCH * DEC_SEQ
M_ALL = M_PROMPT + M_SAMPLE

OFF_A = 0
W_A = A_Q_W + 2 * A_KV_W
OFF_B = OFF_A + W_A
W_B = 4 * B_W
OFF_C = OFF_B + W_B
W_C = 2 * C_W
OFF_G = OFF_C + W_C
W_G = 3 * D_MODEL

LANES = 128
BLK = 128
TM = M_ALL // 4
TM_DOWN = M_ALL // 8
TM_NORM = M_ALL // 16
TM_FULL = M_ALL // 16
TN = 512
TN_NARROW = 256
ATTN_SAMPLE_GROUP = 4
HGRN_SAMPLE_GROUP = 8
VMEM_LIMIT = 56 * 1024 * 1024

_BF16 = jnp.bfloat16
_F32 = jnp.float32
_NT = (((1,), (1,)), ((), ()))
_TN = (((0,), (0,)), ((), ()))


def _params(n_grid):
    return pltpu.CompilerParams(dimension_semantics=("arbitrary",) * n_grid,
                                vmem_limit_bytes=VMEM_LIMIT)


def _sigmoid(x):
    return 0.5 * jnp.tanh(0.5 * x) + 0.5


def _layer_vec(width, layer):
    return pl.BlockSpec((None, 1, width), lambda *_: (layer, 0, 0))


def _rmsnorm_kernel(x_ref, g_ref, o_ref):
    x = x_ref[...]
    y = x * lax.rsqrt(jnp.mean(x * x, axis=-1, keepdims=True) + EPS)
    o_ref[...] = (y * g_ref[...]).astype(_BF16)


def _rmsnorm(x, g, layer):
    m, d = x.shape
    return pl.pallas_call(
        _rmsnorm_kernel,
        grid=(m // TM_NORM,),
        in_specs=[pl.BlockSpec((TM_NORM, d), lambda i: (i, 0)), _layer_vec(d, layer)],
        out_specs=pl.BlockSpec((TM_NORM, d), lambda i: (i, 0)),
        out_shape=jax.ShapeDtypeStruct((m, d), _BF16),
        compiler_params=_params(1),
        name="rmsnorm",
    )(x, g)


def _join_norm_kernel(xp_ref, xs_ref, g_ref, x_ref, h_ref):
    is_sample = pl.program_id(0) == pl.num_programs(0) - 1
    x = jnp.where(is_sample, xs_ref[...], xp_ref[...])
    x_ref[...] = x
    y = x * lax.rsqrt(jnp.mean(x * x, axis=-1, keepdims=True) + EPS)
    h_ref[...] = (y * g_ref[...]).astype(_BF16)


def _join_norm(x_prompt, x_sample, g, layer):
    d = x_prompt.shape[1]
    nblk = M_ALL // BLK
    rows = pl.BlockSpec((BLK, d), lambda i: (i, 0))
    return pl.pallas_call(
        _join_norm_kernel,
        grid=(nblk,),
        in_specs=[pl.BlockSpec((BLK, d), lambda i: (jnp.minimum(i, nblk - 2), 0)),
                  pl.BlockSpec((BLK, d), lambda i: (0, 0)), _layer_vec(d, layer)],
        out_specs=[rows, rows],
        out_shape=[jax.ShapeDtypeStruct((M_ALL, d), _F32), jax.ShapeDtypeStruct((M_ALL, d), _BF16)],
        compiler_params=_params(1),
        name="join_norm",
    )(x_prompt, x_sample, g)


def _mm_kernel(a_ref, w_ref, o_ref):
    o_ref[...] = jnp.dot(a_ref[...], w_ref[...].astype(_BF16), preferred_element_type=_F32)


def _matmul_cols(a, w, layer, col_off, n_cols):
    m, k = a.shape
    off = col_off // TN
    return pl.pallas_call(
        _mm_kernel,
        grid=(m // TM, n_cols // TN),
        in_specs=[pl.BlockSpec((TM, k), lambda i, j: (i, 0)),
                  pl.BlockSpec((None, k, TN), lambda i, j: (layer, 0, j + off))],
        out_specs=pl.BlockSpec((TM, TN), lambda i, j: (i, j)),
        out_shape=jax.ShapeDtypeStruct((m, n_cols), _F32),
        compiler_params=_params(2),
        name="proj_in",
    )(a, w)


def _mm_res_kernel(a_ref, w_ref, r_ref, o_ref):
    o_ref[...] = r_ref[...] + jnp.dot(a_ref[...], w_ref[...].astype(_BF16),
                                      preferred_element_type=_F32)


def _matmul_residual(a, w, layer, r, tm, tn):
    m, k = a.shape
    n = w.shape[2]
    return pl.pallas_call(
        _mm_res_kernel,
        grid=(m // tm, n // tn),
        in_specs=[pl.BlockSpec((tm, k), lambda i, j: (i, 0)),
                  pl.BlockSpec((None, k, tn), lambda i, j: (layer, 0, j)),
                  pl.BlockSpec((tm, tn), lambda i, j: (i, j))],
        out_specs=pl.BlockSpec((tm, tn), lambda i, j: (i, j)),
        out_shape=jax.ShapeDtypeStruct((m, n), _F32),
        compiler_params=_params(2),
        name="proj_residual",
    )(a, w, r)


def _mm_res_norm_kernel(a_ref, w_ref, r_ref, g_ref, o_ref, h_ref):
    x = r_ref[...] + jnp.dot(a_ref[...], w_ref[...], preferred_element_type=_F32)
    o_ref[...] = x
    y = x * lax.rsqrt(jnp.mean(x * x, axis=-1, keepdims=True) + EPS)
    h_ref[...] = (y * g_ref[...]).astype(_BF16)


def _matmul_residual_norm(a, w_bf16, layer, r, g):
    m, k = a.shape
    n = w_bf16.shape[2]
    rows = lambda width: pl.BlockSpec((TM_FULL, width), lambda i: (i, 0))
    return pl.pallas_call(
        _mm_res_norm_kernel,
        grid=(m // TM_FULL,),
        in_specs=[rows(k),
                  pl.BlockSpec((None, k, n), lambda i: (layer, 0, 0), pipeline_mode=pl.Buffered(1)),
                  rows(n), _layer_vec(n, layer)],
        out_specs=[rows(n), rows(n)],
        out_shape=[jax.ShapeDtypeStruct((m, n), _F32), jax.ShapeDtypeStruct((m, n), _BF16)],
        compiler_params=_params(1),
        name="proj_out_norm",
    )(a, w_bf16, r, g)


def _ffn_up_kernel(a_ref, wg_ref, wu_ref, o_ref):
    a = a_ref[...]
    g = jnp.dot(a, wg_ref[...].astype(_BF16), preferred_element_type=_F32)
    u = jnp.dot(a, wu_ref[...].astype(_BF16), preferred_element_type=_F32)
    o_ref[...] = (g * _sigmoid(g) * u).astype(_BF16)


def _ffn_up(a, w_up, layer):
    m, k = a.shape
    tn = TN_NARROW
    nj = FFN_DIM // tn
    return pl.pallas_call(
        _ffn_up_kernel,
        grid=(m // TM, nj),
        in_specs=[pl.BlockSpec((TM, k), lambda i, j: (i, 0)),
                  pl.BlockSpec((None, k, tn), lambda i, j: (layer, 0, j)),
                  pl.BlockSpec((None, k, tn), lambda i, j: (layer, 0, j + nj))],
        out_specs=pl.BlockSpec((TM, tn), lambda i, j: (i, j)),
        out_shape=jax.ShapeDtypeStruct((m, FFN_DIM), _BF16),
        compiler_params=_params(2),
        name="ffn_up",
    )(a, w_up, w_up)


def _merge_kernel(oa_ref, ob_ref, oc_ref, wa_ref, wb_ref, wc_ref, ga_ref, gb_ref, gc_ref, o_ref):
    ya = jnp.dot(oa_ref[...], wa_ref[...].astype(_BF16), preferred_element_type=_F32)
    yb = jnp.dot(ob_ref[...], wb_ref[...].astype(_BF16), preferred_element_type=_F32)
    yc = jnp.dot(oc_ref[...], wc_ref[...].astype(_BF16), preferred_element_type=_F32)
    merged = _sigmoid(ga_ref[...]) * ya + _sigmoid(gb_ref[...]) * yb + _sigmoid(gc_ref[...]) * yc
    o_ref[...] = merged.astype(_BF16)


def _merge(oa, ob, oc, gates, wa, wb, wc, layer):
    m, k = oa.shape
    tn = TN_NARROW
    nj = D_MODEL // tn
    branch = pl.BlockSpec((TM, k), lambda i, j: (i, 0))
    weight = pl.BlockSpec((None, k, tn), lambda i, j: (layer, 0, j))
    return pl.pallas_call(
        _merge_kernel,
        grid=(m // TM, nj),
        in_specs=[branch, branch, branch, weight, weight, weight,
                  pl.BlockSpec((TM, tn), lambda i, j: (i, j)),
                  pl.BlockSpec((TM, tn), lambda i, j: (i, j + nj)),
                  pl.BlockSpec((TM, tn), lambda i, j: (i, j + 2 * nj))],
        out_specs=pl.BlockSpec((TM, tn), lambda i, j: (i, j)),
        out_shape=jax.ShapeDtypeStruct((m, D_MODEL), _BF16),
        compiler_params=_params(2),
        name="merge",
    )(oa, ob, oc, wa, wb, wc, gates, gates, gates)


def _rope_tables(p0, rows):
    pos = (p0 + jnp.arange(rows, dtype=jnp.int32)).astype(_F32)
    inv_freq = jnp.power(jnp.float32(ROPE_THETA), -jnp.arange(ROT_HALF, dtype=_F32) / ROT_HALF)
    ang = pos[:, None] * inv_freq[None, :]
    cos, sin = jnp.cos(ang), jnp.sin(ang)
    rest = A_HEAD_DIM - ROT_DIM
    zeros = jnp.zeros((rows, ROT_HALF), _F32)
    pad = jnp.zeros((rows, rest), _F32)
    c = jnp.concatenate([cos, cos, jnp.ones((rows, rest), _F32)], axis=1)
    s1 = jnp.concatenate([-sin, zeros, pad], axis=1)
    s2 = jnp.concatenate([zeros, sin, pad], axis=1)
    return tuple(jnp.tile(t, (1, LANES // A_HEAD_DIM)) for t in (c, s1, s2))


def _rope(x, c, s1, s2):
    width = x.shape[1]
    reps = width // c.shape[1]
    c, s1, s2 = (jnp.concatenate([t] * reps, axis=1) for t in (c, s1, s2))
    ahead = pltpu.roll(x, width - ROT_HALF, axis=1)
    behind = pltpu.roll(x, ROT_HALF, axis=1)
    return x * c + ahead * s1 + behind * s2


def _head_mean_matrix():
    i = np.arange(LANES)
    same = (i[:, None] // A_HEAD_DIM) == (i[None, :] // A_HEAD_DIM)
    return jnp.asarray(same.astype(np.float32) / A_HEAD_DIM, dtype=_BF16)


def _qk_prep(x, gain, head_mean, c, s1, s2, scale):
    sq = x * x
    hi = sq.astype(_BF16)
    lo = (sq - hi.astype(_F32)).astype(_BF16)
    ms = jnp.concatenate(
        [jnp.dot(hi[:, l:l + LANES], head_mean, preferred_element_type=_F32)
         + jnp.dot(lo[:, l:l + LANES], head_mean, preferred_element_type=_F32)
         for l in range(0, x.shape[1], LANES)], axis=1)
    return _rope(x * gain, c, s1, s2) * (lax.rsqrt(ms + EPS) * scale)


def _low_half(shape):
    return lax.broadcasted_iota(jnp.int32, shape, 1) < A_HEAD_DIM


def _both_halves(col, half):
    low = _low_half(col.shape)
    sel = jnp.where(low if half == 0 else jnp.logical_not(low), col, 0.0)
    return sel + pltpu.roll(sel, A_HEAD_DIM, axis=1)


def _kv_pairs(k, v, h):
    pair, half = divmod(h, 2)
    lanes = slice(pair * LANES, (pair + 1) * LANES)
    return _both_halves(k[:, lanes], half), _both_halves(v[:, lanes], half)


def _mxu_tiles(k_pair, v_pair):
    v_pair = v_pair.astype(_BF16)
    return k_pair.astype(_BF16), jnp.concatenate([v_pair, jnp.ones(v_pair.shape, _BF16)], axis=1)


def _kv_tiles(k, v, h):
    return _mxu_tiles(*_kv_pairs(k, v, h))


def _query_rows(qn, h):
    low = _low_half((qn.shape[0], LANES))
    parts = []
    for g in range(A_GROUP):
        pair, half = divmod(h * A_GROUP + g, 2)
        keep = low if half == 0 else jnp.logical_not(low)
        parts.append(jnp.where(keep, qn[:, pair * LANES:(pair + 1) * LANES], 0.0))
    return jnp.concatenate(parts, axis=0).astype(_BF16)


def _sink_rows(sinks_ref, h, t):
    return jnp.concatenate([jnp.full((t, LANES), sinks_ref[0, h * A_GROUP + g], _F32)
                            for g in range(A_GROUP)], axis=0)


def _store_heads(o_ref, o, h, t):
    low = _low_half((t, LANES))
    for j in range(A_GROUP // 2):
        even = o[(2 * j) * t:(2 * j + 1) * t]
        odd = o[(2 * j + 1) * t:(2 * j + 2) * t]
        pair = (h * A_GROUP) // 2 + j
        o_ref[:, pair * LANES:(pair + 1) * LANES] = jnp.where(low, even, odd).astype(o_ref.dtype)


def _attn_prompt_kernel(sinks_ref, q_ref, kv_ref, c_ref, s1_ref, s2_ref, gq_ref, gk_ref, hm_ref,
                        kc_all_ref, vc_all_ref, o_ref, kc_ref, vc_ref, kprev, vprev):
    b = pl.program_id(0)
    n = pl.program_id(1)
    nb = pl.num_programs(1) - 1

    @pl.when(n == 0)
    def _():
        kprev[...] = jnp.zeros(kprev.shape, _F32)
        vprev[...] = jnp.zeros(vprev.shape, _F32)

    @pl.when(n < nb)
    def _():
        kv = kv_ref[...]
        v = kv[:, A_KV_W:]
        c, s1, s2 = c_ref[...], s1_ref[...], s2_ref[...]
        qn = _qk_prep(q_ref[...], gq_ref[...], hm_ref[...], c, s1, s2, A_HEAD_DIM ** -0.5)
        kn = _qk_prep(kv[:, :A_KV_W], gk_ref[...], hm_ref[...], c, s1, s2, 1.0)
        kc_ref[0] = kn
        vc_ref[0] = v

        rows = A_GROUP * BLK
        row = lax.broadcasted_iota(jnp.int32, (rows, WINDOW), 0) % BLK
        col = lax.broadcasted_iota(jnp.int32, (rows, WINDOW), 1)
        before = col > row
        has_prev = n > 0
        for h in range(A_KV_HEADS):
            k_pair, v_pair = _kv_pairs(kn, v, h)
            k_tile, v_tile = _mxu_tiles(k_pair, v_pair)
            k_before, v_before = _mxu_tiles(kprev[h], vprev[h])
            kprev[h] = k_pair
            vprev[h] = v_pair
            q4 = _query_rows(qn, h)
            s_prev = lax.dot_general(q4, k_before, _NT, preferred_element_type=_F32)
            s_cur = lax.dot_general(q4, k_tile, _NT, preferred_element_type=_F32)
            s = jnp.where(before, jnp.where(has_prev, s_prev, -jnp.inf), s_cur)
            sink = _sink_rows(sinks_ref, h, BLK)
            m = jnp.maximum(jnp.broadcast_to(jnp.max(s, axis=-1, keepdims=True), s.shape), sink)
            p = jnp.exp(s - m)
            acc = (jnp.dot(jnp.where(before, p, 0.0).astype(_BF16), v_before, preferred_element_type=_F32)
                   + jnp.dot(jnp.where(before, 0.0, p).astype(_BF16), v_tile, preferred_element_type=_F32))
            o = acc[:, :LANES] / (acc[:, LANES:] + jnp.exp(sink - m))
            _store_heads(o_ref, o, h, BLK)

    @pl.when((n == nb) & (b == pl.num_programs(0) - 1))
    def _():
        o_ref[...] = jnp.zeros(o_ref.shape, o_ref.dtype)


def _attn_prompt(a, sinks, gq, gk, tables, kc_all, vc_all, layer):
    nb = SEQ // BLK
    c, s1, s2 = tables
    block = lambda b, n: b * nb + jnp.minimum(n, nb - 1)
    table = pl.BlockSpec((BLK, LANES), lambda b, n: (jnp.minimum(n, nb - 1), 0))
    cache = pl.BlockSpec((None, 1, BLK, A_KV_W), lambda b, n: (layer, b, 0, 0))
    stacked = jax.ShapeDtypeStruct((DEPTH, BATCH, WINDOW, A_KV_W), _F32)
    const = lambda w: pl.BlockSpec((w, w), lambda b, n: (0, 0))

    def out_block(b, n):
        tail = (n == nb) & (b == BATCH - 1)
        return (jnp.where(tail, BATCH * nb, block(b, n)), 0)

    return pl.pallas_call(
        _attn_prompt_kernel,
        grid=(BATCH, nb + 1),
        in_specs=[pl.BlockSpec((None, 1, A_HEADS), lambda b, n: (layer, 0, 0), memory_space=pltpu.SMEM),
                  pl.BlockSpec((BLK, A_Q_W), lambda b, n: (block(b, n), 0)),
                  pl.BlockSpec((BLK, 2 * A_KV_W), lambda b, n: (block(b, n), A_Q_W // (2 * A_KV_W))),
                  table, table, table,
                  _layer_vec(A_Q_W, layer), _layer_vec(A_KV_W, layer), const(LANES),
                  pl.BlockSpec(memory_space=pl.ANY), pl.BlockSpec(memory_space=pl.ANY)],
        out_specs=[pl.BlockSpec((BLK, A_Q_W), out_block), cache, cache],
        out_shape=[jax.ShapeDtypeStruct((M_ALL, A_Q_W), _BF16), stacked, stacked],
        input_output_aliases={9: 1, 10: 2},
        scratch_shapes=[pltpu.VMEM((A_KV_HEADS, BLK, LANES), _F32),
                        pltpu.VMEM((A_KV_HEADS, BLK, LANES), _F32)],
        compiler_params=_params(2),
        name="attn_prompt",
    )(sinks, a, a, c, s1, s2, gq, gk, _head_mean_matrix(), kc_all, vc_all)


def _attn_sample_kernel(sinks_ref, qkv_ref, ck_ref, cv_ref, c_ref, s1_ref, s2_ref, gq_ref, gk_ref,
                        hm_ref, o_full_ref, kc_all_ref, vc_all_ref, o_ref, kc_ref, vc_ref,
                        kbuf, vbuf, knew, vnew):
    del o_full_ref, kc_all_ref, vc_all_ref
    grp = ATTN_SAMPLE_GROUP
    t = grp * DEC_SEQ
    qkv = qkv_ref[...]
    v = qkv[:, A_Q_W + A_KV_W:]
    c, s1, s2 = c_ref[...], s1_ref[...], s2_ref[...]
    qn = _qk_prep(qkv[:, :A_Q_W], gq_ref[...], hm_ref[...], c, s1, s2, A_HEAD_DIM ** -0.5)
    kn = _qk_prep(qkv[:, A_Q_W:A_Q_W + A_KV_W], gk_ref[...], hm_ref[...], c, s1, s2, 1.0)

    knew[...] = jnp.zeros(knew.shape, _F32)
    vnew[...] = jnp.zeros(vnew.shape, _F32)
    knew[0:t, :] = kn
    vnew[0:t, :] = v
    for s in range(grp):
        kbuf[s, 0:WINDOW, :] = ck_ref[s]
        vbuf[s, 0:WINDOW, :] = cv_ref[s]
        kbuf[s, WINDOW:, :] = knew[s * DEC_SEQ:s * DEC_SEQ + 8, :]
        vbuf[s, WINDOW:, :] = vnew[s * DEC_SEQ:s * DEC_SEQ + 8, :]
        kc_ref[s] = kbuf[s, DEC_SEQ:DEC_SEQ + WINDOW, :]
        vc_ref[s] = vbuf[s, DEC_SEQ:DEC_SEQ + WINDOW, :]

    rows = A_GROUP * t
    r_old = lax.broadcasted_iota(jnp.int32, (rows, grp * WINDOW), 0) % t
    c_old = lax.broadcasted_iota(jnp.int32, (rows, grp * WINDOW), 1)
    see_old = (c_old // WINDOW == r_old // DEC_SEQ) & (c_old % WINDOW > r_old % DEC_SEQ)
    r_new = lax.broadcasted_iota(jnp.int32, (rows, WINDOW), 0) % t
    c_new = lax.broadcasted_iota(jnp.int32, (rows, WINDOW), 1)
    see_new = (c_new < t) & (c_new // DEC_SEQ == r_new // DEC_SEQ) & (c_new % DEC_SEQ <= r_new % DEC_SEQ)
    pad_k = jnp.zeros((WINDOW - t, LANES), _BF16)
    pad_v = jnp.zeros((WINDOW - t, 2 * LANES), _BF16)
    for h in range(A_KV_HEADS):
        old = [_kv_tiles(ck_ref[s], cv_ref[s], h) for s in range(grp)]
        k_old = jnp.concatenate([kt for kt, _ in old], axis=0)
        v_old = jnp.concatenate([vt for _, vt in old], axis=0)
        k_new, v_new = _kv_tiles(kn, v, h)
        k_new = jnp.concatenate([k_new, pad_k], axis=0)
        v_new = jnp.concatenate([v_new, pad_v], axis=0)
        q4 = _query_rows(qn, h)
        s_old = jnp.where(see_old, lax.dot_general(q4, k_old, _NT, preferred_element_type=_F32), -jnp.inf)
        s_new = jnp.where(see_new, lax.dot_general(q4, k_new, _NT, preferred_element_type=_F32), -jnp.inf)
        sink = _sink_rows(sinks_ref, h, t)
        top = jnp.maximum(jnp.max(s_old, axis=-1, keepdims=True), jnp.max(s_new, axis=-1, keepdims=True))
        m = jnp.maximum(jnp.broadcast_to(top, sink.shape), sink)
        p_old = jnp.exp(s_old - jnp.concatenate([m] * grp, axis=1))
        p_new = jnp.exp(s_new - m)
        acc = (jnp.dot(p_old.astype(_BF16), v_old, preferred_element_type=_F32)
               + jnp.dot(p_new.astype(_BF16), v_new, preferred_element_type=_F32))
        o = acc[:, :LANES] / (acc[:, LANES:] + jnp.exp(sink - m))
        _store_heads(o_ref, o, h, t)


def _attn_sample(a, o_full, cache_k, cache_v, sinks, gq, gk, tables, kc_all, vc_all, layer):
    c, s1, s2 = tables
    grp = ATTN_SAMPLE_GROUP
    t = grp * DEC_SEQ
    first = M_PROMPT // t
    table = pl.BlockSpec((t, LANES), lambda b: (0, 0))
    cache_in = pl.BlockSpec((None, grp, WINDOW, A_KV_W), lambda b: (layer, b, 0, 0))
    cache_out = pl.BlockSpec((None, grp, WINDOW, A_KV_W), lambda b: (layer, b, 0, 0))
    stacked = jax.ShapeDtypeStruct((DEPTH, DEC_BATCH, WINDOW, A_KV_W), _F32)
    any_space = pl.BlockSpec(memory_space=pl.ANY)
    const = lambda w: pl.BlockSpec((w, w), lambda b: (0, 0))
    return pl.pallas_call(
        _attn_sample_kernel,
        grid=(DEC_BATCH // grp,),
        in_specs=[pl.BlockSpec((None, 1, A_HEADS), lambda b: (layer, 0, 0), memory_space=pltpu.SMEM),
                  pl.BlockSpec((t, W_A), lambda b: (first + b, 0)),
                  cache_in, cache_in, table, table, table,
                  _layer_vec(A_Q_W, layer), _layer_vec(A_KV_W, layer), const(LANES),
                  any_space, any_space, any_space],
        out_specs=[pl.BlockSpec((t, A_Q_W), lambda b: (first + b, 0)), cache_out, cache_out],
        out_shape=[jax.ShapeDtypeStruct((M_ALL, A_Q_W), _BF16), stacked, stacked],
        scratch_shapes=[pltpu.VMEM((grp, WINDOW + 8, A_KV_W), _F32),
                        pltpu.VMEM((grp, WINDOW + 8, A_KV_W), _F32),
                        pltpu.VMEM((t + 8, A_KV_W), _F32), pltpu.VMEM((t + 8, A_KV_W), _F32)],
        input_output_aliases={10: 0, 11: 1, 12: 2},
        compiler_params=_params(1),
        name="attn_sample",
    )(sinks, a, cache_k, cache_v, c, s1, s2, gq, gk, _head_mean_matrix(),
      o_full, kc_all, vc_all)


def _lower_bound_kernel(logits_ref, loglb_ref, log1m_ref):
    x = logits_ref[...]
    e = jnp.exp(x - jnp.max(x, axis=0, keepdims=True))
    sm = e / jnp.sum(e, axis=0, keepdims=True)
    acc = sm[0:1]
    rows = [acc]
    for l in range(1, DEPTH):
        acc = acc + sm[l:l + 1]
        rows.append(acc)
    for l in range(DEPTH):
        lb = rows[l] - rows[0]
        loglb_ref[l:l + 1, :] = jnp.log(lb)
        log1m_ref[l:l + 1, :] = jnp.log1p(-lb)


def _lower_bounds(lb_logits):
    shape = jax.ShapeDtypeStruct(lb_logits.shape, _F32)
    return pl.pallas_call(_lower_bound_kernel, out_shape=[shape, shape], name="hgrn_lower_bounds")(lb_logits)


def _chunk_matrices(rows, chunk):
    t = np.arange(rows)[:, None]
    s = np.arange(rows)[None, :]
    same = (t // chunk) == (s // chunk)
    tri = (same & (s <= t)).astype(np.float32)
    ref = (same & ((s % chunk) <= chunk // 2)).astype(np.float32)
    last = same.astype(np.float32)
    return jnp.asarray(np.concatenate([tri, tri - ref, last - tri], axis=0), dtype=_BF16)


def _split2(x):
    hi = x.astype(_BF16)
    return hi, (x - hi.astype(_F32)).astype(_BF16)


def _hgrn_gates(q, z, loglb, log1m, lt, rows):
    log_sig = jnp.minimum(z, 0.0) - jnp.log(1.0 + jnp.exp(-jnp.abs(z)))
    b = log1m + log_sig
    log_f = jnp.maximum(loglb, b) + jnp.log(1.0 + jnp.exp(-jnp.abs(loglb - b)))
    kk = -jnp.tanh(0.5 * log_f) * (jnp.exp(log_f) + 1.0)
    hi, lo = _split2(log_f)
    cums = jnp.dot(lt, hi, preferred_element_type=_F32) + jnp.dot(lt, lo, preferred_element_type=_F32)
    cum = cums[0:rows]
    cum_ref = cums[rows:2 * rows]
    cum_end = cums[2 * rows:3 * rows]
    e_cum = jnp.exp(cum)
    q_intra = q * jnp.exp(cum_ref)
    k_intra = kk * jnp.exp(-cum_ref)
    q_inter = q * e_cum
    k_state = kk * jnp.exp(cum_end)
    return q_intra, k_intra, q_inter, k_state, e_cum


def _hgrn_finish(o, g, w):
    y = o * lax.rsqrt(jnp.mean(o * o, axis=-1, keepdims=True) + EPS) * w
    return (y * (g * _sigmoid(g))).astype(_BF16)


def _hgrn_intra(q_intra, k_intra, v, causal):
    att = lax.dot_general(q_intra.astype(_BF16), k_intra.astype(_BF16), _NT, preferred_element_type=_F32)
    att = jnp.where(causal, att, 0.0)
    return jnp.dot(att.astype(_BF16), v, preferred_element_type=_F32)


def _hgrn_prompt_kernel(q_ref, f_ref, i_ref, g_ref, loglb_ref, log1m_ref, lt_ref, w_ref, s_all_ref,
                        o_ref, s_ref, st, hand):
    b = pl.program_id(0)
    n = pl.program_id(1)
    nb = pl.num_programs(1) - 2

    @pl.when(n == 0)
    def _():
        st[...] = jnp.zeros(st.shape, _F32)
        hand[...] = jnp.zeros(hand.shape, _F32)

    def step(slot, done):
        row = lax.broadcasted_iota(jnp.int32, (BLK, BLK), 0)
        col = lax.broadcasted_iota(jnp.int32, (BLK, BLK), 1)
        causal = (row // B_CHUNK == col // B_CHUNK) & (col <= row)
        chunks = [slice(c * B_CHUNK, (c + 1) * B_CHUNK) for c in range(BLK // B_CHUNK)]
        w = w_ref[...]
        lt = lt_ref[...]
        heads = [slice(h * B_KEY_DIM, (h + 1) * B_KEY_DIM) for h in range(B_HEADS)]
        intra, updates = [], []
        for lanes in heads:
            for j, t in enumerate(_hgrn_gates(q_ref[:, lanes], f_ref[:, lanes], loglb_ref[:, lanes],
                                              log1m_ref[:, lanes], lt, BLK)):
                hand[slot, j, :, lanes] = t
            v = i_ref[:, lanes].astype(_BF16)
            ks = hand[done, 3, :, lanes].astype(_BF16)
            intra.append(_hgrn_intra(hand[done, 0, :, lanes], hand[done, 1, :, lanes], v, causal))
            updates.append([lax.dot_general(v[rows], ks[rows], _TN, preferred_element_type=_F32)
                            for rows in chunks])
        before = []
        for h, lanes in enumerate(heads):
            state_t = st[h]
            seen = []
            for rows, update in zip(chunks, updates[h]):
                seen.append(state_t.astype(_BF16))
                state_t = state_t * hand[done, 4, rows.stop - 1:rows.stop, lanes] + update
            st[h] = state_t
            before.append(seen)
        for h, lanes in enumerate(heads):
            qi = hand[done, 2, :, lanes].astype(_BF16)
            o_inter = [lax.dot_general(qi[rows], s_t, _NT, preferred_element_type=_F32)
                       for rows, s_t in zip(chunks, before[h])]
            o_ref[:, lanes] = _hgrn_finish(intra[h] + jnp.concatenate(o_inter, axis=0), g_ref[:, lanes], w)

    for parity in range(2):
        pl.when((n <= nb) & (n % 2 == parity))(lambda parity=parity: step(parity, 1 - parity))

    @pl.when(n == nb)
    def _():
        for h in range(B_HEADS):
            s_ref[0, h] = st[h].T

    @pl.when((n == nb + 1) & (b == pl.num_programs(0) - 1))
    def _():
        o_ref[...] = jnp.zeros(o_ref.shape, o_ref.dtype)


def _hgrn_prompt(bm, loglb, log1m, w, s_all, layer):
    nb = SEQ // BLK

    def ahead(j):
        return pl.BlockSpec((BLK, B_W), lambda b, n: (b * nb + jnp.minimum(n, nb - 1), j))

    def behind(j):
        return pl.BlockSpec((BLK, B_W), lambda b, n: (b * nb + jnp.clip(n - 1, 0, nb - 1), j))

    def out_block(b, n):
        tail = (n == nb + 1) & (b == BATCH - 1)
        return (jnp.where(tail, BATCH * nb, b * nb + jnp.clip(n - 1, 0, nb - 1)), 0)

    return pl.pallas_call(
        _hgrn_prompt_kernel,
        grid=(BATCH, nb + 2),
        in_specs=[ahead(0), ahead(1), behind(2), behind(3), _layer_vec(B_W, layer), _layer_vec(B_W, layer),
                  pl.BlockSpec((3 * BLK, BLK), lambda b, n: (0, 0)),
                  _layer_vec(B_VAL_DIM, layer), pl.BlockSpec(memory_space=pl.ANY)],
        out_specs=[pl.BlockSpec((BLK, B_W), out_block),
                   pl.BlockSpec((None, 1, B_HEADS, B_KEY_DIM, B_VAL_DIM), lambda b, n: (layer, b, 0, 0, 0))],
        out_shape=[jax.ShapeDtypeStruct((M_ALL, B_W), _BF16),
                   jax.ShapeDtypeStruct((DEPTH, BATCH, B_HEADS, B_KEY_DIM, B_VAL_DIM), _F32)],
        scratch_shapes=[pltpu.VMEM((B_HEADS, B_VAL_DIM, B_KEY_DIM), _F32),
                        pltpu.VMEM((2, 5, BLK, B_W), _F32)],
        input_output_aliases={8: 1},
        compiler_params=_params(2),
        name="hgrn_prompt",
    )(bm, bm, bm, bm, loglb, log1m, _chunk_matrices(BLK, B_CHUNK), w, s_all)


def _hgrn_sample_kernel(q_ref, f_ref, i_ref, g_ref, loglb_ref, log1m_ref, lt_ref, w_ref, s0_ref,
                        o_full_ref, s_all_ref, o_ref, s_ref):
    del o_full_ref, s_all_ref
    rows = HGRN_SAMPLE_GROUP * DEC_SEQ

    def padded(ref):
        return jnp.concatenate([ref[...], jnp.zeros((BLK - rows, B_W), _F32)], axis=0)

    v_all = padded(i_ref)
    g_all = padded(g_ref)
    q_intra, k_intra, q_inter, k_state, e_cum = _hgrn_gates(
        padded(q_ref), padded(f_ref), loglb_ref[...], log1m_ref[...], lt_ref[...], BLK)
    row = lax.broadcasted_iota(jnp.int32, (BLK, BLK), 0)
    col = lax.broadcasted_iota(jnp.int32, (BLK, BLK), 1)
    causal = (row // DEC_SEQ == col // DEC_SEQ) & (col <= row)
    w = w_ref[...]
    for h in range(B_HEADS):
        lanes = slice(h * B_KEY_DIM, (h + 1) * B_KEY_DIM)
        v = v_all[:, lanes].astype(_BF16)
        o = _hgrn_intra(q_intra[:, lanes], k_intra[:, lanes], v, causal)
        qi = q_inter[:, lanes].astype(_BF16)
        ec_t = e_cum[:, lanes].T
        ks_t = k_state[:, lanes].T
        for s in range(HGRN_SAMPLE_GROUP):
            state = s0_ref[s, h]
            o_s = jnp.dot(qi, state.astype(_BF16), preferred_element_type=_F32)
            o = o + jnp.where(row // DEC_SEQ == s, o_s, 0.0)
            ks_seq = jnp.where(col // DEC_SEQ == s, ks_t, 0.0).astype(_BF16)
            update = jnp.dot(ks_seq, v, preferred_element_type=_F32)
            decay = ec_t[:, (s + 1) * DEC_SEQ - 1:(s + 1) * DEC_SEQ]
            s_ref[s, h] = state * decay + update
        o_ref[:, lanes] = _hgrn_finish(o, g_all[:, lanes], w)[0:rows]


def _hgrn_sample(bm, o_full, state, loglb, log1m, w, s_all, layer):
    rows = HGRN_SAMPLE_GROUP * DEC_SEQ
    first = M_PROMPT // rows
    gate = lambda j: pl.BlockSpec((rows, B_W), lambda s: (first + s, j))
    state_shape = (HGRN_SAMPLE_GROUP, B_HEADS, B_KEY_DIM, B_VAL_DIM)
    return pl.pallas_call(
        _hgrn_sample_kernel,
        grid=(DEC_BATCH // HGRN_SAMPLE_GROUP,),
        in_specs=[gate(0), gate(1), gate(2), gate(3), _layer_vec(B_W, layer), _layer_vec(B_W, layer),
                  pl.BlockSpec((3 * BLK, BLK), lambda s: (0, 0)),
                  _layer_vec(B_VAL_DIM, layer),
                  pl.BlockSpec((None,) + state_shape, lambda s: (layer, s, 0, 0, 0)),
                  pl.BlockSpec(memory_space=pl.ANY), pl.BlockSpec(memory_space=pl.ANY)],
        out_specs=[pl.BlockSpec((rows, B_W), lambda s: (first + s, 0)),
                   pl.BlockSpec((None,) + state_shape, lambda s: (layer, s, 0, 0, 0))],
        out_shape=[jax.ShapeDtypeStruct((M_ALL, B_W), _BF16),
                   jax.ShapeDtypeStruct(state.shape, _F32)],
        input_output_aliases={9: 0, 10: 1},
        compiler_params=_params(1),
        name="hgrn_sample",
    )(bm, bm, bm, bm, loglb, log1m, _chunk_matrices(BLK, DEC_SEQ), w, state, o_full, s_all)


def _sgu_kernel(u_ref, v_ref, w_ref, b_ref, g_ref, vn_all_ref, o_ref, vn_ref):
    is_sample = pl.program_id(0) == pl.num_programs(0) - 1
    row = lax.broadcasted_iota(jnp.int32, (BLK, BLK), 0)
    col = lax.broadcasted_iota(jnp.int32, (BLK, BLK), 1)
    same_seq = jnp.logical_or(jnp.logical_not(is_sample), row // DEC_SEQ == col // DEC_SEQ)
    causal = (col <= row) & same_seq
    gain = g_ref[...]
    bias = b_ref[...]
    for g in range(C_GROUPS):
        lanes = slice(g * C_GROUP_DIM, (g + 1) * C_GROUP_DIM)
        v = v_ref[:, lanes]
        vn = v * lax.rsqrt(jnp.mean(v * v, axis=-1, keepdims=True) + EPS) * gain
        w = jnp.where(causal, w_ref[g], 0.0).astype(_BF16)
        z = jnp.dot(w, vn.astype(_BF16), preferred_element_type=_F32) + bias[:, g:g + 1]
        o_ref[:, lanes] = (u_ref[:, lanes] * z).astype(_BF16)
        vn_ref[:, lanes] = vn


def _sgu(cm, w2, b2, gain, vn_all, layer):
    nblk = M_ALL // BLK
    which = lambda i: i // (nblk - 1)
    return pl.pallas_call(
        _sgu_kernel,
        grid=(nblk,),
        in_specs=[pl.BlockSpec((BLK, C_W), lambda i: (i, 0)),
                  pl.BlockSpec((BLK, C_W), lambda i: (i, 1)),
                  pl.BlockSpec((None, None, C_GROUPS, C_CHUNK, C_CHUNK), lambda i: (layer, which(i), 0, 0, 0)),
                  pl.BlockSpec((None, None, C_CHUNK, C_GROUPS), lambda i: (layer, which(i), 0, 0)),
                  _layer_vec(C_GROUP_DIM, layer), pl.BlockSpec(memory_space=pl.ANY)],
        out_specs=[pl.BlockSpec((BLK, C_W), lambda i: (i, 0)),
                   pl.BlockSpec((None, BLK, C_W), lambda i: (layer, 0, 0))],
        out_shape=[jax.ShapeDtypeStruct((M_ALL, C_W), _BF16),
                   jax.ShapeDtypeStruct((DEPTH, M_SAMPLE, C_W), _F32)],
        input_output_aliases={5: 1},
        compiler_params=_params(1),
        name="sgu",
    )(cm, cm, w2, b2, gain, vn_all)


def _sgu_params(w_spatial, b_spatial):
    reps = BLK // DEC_SEQ
    w_sample = jnp.tile(w_spatial[:, :, :DEC_SEQ, :DEC_SEQ], (1, 1, reps, reps))
    b_sample = jnp.tile(b_spatial[:, :, :DEC_SEQ], (1, 1, reps))
    w2 = jnp.stack([w_spatial, w_sample], axis=1)
    b2 = jnp.stack([jnp.swapaxes(b_spatial, 1, 2), jnp.swapaxes(b_sample, 1, 2)], axis=1)
    return w2, b2


def kernel(x_prompt, x_sample, cache_k, cache_v, state_hgrn, norm_mix, w_in, q_norm, k_norm, sinks,
           lb_logits, hgrn_out_norm, sgu_v_norm, w_spatial, b_spatial, w_branch_a, w_branch_b,
           w_branch_c, w_out, norm_ffn, w_ffn_up, w_ffn_down):
    loglb, log1m = _lower_bounds(lb_logits)
    loglb = loglb.reshape(DEPTH, 1, B_W)
    log1m = log1m.reshape(DEPTH, 1, B_W)
    tables_prompt = _rope_tables(0, SEQ)
    tables_sample = tuple(jnp.tile(t, (ATTN_SAMPLE_GROUP, 1)) for t in _rope_tables(PAST_LEN, DEC_SEQ))
    gq = jnp.tile(q_norm, (1, A_HEADS)).reshape(DEPTH, 1, A_Q_W)
    gk = jnp.tile(k_norm, (1, A_KV_HEADS)).reshape(DEPTH, 1, A_KV_W)
    w_hg = hgrn_out_norm.reshape(DEPTH, 1, B_VAL_DIM)
    w_sg = sgu_v_norm.reshape(DEPTH, 1, C_GROUP_DIM)
    g_mix = norm_mix.reshape(DEPTH, 1, D_MODEL)
    g_ffn = norm_ffn.reshape(DEPTH, 1, D_MODEL)
    sinks = sinks.reshape(DEPTH, 1, A_HEADS)
    w_out_bf16 = w_out.astype(_BF16)
    w2, b2 = _sgu_params(w_spatial, b_spatial)
    ck_all = cache_k.reshape(DEPTH, DEC_BATCH, WINDOW, A_KV_W)
    cv_all = cache_v.reshape(DEPTH, DEC_BATCH, WINDOW, A_KV_W)

    kc_p = jnp.zeros((DEPTH, BATCH, WINDOW, A_KV_W), _F32)
    vc_p = jnp.zeros((DEPTH, BATCH, WINDOW, A_KV_W), _F32)
    kc_s = jnp.zeros((DEPTH, DEC_BATCH, WINDOW, A_KV_W), _F32)
    vc_s = jnp.zeros((DEPTH, DEC_BATCH, WINDOW, A_KV_W), _F32)
    st_p = jnp.zeros((DEPTH, BATCH, B_HEADS, B_KEY_DIM, B_VAL_DIM), _F32)
    st_s = jnp.zeros(state_hgrn.shape, _F32)
    vn_s = jnp.zeros((DEPTH, M_SAMPLE, C_W), _F32)
    for l in range(DEPTH):
        if l == 0:
            x, h = _join_norm(x_prompt.reshape(M_PROMPT, D_MODEL), x_sample.reshape(M_SAMPLE, D_MODEL), g_mix, l)
        else:
            h = _rmsnorm(x, g_mix, l)
        am = _matmul_cols(h, w_in, l, OFF_A, W_A)
        bm = _matmul_cols(h, w_in, l, OFF_B, W_B)
        cm = _matmul_cols(h, w_in, l, OFF_C, W_C)
        gates = _matmul_cols(h, w_in, l, OFF_G, W_G)

        oa, kc_p, vc_p = _attn_prompt(am, sinks, gq, gk, tables_prompt, kc_p, vc_p, l)
        oa, kc_s, vc_s = _attn_sample(am, oa, ck_all, cv_all, sinks, gq, gk, tables_sample, kc_s, vc_s, l)
        ob, st_p = _hgrn_prompt(bm, loglb, log1m, w_hg, st_p, l)
        ob, st_s = _hgrn_sample(bm, ob, state_hgrn, loglb, log1m, w_hg, st_s, l)
        oc, vn_s = _sgu(cm, w2, b2, w_sg, vn_s, l)
        merged = _merge(oa, ob, oc, gates, w_branch_a, w_branch_b, w_branch_c, l)
        x, h2 = _matmul_residual_norm(merged, w_out_bf16, l, x, g_ffn)
        act = _ffn_up(h2, w_ffn_up, l)
        x = _matmul_residual(act, w_ffn_down, l, x, TM_DOWN, TN_NARROW)


    y_prompt = x[:M_PROMPT].reshape(BATCH, SEQ, D_MODEL)
    y_sample = x[M_PROMPT:].reshape(DEC_BATCH, DEC_SEQ, D_MODEL)
    heads_p = (DEPTH, BATCH, WINDOW, A_KV_HEADS, A_HEAD_DIM)
    heads_s = (DEPTH, DEC_BATCH, WINDOW, A_KV_HEADS, A_HEAD_DIM)
    return (y_prompt, y_sample, kc_p.reshape(heads_p), vc_p.reshape(heads_p), st_p,
            kc_s.reshape(heads_s), vc_s.reshape(heads_s), st_s,
            vn_s.reshape(DEPTH, DEC_BATCH, DEC_SEQ, C_GROUPS, C_GROUP_DIM))
```

```python
import numpy as np
import jax
import jax.numpy as jnp
from jax import lax
from jax.experimental import pallas as pl
from jax.experimental.pallas import tpu as pltpu

D_MODEL = 2048
BATCH = 4
SEQ = 2048
DEPTH = 4
DEC_BATCH = 32
DEC_SEQ = 4
PAST_LEN = 16384

A_HEADS = 16
A_KV_HEADS = 4
A_HEAD_DIM = 64
A_GROUP = A_HEADS // A_KV_HEADS
WINDOW = 128
ROT_DIM = A_HEAD_DIM // 4
ROT_HALF = ROT_DIM // 2
ROPE_THETA = 500000.0
B_HEADS = 8
B_KEY_DIM = 128
B_VAL_DIM = 128
B_CHUNK = 16
C_GROUPS = 8
C_GROUP_DIM = 128
C_CHUNK = 128
A_Q_W = A_HEADS * A_HEAD_DIM
A_KV_W = A_KV_HEADS * A_HEAD_DIM
B_W = B_HEADS * B_KEY_DIM
C_W = C_GROUPS * C_GROUP_DIM
FFN_DIM = ((8 * D_MODEL + 3 * 256 - 1) // (3 * 256)) * 256
EPS = 1e-6

M_PROMPT = BATCH * SEQ
M_SAMPLE = DEC_BATCH * DEC_SEQ
M_ALL = M_PROMPT + M_SAMPLE

OFF_A = 0
W_A = A_Q_W + 2 * A_KV_W
OFF_B = OFF_A + W_A
W_B = 4 * B_W
OFF_C = OFF_B + W_B
W_C = 2 * C_W
OFF_G = OFF_C + W_C
W_G = 3 * D_MODEL
N_IN = OFF_G + W_G

LANES = 128
BLK = 128
TM = M_ALL // 4
TM_IN = M_ALL // 2
TM_DOWN = M_ALL // 8
TM_NORM = M_ALL // 16
TM_FULL = M_ALL // 16
TN = 512
TN_NARROW = 256
ATTN_SAMPLE_GROUP = 4
HGRN_SAMPLE_GROUP = 8
VMEM_LIMIT = 56 * 1024 * 1024

_BF16 = jnp.bfloat16
_F32 = jnp.float32
_NT = (((1,), (1,)), ((), ()))
_TN = (((0,), (0,)), ((), ()))


def _params(n_grid):
    return pltpu.CompilerParams(dimension_semantics=("arbitrary",) * n_grid,
                                vmem_limit_bytes=VMEM_LIMIT)


def _sigmoid(x):
    return 0.5 * jnp.tanh(0.5 * x) + 0.5


def _layer_vec(width, layer):
    return pl.BlockSpec((None, 1, width), lambda *_: (layer, 0, 0))


def _rmsnorm_kernel(x_ref, g_ref, o_ref):
    x = x_ref[...]
    y = x * lax.rsqrt(jnp.mean(x * x, axis=-1, keepdims=True) + EPS)
    o_ref[...] = (y * g_ref[...]).astype(_BF16)


def _rmsnorm(x, g, layer):
    m, d = x.shape
    return pl.pallas_call(
        _rmsnorm_kernel,
        grid=(m // TM_NORM,),
        in_specs=[pl.BlockSpec((TM_NORM, d), lambda i: (i, 0)), _layer_vec(d, layer)],
        out_specs=pl.BlockSpec((TM_NORM, d), lambda i: (i, 0)),
        out_shape=jax.ShapeDtypeStruct((m, d), _BF16),
        compiler_params=_params(1),
        name="rmsnorm",
    )(x, g)


def _join_norm_kernel(xp_ref, xs_ref, g_ref, x_ref, h_ref):
    is_sample = pl.program_id(0) == pl.num_programs(0) - 1
    x = jnp.where(is_sample, xs_ref[...], xp_ref[...])
    x_ref[...] = x
    y = x * lax.rsqrt(jnp.mean(x * x, axis=-1, keepdims=True) + EPS)
    h_ref[...] = (y * g_ref[...]).astype(_BF16)


def _join_norm(x_prompt, x_sample, g, layer):
    d = x_prompt.shape[1]
    nblk = M_ALL // BLK
    rows = pl.BlockSpec((BLK, d), lambda i: (i, 0))
    return pl.pallas_call(
        _join_norm_kernel,
        grid=(nblk,),
        in_specs=[pl.BlockSpec((BLK, d), lambda i: (jnp.minimum(i, nblk - 2), 0)),
                  pl.BlockSpec((BLK, d), lambda i: (0, 0)), _layer_vec(d, layer)],
        out_specs=[rows, rows],
        out_shape=[jax.ShapeDtypeStruct((M_ALL, d), _F32), jax.ShapeDtypeStruct((M_ALL, d), _BF16)],
        compiler_params=_params(1),
        name="join_norm",
    )(x_prompt, x_sample, g)


def _log_sigmoid(z):
    return jnp.minimum(z, 0.0) - jnp.log(1.0 + jnp.exp(-jnp.abs(z)))


def _mm_kernel(a_ref, w_ref, o_ref):
    o_ref[...] = jnp.dot(a_ref[...], w_ref[...].astype(_BF16), preferred_element_type=_F32)


def _proj_in(a, w, layer):
    m, k = a.shape
    return pl.pallas_call(
        _mm_kernel,
        grid=(m // TM_IN, N_IN // TN),
        in_specs=[pl.BlockSpec((TM_IN, k), lambda i, j: (i, 0), pipeline_mode=pl.Buffered(1)),
                  pl.BlockSpec((None, k, TN), lambda i, j: (layer, 0, j))],
        out_specs=pl.BlockSpec((TM_IN, TN), lambda i, j: (i, j)),
        out_shape=jax.ShapeDtypeStruct((m, N_IN), _F32),
        compiler_params=_params(2),
        name="proj_in",
    )(a, w)


class _WideRef:
    def __init__(self, low, high):
        self.parts = (low, high)

    def __getitem__(self, idx):
        if idx is Ellipsis:
            return jnp.concatenate([p[...] for p in self.parts], axis=1)
        rows, cols = idx
        part, start = divmod(cols.start, TN)
        return self.parts[part][rows, start:start + cols.stop - cols.start]


def _mm_res_kernel(a_ref, w_ref, r_ref, o_ref):
    o_ref[...] = r_ref[...] + jnp.dot(a_ref[...], w_ref[...].astype(_BF16),
                                      preferred_element_type=_F32)


def _matmul_residual(a, w, layer, r, tm, tn):
    m, k = a.shape
    n = w.shape[2]
    return pl.pallas_call(
        _mm_res_kernel,
        grid=(m // tm, n // tn),
        in_specs=[pl.BlockSpec((tm, k), lambda i, j: (i, 0)),
                  pl.BlockSpec((None, k, tn), lambda i, j: (layer, 0, j)),
                  pl.BlockSpec((tm, tn), lambda i, j: (i, j))],
        out_specs=pl.BlockSpec((tm, tn), lambda i, j: (i, j)),
        out_shape=jax.ShapeDtypeStruct((m, n), _F32),
        compiler_params=_params(2),
        name="proj_residual",
    )(a, w, r)


def _mm_res_norm_kernel(a_ref, w_ref, r_ref, g_ref, o_ref, h_ref):
    x = r_ref[...] + jnp.dot(a_ref[...], w_ref[...], preferred_element_type=_F32)
    o_ref[...] = x
    y = x * lax.rsqrt(jnp.mean(x * x, axis=-1, keepdims=True) + EPS)
    h_ref[...] = (y * g_ref[...]).astype(_BF16)


def _matmul_residual_norm(a, w_bf16, layer, r, g):
    m, k = a.shape
    n = w_bf16.shape[2]
    rows = lambda width: pl.BlockSpec((TM_FULL, width), lambda i: (i, 0))
    return pl.pallas_call(
        _mm_res_norm_kernel,
        grid=(m // TM_FULL,),
        in_specs=[rows(k),
                  pl.BlockSpec((None, k, n), lambda i: (layer, 0, 0), pipeline_mode=pl.Buffered(1)),
                  rows(n), _layer_vec(n, layer)],
        out_specs=[rows(n), rows(n)],
        out_shape=[jax.ShapeDtypeStruct((m, n), _F32), jax.ShapeDtypeStruct((m, n), _BF16)],
        compiler_params=_params(1),
        name="proj_out_norm",
    )(a, w_bf16, r, g)


def _ffn_up_kernel(a_ref, wg_ref, wu_ref, o_ref):
    a = a_ref[...]
    g = jnp.dot(a, wg_ref[...].astype(_BF16), preferred_element_type=_F32)
    u = jnp.dot(a, wu_ref[...].astype(_BF16), preferred_element_type=_F32)
    o_ref[...] = (g * _sigmoid(g) * u).astype(_BF16)


def _ffn_up(a, w_up, layer):
    m, k = a.shape
    tn = TN_NARROW
    nj = FFN_DIM // tn
    return pl.pallas_call(
        _ffn_up_kernel,
        grid=(m // TM, nj),
        in_specs=[pl.BlockSpec((TM, k), lambda i, j: (i, 0)),
                  pl.BlockSpec((None, k, tn), lambda i, j: (layer, 0, j)),
                  pl.BlockSpec((None, k, tn), lambda i, j: (layer, 0, j + nj))],
        out_specs=pl.BlockSpec((TM, tn), lambda i, j: (i, j)),
        out_shape=jax.ShapeDtypeStruct((m, FFN_DIM), _BF16),
        compiler_params=_params(2),
        name="ffn_up",
    )(a, w_up, w_up)


def _merge_kernel(oa_ref, ob_ref, oc_ref, wa_ref, wb_ref, wc_ref, ga_ref, gb_ref, gc_ref, o_ref):
    ya = jnp.dot(oa_ref[...], wa_ref[...].astype(_BF16), preferred_element_type=_F32)
    yb = jnp.dot(ob_ref[...], wb_ref[...].astype(_BF16), preferred_element_type=_F32)
    yc = jnp.dot(oc_ref[...], wc_ref[...].astype(_BF16), preferred_element_type=_F32)
    merged = _sigmoid(ga_ref[...]) * ya + _sigmoid(gb_ref[...]) * yb + _sigmoid(gc_ref[...]) * yc
    o_ref[...] = merged.astype(_BF16)


def _merge(oa, ob, oc, proj, wa, wb, wc, layer):
    m, k = oa.shape
    tn = TN_NARROW
    nj = D_MODEL // tn
    branch = pl.BlockSpec((TM, k), lambda i, j: (i, 0))
    weight = pl.BlockSpec((None, k, tn), lambda i, j: (layer, 0, j))
    return pl.pallas_call(
        _merge_kernel,
        grid=(m // TM, nj),
        in_specs=[branch, branch, branch, weight, weight, weight,
                  *(pl.BlockSpec((TM, tn), lambda i, j, g=g: (i, OFF_G // tn + g * nj + j)) for g in range(3))],
        out_specs=pl.BlockSpec((TM, tn), lambda i, j: (i, j)),
        out_shape=jax.ShapeDtypeStruct((m, D_MODEL), _BF16),
        compiler_params=_params(2),
        name="merge",
    )(oa, ob, oc, wa, wb, wc, proj, proj, proj)


def _rope_tables(p0, rows):
    pos = (p0 + jnp.arange(rows, dtype=jnp.int32)).astype(_F32)
    inv_freq = jnp.power(jnp.float32(ROPE_THETA), -jnp.arange(ROT_HALF, dtype=_F32) / ROT_HALF)
    ang = pos[:, None] * inv_freq[None, :]
    cos, sin = jnp.cos(ang), jnp.sin(ang)
    rest = A_HEAD_DIM - ROT_DIM
    zeros = jnp.zeros((rows, ROT_HALF), _F32)
    pad = jnp.zeros((rows, rest), _F32)
    c = jnp.concatenate([cos, cos, jnp.ones((rows, rest), _F32)], axis=1)
    s1 = jnp.concatenate([-sin, zeros, pad], axis=1)
    s2 = jnp.concatenate([zeros, sin, pad], axis=1)
    return tuple(jnp.tile(t, (1, LANES // A_HEAD_DIM)) for t in (c, s1, s2))


def _rope(x, c, s1, s2):
    width = x.shape[1]
    reps = width // c.shape[1]
    c, s1, s2 = (jnp.concatenate([t] * reps, axis=1) for t in (c, s1, s2))
    ahead = pltpu.roll(x, width - ROT_HALF, axis=1)
    behind = pltpu.roll(x, ROT_HALF, axis=1)
    return x * c + ahead * s1 + behind * s2


def _head_mean_matrix():
    i = np.arange(LANES)
    same = (i[:, None] // A_HEAD_DIM) == (i[None, :] // A_HEAD_DIM)
    return jnp.asarray(same.astype(np.float32) / A_HEAD_DIM, dtype=_BF16)


def _qk_prep(x, gain, head_mean, c, s1, s2, scale):
    sq = x * x
    hi = sq.astype(_BF16)
    lo = (sq - hi.astype(_F32)).astype(_BF16)
    ms = jnp.concatenate(
        [jnp.dot(hi[:, l:l + LANES], head_mean, preferred_element_type=_F32)
         + jnp.dot(lo[:, l:l + LANES], head_mean, preferred_element_type=_F32)
         for l in range(0, x.shape[1], LANES)], axis=1)
    return _rope(x * gain, c, s1, s2) * (lax.rsqrt(ms + EPS) * scale)


def _low_half(shape):
    return lax.broadcasted_iota(jnp.int32, shape, 1) < A_HEAD_DIM


def _both_halves(col, half):
    low = _low_half(col.shape)
    sel = jnp.where(low if half == 0 else jnp.logical_not(low), col, 0.0)
    return sel + pltpu.roll(sel, A_HEAD_DIM, axis=1)


def _kv_pairs(k, v, h):
    pair, half = divmod(h, 2)
    lanes = slice(pair * LANES, (pair + 1) * LANES)
    return _both_halves(k[:, lanes], half), _both_halves(v[:, lanes], half)


def _mxu_tiles(k_pair, v_pair):
    v_pair = v_pair.astype(_BF16)
    return k_pair.astype(_BF16), jnp.concatenate([v_pair, jnp.ones(v_pair.shape, _BF16)], axis=1)


def _kv_tiles(k, v, h):
    return _mxu_tiles(*_kv_pairs(k, v, h))


def _query_rows(qn, h):
    low = _low_half((qn.shape[0], LANES))
    parts = []
    for g in range(A_GROUP):
        pair, half = divmod(h * A_GROUP + g, 2)
        keep = low if half == 0 else jnp.logical_not(low)
        parts.append(jnp.where(keep, qn[:, pair * LANES:(pair + 1) * LANES], 0.0))
    return jnp.concatenate(parts, axis=0).astype(_BF16)


def _sink_rows(sinks_ref, h, t):
    return jnp.concatenate([jnp.full((t, LANES), sinks_ref[0, h * A_GROUP + g], _F32)
                            for g in range(A_GROUP)], axis=0)


def _store_heads(o_ref, o, h, t):
    low = _low_half((t, LANES))
    for j in range(A_GROUP // 2):
        even = o[(2 * j) * t:(2 * j + 1) * t]
        odd = o[(2 * j + 1) * t:(2 * j + 2) * t]
        pair = (h * A_GROUP) // 2 + j
        o_ref[:, pair * LANES:(pair + 1) * LANES] = jnp.where(low, even, odd).astype(o_ref.dtype)


def _attn_prompt_kernel(sinks_ref, q_ref, kv_ref, c_ref, s1_ref, s2_ref, gq_ref, gk_ref, hm_ref,
                        kc_all_ref, vc_all_ref, o_ref, kc_ref, vc_ref, kprev, vprev):
    b = pl.program_id(0)
    n = pl.program_id(1)
    nb = pl.num_programs(1) - 1

    @pl.when(n == 0)
    def _():
        kprev[...] = jnp.zeros(kprev.shape, _F32)
        vprev[...] = jnp.zeros(vprev.shape, _F32)

    @pl.when(n < nb)
    def _():
        kv = kv_ref[...]
        v = kv[:, A_KV_W:]
        c, s1, s2 = c_ref[...], s1_ref[...], s2_ref[...]
        qn = _qk_prep(q_ref[...], gq_ref[...], hm_ref[...], c, s1, s2, A_HEAD_DIM ** -0.5)
        kn = _qk_prep(kv[:, :A_KV_W], gk_ref[...], hm_ref[...], c, s1, s2, 1.0)
        kc_ref[0] = kn
        vc_ref[0] = v

        rows = A_GROUP * BLK
        row = lax.broadcasted_iota(jnp.int32, (rows, WINDOW), 0) % BLK
        col = lax.broadcasted_iota(jnp.int32, (rows, WINDOW), 1)
        before = col > row
        has_prev = n > 0
        for h in range(A_KV_HEADS):
            k_pair, v_pair = _kv_pairs(kn, v, h)
            k_tile, v_tile = _mxu_tiles(k_pair, v_pair)
            k_before, v_before = _mxu_tiles(kprev[h], vprev[h])
            kprev[h] = k_pair
            vprev[h] = v_pair
            q4 = _query_rows(qn, h)
            s_prev = lax.dot_general(q4, k_before, _NT, preferred_element_type=_F32)
            s_cur = lax.dot_general(q4, k_tile, _NT, preferred_element_type=_F32)
            s = jnp.where(before, jnp.where(has_prev, s_prev, -jnp.inf), s_cur)
            sink = _sink_rows(sinks_ref, h, BLK)
            m = jnp.maximum(jnp.broadcast_to(jnp.max(s, axis=-1, keepdims=True), s.shape), sink)
            p = jnp.exp(s - m)
            acc = (jnp.dot(jnp.where(before, p, 0.0).astype(_BF16), v_before, preferred_element_type=_F32)
                   + jnp.dot(jnp.where(before, 0.0, p).astype(_BF16), v_tile, preferred_element_type=_F32))
            o = acc[:, :LANES] / (acc[:, LANES:] + jnp.exp(sink - m))
            _store_heads(o_ref, o, h, BLK)

    @pl.when((n == nb) & (b == pl.num_programs(0) - 1))
    def _():
        o_ref[...] = jnp.zeros(o_ref.shape, o_ref.dtype)


def _attn_prompt(a, sinks, gq, gk, tables, kc_all, vc_all, layer):
    nb = SEQ // BLK
    c, s1, s2 = tables
    block = lambda b, n: b * nb + jnp.minimum(n, nb - 1)
    table = pl.BlockSpec((BLK, LANES), lambda b, n: (jnp.minimum(n, nb - 1), 0))
    cache = pl.BlockSpec((None, 1, BLK, A_KV_W), lambda b, n: (layer, b, 0, 0))
    stacked = jax.ShapeDtypeStruct((DEPTH, BATCH, WINDOW, A_KV_W), _F32)
    const = lambda w: pl.BlockSpec((w, w), lambda b, n: (0, 0))

    def out_block(b, n):
        tail = (n == nb) & (b == BATCH - 1)
        return (jnp.where(tail, BATCH * nb, block(b, n)), 0)

    return pl.pallas_call(
        _attn_prompt_kernel,
        grid=(BATCH, nb + 1),
        in_specs=[pl.BlockSpec((None, 1, A_HEADS), lambda b, n: (layer, 0, 0), memory_space=pltpu.SMEM),
                  pl.BlockSpec((BLK, A_Q_W), lambda b, n: (block(b, n), 0)),
                  pl.BlockSpec((BLK, 2 * A_KV_W), lambda b, n: (block(b, n), A_Q_W // (2 * A_KV_W))),
                  table, table, table,
                  _layer_vec(A_Q_W, layer), _layer_vec(A_KV_W, layer), const(LANES),
                  pl.BlockSpec(memory_space=pl.ANY), pl.BlockSpec(memory_space=pl.ANY)],
        out_specs=[pl.BlockSpec((BLK, A_Q_W), out_block), cache, cache],
        out_shape=[jax.ShapeDtypeStruct((M_ALL, A_Q_W), _BF16), stacked, stacked],
        input_output_aliases={9: 1, 10: 2},
        scratch_shapes=[pltpu.VMEM((A_KV_HEADS, BLK, LANES), _F32),
                        pltpu.VMEM((A_KV_HEADS, BLK, LANES), _F32)],
        compiler_params=_params(2),
        name="attn_prompt",
    )(sinks, a, a, c, s1, s2, gq, gk, _head_mean_matrix(), kc_all, vc_all)


def _attn_sample_kernel(sinks_ref, qkv_ref, ck_ref, cv_ref, c_ref, s1_ref, s2_ref, gq_ref, gk_ref,
                        hm_ref, o_full_ref, kc_all_ref, vc_all_ref, o_ref, kc_ref, vc_ref,
                        kbuf, vbuf, knew, vnew):
    del o_full_ref, kc_all_ref, vc_all_ref
    grp = ATTN_SAMPLE_GROUP
    t = grp * DEC_SEQ
    qkv = qkv_ref[...]
    v = qkv[:, A_Q_W + A_KV_W:]
    c, s1, s2 = c_ref[...], s1_ref[...], s2_ref[...]
    qn = _qk_prep(qkv[:, :A_Q_W], gq_ref[...], hm_ref[...], c, s1, s2, A_HEAD_DIM ** -0.5)
    kn = _qk_prep(qkv[:, A_Q_W:A_Q_W + A_KV_W], gk_ref[...], hm_ref[...], c, s1, s2, 1.0)

    knew[...] = jnp.zeros(knew.shape, _F32)
    vnew[...] = jnp.zeros(vnew.shape, _F32)
    knew[0:t, :] = kn
    vnew[0:t, :] = v
    for s in range(grp):
        kbuf[s, 0:WINDOW, :] = ck_ref[s]
        vbuf[s, 0:WINDOW, :] = cv_ref[s]
        kbuf[s, WINDOW:, :] = knew[s * DEC_SEQ:s * DEC_SEQ + 8, :]
        vbuf[s, WINDOW:, :] = vnew[s * DEC_SEQ:s * DEC_SEQ + 8, :]
        kc_ref[s] = kbuf[s, DEC_SEQ:DEC_SEQ + WINDOW, :]
        vc_ref[s] = vbuf[s, DEC_SEQ:DEC_SEQ + WINDOW, :]

    rows = A_GROUP * t
    r_old = lax.broadcasted_iota(jnp.int32, (rows, grp * WINDOW), 0) % t
    c_old = lax.broadcasted_iota(jnp.int32, (rows, grp * WINDOW), 1)
    see_old = (c_old // WINDOW == r_old // DEC_SEQ) & (c_old % WINDOW > r_old % DEC_SEQ)
    r_new = lax.broadcasted_iota(jnp.int32, (rows, WINDOW), 0) % t
    c_new = lax.broadcasted_iota(jnp.int32, (rows, WINDOW), 1)
    see_new = (c_new < t) & (c_new // DEC_SEQ == r_new // DEC_SEQ) & (c_new % DEC_SEQ <= r_new % DEC_SEQ)
    pad_k = jnp.zeros((WINDOW - t, LANES), _BF16)
    pad_v = jnp.zeros((WINDOW - t, 2 * LANES), _BF16)
    for h in range(A_KV_HEADS):
        old = [_kv_tiles(ck_ref[s], cv_ref[s], h) for s in range(grp)]
        k_old = jnp.concatenate([kt for kt, _ in old], axis=0)
        v_old = jnp.concatenate([vt for _, vt in old], axis=0)
        k_new, v_new = _kv_tiles(kn, v, h)
        k_new = jnp.concatenate([k_new, pad_k], axis=0)
        v_new = jnp.concatenate([v_new, pad_v], axis=0)
        q4 = _query_rows(qn, h)
        s_old = jnp.where(see_old, lax.dot_general(q4, k_old, _NT, preferred_element_type=_F32), -jnp.inf)
        s_new = jnp.where(see_new, lax.dot_general(q4, k_new, _NT, preferred_element_type=_F32), -jnp.inf)
        sink = _sink_rows(sinks_ref, h, t)
        top = jnp.maximum(jnp.max(s_old, axis=-1, keepdims=True), jnp.max(s_new, axis=-1, keepdims=True))
        m = jnp.maximum(jnp.broadcast_to(top, sink.shape), sink)
        p_old = jnp.exp(s_old - jnp.concatenate([m] * grp, axis=1))
        p_new = jnp.exp(s_new - m)
        acc = (jnp.dot(p_old.astype(_BF16), v_old, preferred_element_type=_F32)
               + jnp.dot(p_new.astype(_BF16), v_new, preferred_element_type=_F32))
        o = acc[:, :LANES] / (acc[:, LANES:] + jnp.exp(sink - m))
        _store_heads(o_ref, o, h, t)


def _attn_sample(a, o_full, cache_k, cache_v, sinks, gq, gk, tables, kc_all, vc_all, layer):
    c, s1, s2 = tables
    grp = ATTN_SAMPLE_GROUP
    t = grp * DEC_SEQ
    first = M_PROMPT // t
    table = pl.BlockSpec((t, LANES), lambda b: (0, 0))
    cache_in = pl.BlockSpec((None, grp, WINDOW, A_KV_W), lambda b: (layer, b, 0, 0))
    cache_out = pl.BlockSpec((None, grp, WINDOW, A_KV_W), lambda b: (layer, b, 0, 0))
    stacked = jax.ShapeDtypeStruct((DEPTH, DEC_BATCH, WINDOW, A_KV_W), _F32)
    any_space = pl.BlockSpec(memory_space=pl.ANY)
    const = lambda w: pl.BlockSpec((w, w), lambda b: (0, 0))
    return pl.pallas_call(
        _attn_sample_kernel,
        grid=(DEC_BATCH // grp,),
        in_specs=[pl.BlockSpec((None, 1, A_HEADS), lambda b: (layer, 0, 0), memory_space=pltpu.SMEM),
                  pl.BlockSpec((t, W_A), lambda b: (first + b, 0)),
                  cache_in, cache_in, table, table, table,
                  _layer_vec(A_Q_W, layer), _layer_vec(A_KV_W, layer), const(LANES),
                  any_space, any_space, any_space],
        out_specs=[pl.BlockSpec((t, A_Q_W), lambda b: (first + b, 0)), cache_out, cache_out],
        out_shape=[jax.ShapeDtypeStruct((M_ALL, A_Q_W), _BF16), stacked, stacked],
        scratch_shapes=[pltpu.VMEM((grp, WINDOW + 8, A_KV_W), _F32),
                        pltpu.VMEM((grp, WINDOW + 8, A_KV_W), _F32),
                        pltpu.VMEM((t + 8, A_KV_W), _F32), pltpu.VMEM((t + 8, A_KV_W), _F32)],
        input_output_aliases={10: 0, 11: 1, 12: 2},
        compiler_params=_params(1),
        name="attn_sample",
    )(sinks, a, cache_k, cache_v, c, s1, s2, gq, gk, _head_mean_matrix(),
      o_full, kc_all, vc_all)


def _lower_bound_kernel(logits_ref, loglb_ref, log1m_ref):
    x = logits_ref[...]
    e = jnp.exp(x - jnp.max(x, axis=0, keepdims=True))
    sm = e / jnp.sum(e, axis=0, keepdims=True)
    acc = sm[0:1]
    rows = [acc]
    for l in range(1, DEPTH):
        acc = acc + sm[l:l + 1]
        rows.append(acc)
    for l in range(DEPTH):
        lb = rows[l] - rows[0]
        loglb_ref[l:l + 1, :] = jnp.log(lb)
        log1m_ref[l:l + 1, :] = jnp.log1p(-lb)


def _lower_bounds(lb_logits):
    shape = jax.ShapeDtypeStruct(lb_logits.shape, _F32)
    return pl.pallas_call(_lower_bound_kernel, out_shape=[shape, shape], name="hgrn_lower_bounds")(lb_logits)


def _chunk_matrices(rows, chunk):
    t = np.arange(rows)[:, None]
    s = np.arange(rows)[None, :]
    same = (t // chunk) == (s // chunk)
    tri = (same & (s <= t)).astype(np.float32)
    ref = (same & ((s % chunk) <= chunk // 2)).astype(np.float32)
    last = same.astype(np.float32)
    return jnp.asarray(np.concatenate([tri, tri - ref, last - tri], axis=0), dtype=_BF16)


def _split2(x):
    hi = x.astype(_BF16)
    return hi, (x - hi.astype(_F32)).astype(_BF16)


def _hgrn_gates(q, z, loglb, log1m, lt, rows):
    b = log1m + _log_sigmoid(z)
    log_f = jnp.maximum(loglb, b) + jnp.log(1.0 + jnp.exp(-jnp.abs(loglb - b)))
    kk = -jnp.tanh(0.5 * log_f) * (jnp.exp(log_f) + 1.0)
    hi, lo = _split2(log_f)
    cums = jnp.dot(lt, hi, preferred_element_type=_F32) + jnp.dot(lt, lo, preferred_element_type=_F32)
    cum = cums[0:rows]
    cum_ref = cums[rows:2 * rows]
    cum_end = cums[2 * rows:3 * rows]
    e_cum = jnp.exp(cum)
    q_intra = q * jnp.exp(cum_ref)
    k_intra = kk * jnp.exp(-cum_ref)
    q_inter = q * e_cum
    k_state = kk * jnp.exp(cum_end)
    return q_intra, k_intra, q_inter, k_state, e_cum


def _hgrn_finish(o, g, w):
    y = o * lax.rsqrt(jnp.mean(o * o, axis=-1, keepdims=True) + EPS) * w
    return (y * (g * _sigmoid(g))).astype(_BF16)


def _hgrn_intra(q_intra, k_intra, v, causal):
    att = lax.dot_general(q_intra.astype(_BF16), k_intra.astype(_BF16), _NT, preferred_element_type=_F32)
    att = jnp.where(causal, att, 0.0)
    return jnp.dot(att.astype(_BF16), v, preferred_element_type=_F32)


def _hgrn_prompt_kernel(q_lo, q_hi, f_lo, f_hi, i_lo, i_hi, g_lo, g_hi, loglb_ref, log1m_ref, lt_ref, w_ref,
                        s_all_ref, o_ref, s_ref, st, hand):
    q_ref, f_ref, i_ref, g_ref = (_WideRef(q_lo, q_hi), _WideRef(f_lo, f_hi), _WideRef(i_lo, i_hi),
                                  _WideRef(g_lo, g_hi))
    b = pl.program_id(0)
    n = pl.program_id(1)
    nb = pl.num_programs(1) - 2

    @pl.when(n == 0)
    def _():
        st[...] = jnp.zeros(st.shape, _F32)
        hand[...] = jnp.zeros(hand.shape, _F32)

    def step(slot, done):
        row = lax.broadcasted_iota(jnp.int32, (BLK, BLK), 0)
        col = lax.broadcasted_iota(jnp.int32, (BLK, BLK), 1)
        causal = (row // B_CHUNK == col // B_CHUNK) & (col <= row)
        chunks = [slice(c * B_CHUNK, (c + 1) * B_CHUNK) for c in range(BLK // B_CHUNK)]
        w = w_ref[...]
        lt = lt_ref[...]
        heads = [slice(h * B_KEY_DIM, (h + 1) * B_KEY_DIM) for h in range(B_HEADS)]
        intra, updates = [], []
        for lanes in heads:
            for j, t in enumerate(_hgrn_gates(q_ref[:, lanes], f_ref[:, lanes], loglb_ref[:, lanes],
                                              log1m_ref[:, lanes], lt, BLK)):
                hand[slot, j, :, lanes] = t
            v = i_ref[:, lanes].astype(_BF16)
            ks = hand[done, 3, :, lanes].astype(_BF16)
            intra.append(_hgrn_intra(hand[done, 0, :, lanes], hand[done, 1, :, lanes], v, causal))
            updates.append([lax.dot_general(v[rows], ks[rows], _TN, preferred_element_type=_F32)
                            for rows in chunks])
        before = []
        for h, lanes in enumerate(heads):
            state_t = st[h]
            seen = []
            for rows, update in zip(chunks, updates[h]):
                seen.append(state_t.astype(_BF16))
                state_t = state_t * hand[done, 4, rows.stop - 1:rows.stop, lanes] + update
            st[h] = state_t
            before.append(seen)
        for h, lanes in enumerate(heads):
            qi = hand[done, 2, :, lanes].astype(_BF16)
            o_inter = [lax.dot_general(qi[rows], s_t, _NT, preferred_element_type=_F32)
                       for rows, s_t in zip(chunks, before[h])]
            o_ref[:, lanes] = _hgrn_finish(intra[h] + jnp.concatenate(o_inter, axis=0), g_ref[:, lanes], w)

    for parity in range(2):
        pl.when((n <= nb) & (n % 2 == parity))(lambda parity=parity: step(parity, 1 - parity))

    @pl.when(n == nb)
    def _():
        for h in range(B_HEADS):
            s_ref[0, h] = st[h].T

    @pl.when((n == nb + 1) & (b == pl.num_programs(0) - 1))
    def _():
        o_ref[...] = jnp.zeros(o_ref.shape, o_ref.dtype)


def _gate_columns(j):
    first = (OFF_B + j * B_W) // TN
    return (first, first + 1)


def _hgrn_prompt(proj, loglb, log1m, w, s_all, layer):
    nb = SEQ // BLK

    def ahead(j):
        return [pl.BlockSpec((BLK, TN), lambda b, n, c=c: (b * nb + jnp.minimum(n, nb - 1), c))
                for c in _gate_columns(j)]

    def behind(j):
        return [pl.BlockSpec((BLK, TN), lambda b, n, c=c: (b * nb + jnp.clip(n - 1, 0, nb - 1), c))
                for c in _gate_columns(j)]

    def out_block(b, n):
        tail = (n == nb + 1) & (b == BATCH - 1)
        return (jnp.where(tail, BATCH * nb, b * nb + jnp.clip(n - 1, 0, nb - 1)), 0)

    return pl.pallas_call(
        _hgrn_prompt_kernel,
        grid=(BATCH, nb + 2),
        in_specs=[*ahead(0), *ahead(1), *behind(2), *behind(3), _layer_vec(B_W, layer), _layer_vec(B_W, layer),
                  pl.BlockSpec((3 * BLK, BLK), lambda b, n: (0, 0)),
                  _layer_vec(B_VAL_DIM, layer), pl.BlockSpec(memory_space=pl.ANY)],
        out_specs=[pl.BlockSpec((BLK, B_W), out_block),
                   pl.BlockSpec((None, 1, B_HEADS, B_KEY_DIM, B_VAL_DIM), lambda b, n: (layer, b, 0, 0, 0))],
        out_shape=[jax.ShapeDtypeStruct((M_ALL, B_W), _BF16),
                   jax.ShapeDtypeStruct((DEPTH, BATCH, B_HEADS, B_KEY_DIM, B_VAL_DIM), _F32)],
        scratch_shapes=[pltpu.VMEM((B_HEADS, B_VAL_DIM, B_KEY_DIM), _F32),
                        pltpu.VMEM((2, 5, BLK, B_W), _F32)],
        input_output_aliases={12: 1},
        compiler_params=_params(2),
        name="hgrn_prompt",
    )(*(proj,) * 8, loglb, log1m, _chunk_matrices(BLK, B_CHUNK), w, s_all)


def _hgrn_sample_kernel(q_lo, q_hi, f_lo, f_hi, i_lo, i_hi, g_lo, g_hi, loglb_ref, log1m_ref, lt_ref, w_ref,
                        s0_ref, o_full_ref, s_all_ref, o_ref, s_ref):
    q_ref, f_ref, i_ref, g_ref = (_WideRef(q_lo, q_hi), _WideRef(f_lo, f_hi), _WideRef(i_lo, i_hi),
                                  _WideRef(g_lo, g_hi))
    del o_full_ref, s_all_ref
    rows = HGRN_SAMPLE_GROUP * DEC_SEQ

    def padded(ref):
        return jnp.concatenate([ref[...], jnp.zeros((BLK - rows, B_W), _F32)], axis=0)

    v_all = padded(i_ref)
    g_all = padded(g_ref)
    q_intra, k_intra, q_inter, k_state, e_cum = _hgrn_gates(
        padded(q_ref), padded(f_ref), loglb_ref[...], log1m_ref[...], lt_ref[...], BLK)
    row = lax.broadcasted_iota(jnp.int32, (BLK, BLK), 0)
    col = lax.broadcasted_iota(jnp.int32, (BLK, BLK), 1)
    causal = (row // DEC_SEQ == col // DEC_SEQ) & (col <= row)
    w = w_ref[...]
    for h in range(B_HEADS):
        lanes = slice(h * B_KEY_DIM, (h + 1) * B_KEY_DIM)
        v = v_all[:, lanes].astype(_BF16)
        o = _hgrn_intra(q_intra[:, lanes], k_intra[:, lanes], v, causal)
        qi = q_inter[:, lanes].astype(_BF16)
        ec_t = e_cum[:, lanes].T
        ks_t = k_state[:, lanes].T
        for s in range(HGRN_SAMPLE_GROUP):
            state = s0_ref[s, h]
            o_s = jnp.dot(qi, state.astype(_BF16), preferred_element_type=_F32)
            o = o + jnp.where(row // DEC_SEQ == s, o_s, 0.0)
            ks_seq = jnp.where(col // DEC_SEQ == s, ks_t, 0.0).astype(_BF16)
            update = jnp.dot(ks_seq, v, preferred_element_type=_F32)
            decay = ec_t[:, (s + 1) * DEC_SEQ - 1:(s + 1) * DEC_SEQ]
            s_ref[s, h] = state * decay + update
        o_ref[:, lanes] = _hgrn_finish(o, g_all[:, lanes], w)[0:rows]


def _hgrn_sample(proj, o_full, state, loglb, log1m, w, s_all, layer):
    rows = HGRN_SAMPLE_GROUP * DEC_SEQ
    first = M_PROMPT // rows
    gate = lambda j: [pl.BlockSpec((rows, TN), lambda s, c=c: (first + s, c)) for c in _gate_columns(j)]
    state_shape = (HGRN_SAMPLE_GROUP, B_HEADS, B_KEY_DIM, B_VAL_DIM)
    return pl.pallas_call(
        _hgrn_sample_kernel,
        grid=(DEC_BATCH // HGRN_SAMPLE_GROUP,),
        in_specs=[*gate(0), *gate(1), *gate(2), *gate(3), _layer_vec(B_W, layer), _layer_vec(B_W, layer),
                  pl.BlockSpec((3 * BLK, BLK), lambda s: (0, 0)),
                  _layer_vec(B_VAL_DIM, layer),
                  pl.BlockSpec((None,) + state_shape, lambda s: (layer, s, 0, 0, 0)),
                  pl.BlockSpec(memory_space=pl.ANY), pl.BlockSpec(memory_space=pl.ANY)],
        out_specs=[pl.BlockSpec((rows, B_W), lambda s: (first + s, 0)),
                   pl.BlockSpec((None,) + state_shape, lambda s: (layer, s, 0, 0, 0))],
        out_shape=[jax.ShapeDtypeStruct((M_ALL, B_W), _BF16),
                   jax.ShapeDtypeStruct(state.shape, _F32)],
        input_output_aliases={13: 0, 14: 1},
        compiler_params=_params(1),
        name="hgrn_sample",
    )(*(proj,) * 8, loglb, log1m, _chunk_matrices(BLK, DEC_SEQ), w, state, o_full, s_all)


def _sgu_kernel(u_lo, u_hi, v_lo, v_hi, w_ref, b_ref, g_ref, vn_all_ref, o_ref, vn_ref):
    u_ref, v_ref = _WideRef(u_lo, u_hi), _WideRef(v_lo, v_hi)
    is_sample = pl.program_id(0) == pl.num_programs(0) - 1
    row = lax.broadcasted_iota(jnp.int32, (BLK, BLK), 0)
    col = lax.broadcasted_iota(jnp.int32, (BLK, BLK), 1)
    same_seq = jnp.logical_or(jnp.logical_not(is_sample), row // DEC_SEQ == col // DEC_SEQ)
    causal = (col <= row) & same_seq
    gain = g_ref[...]
    bias = b_ref[...]
    for g in range(C_GROUPS):
        lanes = slice(g * C_GROUP_DIM, (g + 1) * C_GROUP_DIM)
        v = v_ref[:, lanes]
        vn = v * lax.rsqrt(jnp.mean(v * v, axis=-1, keepdims=True) + EPS) * gain
        w = jnp.where(causal, w_ref[g], 0.0).astype(_BF16)
        z = jnp.dot(w, vn.astype(_BF16), preferred_element_type=_F32) + bias[:, g:g + 1]
        o_ref[:, lanes] = (u_ref[:, lanes] * z).astype(_BF16)
        vn_ref[:, lanes] = vn


def _sgu(proj, w2, b2, gain, vn_all, layer):
    nblk = M_ALL // BLK
    which = lambda i: i // (nblk - 1)
    return pl.pallas_call(
        _sgu_kernel,
        grid=(nblk,),
        in_specs=[*(pl.BlockSpec((BLK, TN), lambda i, c=c: (i, OFF_C // TN + c)) for c in range(4)),
                  pl.BlockSpec((None, None, C_GROUPS, C_CHUNK, C_CHUNK), lambda i: (layer, which(i), 0, 0, 0)),
                  pl.BlockSpec((None, None, C_CHUNK, C_GROUPS), lambda i: (layer, which(i), 0, 0)),
                  _layer_vec(C_GROUP_DIM, layer), pl.BlockSpec(memory_space=pl.ANY)],
        out_specs=[pl.BlockSpec((BLK, C_W), lambda i: (i, 0)),
                   pl.BlockSpec((None, BLK, C_W), lambda i: (layer, 0, 0))],
        out_shape=[jax.ShapeDtypeStruct((M_ALL, C_W), _BF16),
                   jax.ShapeDtypeStruct((DEPTH, M_SAMPLE, C_W), _F32)],
        input_output_aliases={7: 1},
        compiler_params=_params(1),
        name="sgu",
    )(*(proj,) * 4, w2, b2, gain, vn_all)


def _sgu_params(w_spatial, b_spatial):
    reps = BLK // DEC_SEQ
    w_sample = jnp.tile(w_spatial[:, :, :DEC_SEQ, :DEC_SEQ], (1, 1, reps, reps))
    b_sample = jnp.tile(b_spatial[:, :, :DEC_SEQ], (1, 1, reps))
    w2 = jnp.stack([w_spatial, w_sample], axis=1)
    b2 = jnp.stack([jnp.swapaxes(b_spatial, 1, 2), jnp.swapaxes(b_sample, 1, 2)], axis=1)
    return w2, b2


def kernel(x_prompt, x_sample, cache_k, cache_v, state_hgrn, norm_mix, w_in, q_norm, k_norm, sinks,
           lb_logits, hgrn_out_norm, sgu_v_norm, w_spatial, b_spatial, w_branch_a, w_branch_b,
           w_branch_c, w_out, norm_ffn, w_ffn_up, w_ffn_down):
    loglb, log1m = _lower_bounds(lb_logits)
    loglb = loglb.reshape(DEPTH, 1, B_W)
    log1m = log1m.reshape(DEPTH, 1, B_W)
    tables_prompt = _rope_tables(0, SEQ)
    tables_sample = tuple(jnp.tile(t, (ATTN_SAMPLE_GROUP, 1)) for t in _rope_tables(PAST_LEN, DEC_SEQ))
    gq = jnp.tile(q_norm, (1, A_HEADS)).reshape(DEPTH, 1, A_Q_W)
    gk = jnp.tile(k_norm, (1, A_KV_HEADS)).reshape(DEPTH, 1, A_KV_W)
    w_hg = hgrn_out_norm.reshape(DEPTH, 1, B_VAL_DIM)
    w_sg = sgu_v_norm.reshape(DEPTH, 1, C_GROUP_DIM)
    g_mix = norm_mix.reshape(DEPTH, 1, D_MODEL)
    g_ffn = norm_ffn.reshape(DEPTH, 1, D_MODEL)
    sinks = sinks.reshape(DEPTH, 1, A_HEADS)
    w_out_bf16 = w_out.astype(_BF16)
    w2, b2 = _sgu_params(w_spatial, b_spatial)
    ck_all = cache_k.reshape(DEPTH, DEC_BATCH, WINDOW, A_KV_W)
    cv_all = cache_v.reshape(DEPTH, DEC_BATCH, WINDOW, A_KV_W)

    kc_p = jnp.zeros((DEPTH, BATCH, WINDOW, A_KV_W), _F32)
    vc_p = jnp.zeros((DEPTH, BATCH, WINDOW, A_KV_W), _F32)
    kc_s = jnp.zeros((DEPTH, DEC_BATCH, WINDOW, A_KV_W), _F32)
    vc_s = jnp.zeros((DEPTH, DEC_BATCH, WINDOW, A_KV_W), _F32)
    st_p = jnp.zeros((DEPTH, BATCH, B_HEADS, B_KEY_DIM, B_VAL_DIM), _F32)
    st_s = jnp.zeros(state_hgrn.shape, _F32)
    vn_s = jnp.zeros((DEPTH, M_SAMPLE, C_W), _F32)
    for l in range(DEPTH):
        if l == 0:
            x, h = _join_norm(x_prompt.reshape(M_PROMPT, D_MODEL), x_sample.reshape(M_SAMPLE, D_MODEL), g_mix, l)
        else:
            h = _rmsnorm(x, g_mix, l)
        proj = _proj_in(h, w_in, l)

        oa, kc_p, vc_p = _attn_prompt(proj, sinks, gq, gk, tables_prompt, kc_p, vc_p, l)
        oa, kc_s, vc_s = _attn_sample(proj, oa, ck_all, cv_all, sinks, gq, gk, tables_sample, kc_s, vc_s, l)
        ob, st_p = _hgrn_prompt(proj, loglb, log1m, w_hg, st_p, l)
        ob, st_s = _hgrn_sample(proj, ob, state_hgrn, loglb, log1m, w_hg, st_s, l)
        oc, vn_s = _sgu(proj, w2, b2, w_sg, vn_s, l)
        merged = _merge(oa, ob, oc, proj, w_branch_a, w_branch_b, w_branch_c, l)
        x, h2 = _matmul_residual_norm(merged, w_out_bf16, l, x, g_ffn)
        act = _ffn_up(h2, w_ffn_up, l)
        x = _matmul_residual(act, w_ffn_down, l, x, TM_DOWN, TN_NARROW)


    y_prompt = x[:M_PROMPT].reshape(BATCH, SEQ, D_MODEL)
    y_sample = x[M_PROMPT:].reshape(DEC_BATCH, DEC_SEQ, D_MODEL)
    heads_p = (DEPTH, BATCH, WINDOW, A_KV_HEADS, A_HEAD_DIM)
    heads_s = (DEPTH, DEC_BATCH, WINDOW, A_KV_HEADS, A_HEAD_DIM)
    return (y_prompt, y_sample, kc_p.reshape(heads_p), vc_p.reshape(heads_p), st_p,
            kc_s.reshape(heads_s), vc_s.reshape(heads_s), st_s,
            vn_s.reshape(DEPTH, DEC_BATCH, DEC_SEQ, C_GROUPS, C_GROUP_DIM))
```

```python
import numpy as np
import jax
import jax.numpy as jnp
from jax import lax
from jax.experimental import pallas as pl
from jax.experimental.pallas import tpu as pltpu

D_MODEL = 2048
BATCH = 4
SEQ = 2048
DEPTH = 4
DEC_BATCH = 32
DEC_SEQ = 4
PAST_LEN = 16384

A_HEADS = 16
A_KV_HEADS = 4
A_HEAD_DIM = 64
A_GROUP = A_HEADS // A_KV_HEADS
WINDOW = 128
ROT_DIM = A_HEAD_DIM // 4
ROT_HALF = ROT_DIM // 2
ROPE_THETA = 500000.0
B_HEADS = 8
B_KEY_DIM = 128
B_VAL_DIM = 128
B_CHUNK = 16
C_GROUPS = 8
C_GROUP_DIM = 128
C_CHUNK = 128
A_Q_W = A_HEADS * A_HEAD_DIM
A_KV_W = A_KV_HEADS * A_HEAD_DIM
B_W = B_HEADS * B_KEY_DIM
C_W = C_GROUPS * C_GROUP_DIM
FFN_DIM = ((8 * D_MODEL + 3 * 256 - 1) // (3 * 256)) * 256
EPS = 1e-6

M_PROMPT = BATCH * SEQ
M_SAMPLE = DEC_BATCH * DEC_SEQ
M_ALL = M_PROMPT + M_SAMPLE

OFF_A = 0
W_A = A_Q_W + 2 * A_KV_W
OFF_B = OFF_A + W_A
W_B = 4 * B_W
OFF_C = OFF_B + W_B
W_C = 2 * C_W
OFF_G = OFF_C + W_C
W_G = 3 * D_MODEL
N_IN = OFF_G + W_G

LANES = 128
BLK = 128
TM = M_ALL // 4
TM_IN = M_ALL // 2
TM_DOWN = M_ALL // 8
TM_NORM = M_ALL // 16
TM_FULL = M_ALL // 16
TN = 512
TN_NARROW = 256
SGU_BLOCKS = 5
ATTN_SAMPLE_GROUP = 4
HGRN_SAMPLE_GROUP = 8
VMEM_LIMIT = 56 * 1024 * 1024

_BF16 = jnp.bfloat16
_F32 = jnp.float32
_NT = (((1,), (1,)), ((), ()))
_TN = (((0,), (0,)), ((), ()))


def _params(n_grid):
    return pltpu.CompilerParams(dimension_semantics=("arbitrary",) * n_grid,
                                vmem_limit_bytes=VMEM_LIMIT)


def _sigmoid(x):
    return 0.5 * jnp.tanh(0.5 * x) + 0.5


def _layer_vec(width, layer):
    return pl.BlockSpec((None, 1, width), lambda *_: (layer, 0, 0))


def _rmsnorm_kernel(x_ref, g_ref, o_ref):
    x = x_ref[...]
    y = x * lax.rsqrt(jnp.mean(x * x, axis=-1, keepdims=True) + EPS)
    o_ref[...] = (y * g_ref[...]).astype(_BF16)


def _rmsnorm(x, g, layer):
    m, d = x.shape
    return pl.pallas_call(
        _rmsnorm_kernel,
        grid=(m // TM_NORM,),
        in_specs=[pl.BlockSpec((TM_NORM, d), lambda i: (i, 0)), _layer_vec(d, layer)],
        out_specs=pl.BlockSpec((TM_NORM, d), lambda i: (i, 0)),
        out_shape=jax.ShapeDtypeStruct((m, d), _BF16),
        compiler_params=_params(1),
        name="rmsnorm",
    )(x, g)


def _join_norm_kernel(xp_ref, xs_ref, g_ref, x_ref, h_ref):
    is_sample = pl.program_id(0) == pl.num_programs(0) - 1
    x = jnp.where(is_sample, xs_ref[...], xp_ref[...])
    x_ref[...] = x
    y = x * lax.rsqrt(jnp.mean(x * x, axis=-1, keepdims=True) + EPS)
    h_ref[...] = (y * g_ref[...]).astype(_BF16)


def _join_norm(x_prompt, x_sample, g, layer):
    d = x_prompt.shape[1]
    nblk = M_ALL // BLK
    rows = pl.BlockSpec((BLK, d), lambda i: (i, 0))
    return pl.pallas_call(
        _join_norm_kernel,
        grid=(nblk,),
        in_specs=[pl.BlockSpec((BLK, d), lambda i: (jnp.minimum(i, nblk - 2), 0)),
                  pl.BlockSpec((BLK, d), lambda i: (0, 0)), _layer_vec(d, layer)],
        out_specs=[rows, rows],
        out_shape=[jax.ShapeDtypeStruct((M_ALL, d), _F32), jax.ShapeDtypeStruct((M_ALL, d), _BF16)],
        compiler_params=_params(1),
        name="join_norm",
    )(x_prompt, x_sample, g)


def _log_sigmoid(z):
    return jnp.minimum(z, 0.0) - jnp.log(1.0 + jnp.exp(-jnp.abs(z)))


def _mm_kernel(a_ref, w_ref, o_ref):
    o_ref[...] = jnp.dot(a_ref[...], w_ref[...].astype(_BF16), preferred_element_type=_F32)


def _proj_in(a, w, layer):
    m, k = a.shape
    return pl.pallas_call(
        _mm_kernel,
        grid=(m // TM_IN, N_IN // TN),
        in_specs=[pl.BlockSpec((TM_IN, k), lambda i, j: (i, 0), pipeline_mode=pl.Buffered(1)),
                  pl.BlockSpec((None, k, TN), lambda i, j: (layer, 0, j))],
        out_specs=pl.BlockSpec((TM_IN, TN), lambda i, j: (i, j)),
        out_shape=jax.ShapeDtypeStruct((m, N_IN), _F32),
        compiler_params=_params(2),
        name="proj_in",
    )(a, w)


class _WideRef:
    def __init__(self, low, high):
        self.parts = (low, high)

    def __getitem__(self, idx):
        if idx is Ellipsis:
            return jnp.concatenate([p[...] for p in self.parts], axis=1)
        rows, cols = idx
        part, start = divmod(cols.start, TN)
        return self.parts[part][rows, start:start + cols.stop - cols.start]


def _mm_res_kernel(a_ref, w_ref, r_ref, o_ref):
    o_ref[...] = r_ref[...] + jnp.dot(a_ref[...], w_ref[...].astype(_BF16),
                                      preferred_element_type=_F32)


def _matmul_residual(a, w, layer, r, tm, tn):
    m, k = a.shape
    n = w.shape[2]
    return pl.pallas_call(
        _mm_res_kernel,
        grid=(m // tm, n // tn),
        in_specs=[pl.BlockSpec((tm, k), lambda i, j: (i, 0)),
                  pl.BlockSpec((None, k, tn), lambda i, j: (layer, 0, j)),
                  pl.BlockSpec((tm, tn), lambda i, j: (i, j))],
        out_specs=pl.BlockSpec((tm, tn), lambda i, j: (i, j)),
        out_shape=jax.ShapeDtypeStruct((m, n), _F32),
        compiler_params=_params(2),
        name="proj_residual",
    )(a, w, r)


def _mm_res_norm_kernel(a_ref, w_ref, r_ref, g_ref, o_ref, h_ref):
    x = r_ref[...] + jnp.dot(a_ref[...], w_ref[...], preferred_element_type=_F32)
    o_ref[...] = x
    y = x * lax.rsqrt(jnp.mean(x * x, axis=-1, keepdims=True) + EPS)
    h_ref[...] = (y * g_ref[...]).astype(_BF16)


def _matmul_residual_norm(a, w_bf16, layer, r, g):
    m, k = a.shape
    n = w_bf16.shape[2]
    rows = lambda width: pl.BlockSpec((TM_FULL, width), lambda i: (i, 0))
    return pl.pallas_call(
        _mm_res_norm_kernel,
        grid=(m // TM_FULL,),
        in_specs=[rows(k),
                  pl.BlockSpec((None, k, n), lambda i: (layer, 0, 0), pipeline_mode=pl.Buffered(1)),
                  rows(n), _layer_vec(n, layer)],
        out_specs=[rows(n), rows(n)],
        out_shape=[jax.ShapeDtypeStruct((m, n), _F32), jax.ShapeDtypeStruct((m, n), _BF16)],
        compiler_params=_params(1),
        name="proj_out_norm",
    )(a, w_bf16, r, g)


def _ffn_up_kernel(a_ref, wg_ref, wu_ref, o_ref):
    a = a_ref[...]
    g = jnp.dot(a, wg_ref[...].astype(_BF16), preferred_element_type=_F32)
    u = jnp.dot(a, wu_ref[...].astype(_BF16), preferred_element_type=_F32)
    o_ref[...] = (g * _sigmoid(g) * u).astype(_BF16)


def _ffn_up(a, w_up, layer):
    m, k = a.shape
    tn = TN_NARROW
    nj = FFN_DIM // tn
    return pl.pallas_call(
        _ffn_up_kernel,
        grid=(m // TM, nj),
        in_specs=[pl.BlockSpec((TM, k), lambda i, j: (i, 0)),
                  pl.BlockSpec((None, k, tn), lambda i, j: (layer, 0, j)),
                  pl.BlockSpec((None, k, tn), lambda i, j: (layer, 0, j + nj))],
        out_specs=pl.BlockSpec((TM, tn), lambda i, j: (i, j)),
        out_shape=jax.ShapeDtypeStruct((m, FFN_DIM), _BF16),
        compiler_params=_params(2),
        name="ffn_up",
    )(a, w_up, w_up)


def _merge_kernel(oa_ref, ob_ref, oc_ref, wa_ref, wb_ref, wc_ref, ga_ref, gb_ref, gc_ref, o_ref):
    ya = jnp.dot(oa_ref[...], wa_ref[...].astype(_BF16), preferred_element_type=_F32)
    yb = jnp.dot(ob_ref[...], wb_ref[...].astype(_BF16), preferred_element_type=_F32)
    yc = jnp.dot(oc_ref[...], wc_ref[...].astype(_BF16), preferred_element_type=_F32)
    merged = _sigmoid(ga_ref[...]) * ya + _sigmoid(gb_ref[...]) * yb + _sigmoid(gc_ref[...]) * yc
    o_ref[...] = merged.astype(_BF16)


def _merge(oa, ob, oc, proj, wa, wb, wc, layer):
    m, k = oa.shape
    tn = TN_NARROW
    nj = D_MODEL // tn
    branch = pl.BlockSpec((TM, k), lambda i, j: (i, 0))
    weight = pl.BlockSpec((None, k, tn), lambda i, j: (layer, 0, j))
    return pl.pallas_call(
        _merge_kernel,
        grid=(m // TM, nj),
        in_specs=[branch, branch, branch, weight, weight, weight,
                  *(pl.BlockSpec((TM, tn), lambda i, j, g=g: (i, OFF_G // tn + g * nj + j)) for g in range(3))],
        out_specs=pl.BlockSpec((TM, tn), lambda i, j: (i, j)),
        out_shape=jax.ShapeDtypeStruct((m, D_MODEL), _BF16),
        compiler_params=_params(2),
        name="merge",
    )(oa, ob, oc, wa, wb, wc, proj, proj, proj)


def _rope_tables(p0, rows):
    pos = (p0 + jnp.arange(rows, dtype=jnp.int32)).astype(_F32)
    inv_freq = jnp.power(jnp.float32(ROPE_THETA), -jnp.arange(ROT_HALF, dtype=_F32) / ROT_HALF)
    ang = pos[:, None] * inv_freq[None, :]
    cos, sin = jnp.cos(ang), jnp.sin(ang)
    rest = A_HEAD_DIM - ROT_DIM
    zeros = jnp.zeros((rows, ROT_HALF), _F32)
    pad = jnp.zeros((rows, rest), _F32)
    c = jnp.concatenate([cos, cos, jnp.ones((rows, rest), _F32)], axis=1)
    s1 = jnp.concatenate([-sin, zeros, pad], axis=1)
    s2 = jnp.concatenate([zeros, sin, pad], axis=1)
    return tuple(jnp.tile(t, (1, LANES // A_HEAD_DIM)) for t in (c, s1, s2))


def _rope(x, c, s1, s2):
    width = x.shape[1]
    reps = width // c.shape[1]
    c, s1, s2 = (jnp.concatenate([t] * reps, axis=1) for t in (c, s1, s2))
    ahead = pltpu.roll(x, width - ROT_HALF, axis=1)
    behind = pltpu.roll(x, ROT_HALF, axis=1)
    return x * c + ahead * s1 + behind * s2


def _head_mean_matrix():
    i = np.arange(LANES)
    same = (i[:, None] // A_HEAD_DIM) == (i[None, :] // A_HEAD_DIM)
    return jnp.asarray(same.astype(np.float32) / A_HEAD_DIM, dtype=_BF16)


def _qk_prep(x, gain, head_mean, c, s1, s2, scale):
    sq = x * x
    hi = sq.astype(_BF16)
    lo = (sq - hi.astype(_F32)).astype(_BF16)
    ms = jnp.concatenate(
        [jnp.dot(hi[:, l:l + LANES], head_mean, preferred_element_type=_F32)
         + jnp.dot(lo[:, l:l + LANES], head_mean, preferred_element_type=_F32)
         for l in range(0, x.shape[1], LANES)], axis=1)
    return _rope(x * gain, c, s1, s2) * (lax.rsqrt(ms + EPS) * scale)


def _low_half(shape):
    return lax.broadcasted_iota(jnp.int32, shape, 1) < A_HEAD_DIM


def _both_halves(col, half):
    low = _low_half(col.shape)
    sel = jnp.where(low if half == 0 else jnp.logical_not(low), col, 0.0)
    return sel + pltpu.roll(sel, A_HEAD_DIM, axis=1)


def _kv_pairs(k, v, h):
    pair, half = divmod(h, 2)
    lanes = slice(pair * LANES, (pair + 1) * LANES)
    return _both_halves(k[:, lanes], half), _both_halves(v[:, lanes], half)


def _mxu_tiles(k_pair, v_pair):
    v_pair = v_pair.astype(_BF16)
    return k_pair.astype(_BF16), jnp.concatenate([v_pair, jnp.ones(v_pair.shape, _BF16)], axis=1)


def _kv_tiles(k, v, h):
    return _mxu_tiles(*_kv_pairs(k, v, h))


def _query_rows(qn, h):
    low = _low_half((qn.shape[0], LANES))
    parts = []
    for g in range(A_GROUP):
        pair, half = divmod(h * A_GROUP + g, 2)
        keep = low if half == 0 else jnp.logical_not(low)
        parts.append(jnp.where(keep, qn[:, pair * LANES:(pair + 1) * LANES], 0.0))
    return jnp.concatenate(parts, axis=0).astype(_BF16)


def _sink_rows(sinks_ref, h, t):
    return jnp.concatenate([jnp.full((t, LANES), sinks_ref[0, h * A_GROUP + g], _F32)
                            for g in range(A_GROUP)], axis=0)


def _store_heads(o_ref, o, h, t):
    low = _low_half((t, LANES))
    for j in range(A_GROUP // 2):
        even = o[(2 * j) * t:(2 * j + 1) * t]
        odd = o[(2 * j + 1) * t:(2 * j + 2) * t]
        pair = (h * A_GROUP) // 2 + j
        o_ref[:, pair * LANES:(pair + 1) * LANES] = jnp.where(low, even, odd).astype(o_ref.dtype)


def _attn_prompt_kernel(sinks_ref, q_ref, kv_ref, c_ref, s1_ref, s2_ref, gq_ref, gk_ref, hm_ref,
                        kc_all_ref, vc_all_ref, o_ref, kc_ref, vc_ref, kprev, vprev):
    b = pl.program_id(0)
    n = pl.program_id(1)
    nb = pl.num_programs(1) - 1

    @pl.when(n == 0)
    def _():
        kprev[...] = jnp.zeros(kprev.shape, _F32)
        vprev[...] = jnp.zeros(vprev.shape, _F32)

    @pl.when(n < nb)
    def _():
        kv = kv_ref[...]
        v = kv[:, A_KV_W:]
        c, s1, s2 = c_ref[...], s1_ref[...], s2_ref[...]
        qn = _qk_prep(q_ref[...], gq_ref[...], hm_ref[...], c, s1, s2, A_HEAD_DIM ** -0.5)
        kn = _qk_prep(kv[:, :A_KV_W], gk_ref[...], hm_ref[...], c, s1, s2, 1.0)
        kc_ref[0] = kn
        vc_ref[0] = v

        rows = A_GROUP * BLK
        row = lax.broadcasted_iota(jnp.int32, (rows, WINDOW), 0) % BLK
        col = lax.broadcasted_iota(jnp.int32, (rows, WINDOW), 1)
        before = col > row
        has_prev = n > 0
        for h in range(A_KV_HEADS):
            k_pair, v_pair = _kv_pairs(kn, v, h)
            k_tile, v_tile = _mxu_tiles(k_pair, v_pair)
            k_before, v_before = _mxu_tiles(kprev[h], vprev[h])
            kprev[h] = k_pair
            vprev[h] = v_pair
            q4 = _query_rows(qn, h)
            s_prev = lax.dot_general(q4, k_before, _NT, preferred_element_type=_F32)
            s_cur = lax.dot_general(q4, k_tile, _NT, preferred_element_type=_F32)
            s = jnp.where(before, jnp.where(has_prev, s_prev, -jnp.inf), s_cur)
            sink = _sink_rows(sinks_ref, h, BLK)
            m = jnp.maximum(jnp.broadcast_to(jnp.max(s, axis=-1, keepdims=True), s.shape), sink)
            p = jnp.exp(s - m)
            acc = (jnp.dot(jnp.where(before, p, 0.0).astype(_BF16), v_before, preferred_element_type=_F32)
                   + jnp.dot(jnp.where(before, 0.0, p).astype(_BF16), v_tile, preferred_element_type=_F32))
            o = acc[:, :LANES] / (acc[:, LANES:] + jnp.exp(sink - m))
            _store_heads(o_ref, o, h, BLK)

    @pl.when((n == nb) & (b == pl.num_programs(0) - 1))
    def _():
        o_ref[...] = jnp.zeros(o_ref.shape, o_ref.dtype)


def _attn_prompt(a, sinks, gq, gk, tables, kc_all, vc_all, layer):
    nb = SEQ // BLK
    c, s1, s2 = tables
    block = lambda b, n: b * nb + jnp.minimum(n, nb - 1)
    table = pl.BlockSpec((BLK, LANES), lambda b, n: (jnp.minimum(n, nb - 1), 0))
    cache = pl.BlockSpec((None, 1, BLK, A_KV_W), lambda b, n: (layer, b, 0, 0))
    stacked = jax.ShapeDtypeStruct((DEPTH, BATCH, WINDOW, A_KV_W), _F32)
    const = lambda w: pl.BlockSpec((w, w), lambda b, n: (0, 0))

    def out_block(b, n):
        tail = (n == nb) & (b == BATCH - 1)
        return (jnp.where(tail, BATCH * nb, block(b, n)), 0)

    return pl.pallas_call(
        _attn_prompt_kernel,
        grid=(BATCH, nb + 1),
        in_specs=[pl.BlockSpec((None, 1, A_HEADS), lambda b, n: (layer, 0, 0), memory_space=pltpu.SMEM),
                  pl.BlockSpec((BLK, A_Q_W), lambda b, n: (block(b, n), 0)),
                  pl.BlockSpec((BLK, 2 * A_KV_W), lambda b, n: (block(b, n), A_Q_W // (2 * A_KV_W))),
                  table, table, table,
                  _layer_vec(A_Q_W, layer), _layer_vec(A_KV_W, layer), const(LANES),
                  pl.BlockSpec(memory_space=pl.ANY), pl.BlockSpec(memory_space=pl.ANY)],
        out_specs=[pl.BlockSpec((BLK, A_Q_W), out_block), cache, cache],
        out_shape=[jax.ShapeDtypeStruct((M_ALL, A_Q_W), _BF16), stacked, stacked],
        input_output_aliases={9: 1, 10: 2},
        scratch_shapes=[pltpu.VMEM((A_KV_HEADS, BLK, LANES), _F32),
                        pltpu.VMEM((A_KV_HEADS, BLK, LANES), _F32)],
        compiler_params=_params(2),
        name="attn_prompt",
    )(sinks, a, a, c, s1, s2, gq, gk, _head_mean_matrix(), kc_all, vc_all)


def _attn_sample_kernel(sinks_ref, qkv_ref, ck_ref, cv_ref, c_ref, s1_ref, s2_ref, gq_ref, gk_ref,
                        hm_ref, o_full_ref, kc_all_ref, vc_all_ref, o_ref, kc_ref, vc_ref,
                        kbuf, vbuf, knew, vnew):
    del o_full_ref, kc_all_ref, vc_all_ref
    grp = ATTN_SAMPLE_GROUP
    t = grp * DEC_SEQ
    qkv = qkv_ref[...]
    v = qkv[:, A_Q_W + A_KV_W:]
    c, s1, s2 = c_ref[...], s1_ref[...], s2_ref[...]
    qn = _qk_prep(qkv[:, :A_Q_W], gq_ref[...], hm_ref[...], c, s1, s2, A_HEAD_DIM ** -0.5)
    kn = _qk_prep(qkv[:, A_Q_W:A_Q_W + A_KV_W], gk_ref[...], hm_ref[...], c, s1, s2, 1.0)

    knew[...] = jnp.zeros(knew.shape, _F32)
    vnew[...] = jnp.zeros(vnew.shape, _F32)
    knew[0:t, :] = kn
    vnew[0:t, :] = v
    for s in range(grp):
        kbuf[s, 0:WINDOW, :] = ck_ref[s]
        vbuf[s, 0:WINDOW, :] = cv_ref[s]
        kbuf[s, WINDOW:, :] = knew[s * DEC_SEQ:s * DEC_SEQ + 8, :]
        vbuf[s, WINDOW:, :] = vnew[s * DEC_SEQ:s * DEC_SEQ + 8, :]
        kc_ref[s] = kbuf[s, DEC_SEQ:DEC_SEQ + WINDOW, :]
        vc_ref[s] = vbuf[s, DEC_SEQ:DEC_SEQ + WINDOW, :]

    rows = A_GROUP * t
    r_old = lax.broadcasted_iota(jnp.int32, (rows, grp * WINDOW), 0) % t
    c_old = lax.broadcasted_iota(jnp.int32, (rows, grp * WINDOW), 1)
    see_old = (c_old // WINDOW == r_old // DEC_SEQ) & (c_old % WINDOW > r_old % DEC_SEQ)
    r_new = lax.broadcasted_iota(jnp.int32, (rows, WINDOW), 0) % t
    c_new = lax.broadcasted_iota(jnp.int32, (rows, WINDOW), 1)
    see_new = (c_new < t) & (c_new // DEC_SEQ == r_new // DEC_SEQ) & (c_new % DEC_SEQ <= r_new % DEC_SEQ)
    pad_k = jnp.zeros((WINDOW - t, LANES), _BF16)
    pad_v = jnp.zeros((WINDOW - t, 2 * LANES), _BF16)
    for h in range(A_KV_HEADS):
        old = [_kv_tiles(ck_ref[s], cv_ref[s], h) for s in range(grp)]
        k_old = jnp.concatenate([kt for kt, _ in old], axis=0)
        v_old = jnp.concatenate([vt for _, vt in old], axis=0)
        k_new, v_new = _kv_tiles(kn, v, h)
        k_new = jnp.concatenate([k_new, pad_k], axis=0)
        v_new = jnp.concatenate([v_new, pad_v], axis=0)
        q4 = _query_rows(qn, h)
        s_old = jnp.where(see_old, lax.dot_general(q4, k_old, _NT, preferred_element_type=_F32), -jnp.inf)
        s_new = jnp.where(see_new, lax.dot_general(q4, k_new, _NT, preferred_element_type=_F32), -jnp.inf)
        sink = _sink_rows(sinks_ref, h, t)
        top = jnp.maximum(jnp.max(s_old, axis=-1, keepdims=True), jnp.max(s_new, axis=-1, keepdims=True))
        m = jnp.maximum(jnp.broadcast_to(top, sink.shape), sink)
        p_old = jnp.exp(s_old - jnp.concatenate([m] * grp, axis=1))
        p_new = jnp.exp(s_new - m)
        acc = (jnp.dot(p_old.astype(_BF16), v_old, preferred_element_type=_F32)
               + jnp.dot(p_new.astype(_BF16), v_new, preferred_element_type=_F32))
        o = acc[:, :LANES] / (acc[:, LANES:] + jnp.exp(sink - m))
        _store_heads(o_ref, o, h, t)


def _attn_sample(a, o_full, cache_k, cache_v, sinks, gq, gk, tables, kc_all, vc_all, layer):
    c, s1, s2 = tables
    grp = ATTN_SAMPLE_GROUP
    t = grp * DEC_SEQ
    first = M_PROMPT // t
    table = pl.BlockSpec((t, LANES), lambda b: (0, 0))
    cache_in = pl.BlockSpec((None, grp, WINDOW, A_KV_W), lambda b: (layer, b, 0, 0))
    cache_out = pl.BlockSpec((None, grp, WINDOW, A_KV_W), lambda b: (layer, b, 0, 0))
    stacked = jax.ShapeDtypeStruct((DEPTH, DEC_BATCH, WINDOW, A_KV_W), _F32)
    any_space = pl.BlockSpec(memory_space=pl.ANY)
    const = lambda w: pl.BlockSpec((w, w), lambda b: (0, 0))
    return pl.pallas_call(
        _attn_sample_kernel,
        grid=(DEC_BATCH // grp,),
        in_specs=[pl.BlockSpec((None, 1, A_HEADS), lambda b: (layer, 0, 0), memory_space=pltpu.SMEM),
                  pl.BlockSpec((t, W_A), lambda b: (first + b, 0)),
                  cache_in, cache_in, table, table, table,
                  _layer_vec(A_Q_W, layer), _layer_vec(A_KV_W, layer), const(LANES),
                  any_space, any_space, any_space],
        out_specs=[pl.BlockSpec((t, A_Q_W), lambda b: (first + b, 0)), cache_out, cache_out],
        out_shape=[jax.ShapeDtypeStruct((M_ALL, A_Q_W), _BF16), stacked, stacked],
        scratch_shapes=[pltpu.VMEM((grp, WINDOW + 8, A_KV_W), _F32),
                        pltpu.VMEM((grp, WINDOW + 8, A_KV_W), _F32),
                        pltpu.VMEM((t + 8, A_KV_W), _F32), pltpu.VMEM((t + 8, A_KV_W), _F32)],
        input_output_aliases={10: 0, 11: 1, 12: 2},
        compiler_params=_params(1),
        name="attn_sample",
    )(sinks, a, cache_k, cache_v, c, s1, s2, gq, gk, _head_mean_matrix(),
      o_full, kc_all, vc_all)


def _lower_bound_kernel(logits_ref, loglb_ref, log1m_ref):
    x = logits_ref[...]
    e = jnp.exp(x - jnp.max(x, axis=0, keepdims=True))
    sm = e / jnp.sum(e, axis=0, keepdims=True)
    acc = sm[0:1]
    rows = [acc]
    for l in range(1, DEPTH):
        acc = acc + sm[l:l + 1]
        rows.append(acc)
    for l in range(DEPTH):
        lb = rows[l] - rows[0]
        loglb_ref[l:l + 1, :] = jnp.log(lb)
        log1m_ref[l:l + 1, :] = jnp.log1p(-lb)


def _lower_bounds(lb_logits):
    shape = jax.ShapeDtypeStruct(lb_logits.shape, _F32)
    return pl.pallas_call(_lower_bound_kernel, out_shape=[shape, shape], name="hgrn_lower_bounds")(lb_logits)


def _chunk_matrices(rows, chunk):
    t = np.arange(rows)[:, None]
    s = np.arange(rows)[None, :]
    same = (t // chunk) == (s // chunk)
    tri = (same & (s <= t)).astype(np.float32)
    ref = (same & ((s % chunk) <= chunk // 2)).astype(np.float32)
    last = same.astype(np.float32)
    return jnp.asarray(np.concatenate([tri, tri - ref, last - tri], axis=0), dtype=_BF16)


def _split2(x):
    hi = x.astype(_BF16)
    return hi, (x - hi.astype(_F32)).astype(_BF16)


def _hgrn_gates(q, z, loglb, log1m, lt, rows):
    b = log1m + _log_sigmoid(z)
    log_f = jnp.maximum(loglb, b) + jnp.log(1.0 + jnp.exp(-jnp.abs(loglb - b)))
    kk = -jnp.tanh(0.5 * log_f) * (jnp.exp(log_f) + 1.0)
    hi, lo = _split2(log_f)
    cums = jnp.dot(lt, hi, preferred_element_type=_F32) + jnp.dot(lt, lo, preferred_element_type=_F32)
    cum = cums[0:rows]
    cum_ref = cums[rows:2 * rows]
    cum_end = cums[2 * rows:3 * rows]
    e_cum = jnp.exp(cum)
    q_intra = q * jnp.exp(cum_ref)
    k_intra = kk * jnp.exp(-cum_ref)
    q_inter = q * e_cum
    k_state = kk * jnp.exp(cum_end)
    return q_intra, k_intra, q_inter, k_state, e_cum


def _hgrn_finish(o, g, w):
    y = o * lax.rsqrt(jnp.mean(o * o, axis=-1, keepdims=True) + EPS) * w
    return (y * (g * _sigmoid(g))).astype(_BF16)


def _hgrn_intra(q_intra, k_intra, v, causal):
    att = lax.dot_general(q_intra.astype(_BF16), k_intra.astype(_BF16), _NT, preferred_element_type=_F32)
    att = jnp.where(causal, att, 0.0)
    return jnp.dot(att.astype(_BF16), v, preferred_element_type=_F32)


def _hgrn_prompt_kernel(q_lo, q_hi, f_lo, f_hi, i_lo, i_hi, g_lo, g_hi, loglb_ref, log1m_ref, lt_ref, w_ref,
                        s_all_ref, o_ref, s_ref, st, hand):
    q_ref, f_ref, i_ref, g_ref = (_WideRef(q_lo, q_hi), _WideRef(f_lo, f_hi), _WideRef(i_lo, i_hi),
                                  _WideRef(g_lo, g_hi))
    b = pl.program_id(0)
    n = pl.program_id(1)
    nb = pl.num_programs(1) - 2

    @pl.when(n == 0)
    def _():
        st[...] = jnp.zeros(st.shape, _F32)
        hand[...] = jnp.zeros(hand.shape, _F32)

    def step(slot, done):
        row = lax.broadcasted_iota(jnp.int32, (BLK, BLK), 0)
        col = lax.broadcasted_iota(jnp.int32, (BLK, BLK), 1)
        causal = (row // B_CHUNK == col // B_CHUNK) & (col <= row)
        chunks = [slice(c * B_CHUNK, (c + 1) * B_CHUNK) for c in range(BLK // B_CHUNK)]
        w = w_ref[...]
        lt = lt_ref[...]
        heads = [slice(h * B_KEY_DIM, (h + 1) * B_KEY_DIM) for h in range(B_HEADS)]
        intra, updates = [], []
        for lanes in heads:
            for j, t in enumerate(_hgrn_gates(q_ref[:, lanes], f_ref[:, lanes], loglb_ref[:, lanes],
                                              log1m_ref[:, lanes], lt, BLK)):
                hand[slot, j, :, lanes] = t
            v = i_ref[:, lanes].astype(_BF16)
            ks = hand[done, 3, :, lanes].astype(_BF16)
            intra.append(_hgrn_intra(hand[done, 0, :, lanes], hand[done, 1, :, lanes], v, causal))
            updates.append([lax.dot_general(v[rows], ks[rows], _TN, preferred_element_type=_F32)
                            for rows in chunks])
        before = []
        for h, lanes in enumerate(heads):
            state_t = st[h]
            seen = []
            for rows, update in zip(chunks, updates[h]):
                seen.append(state_t.astype(_BF16))
                state_t = state_t * hand[done, 4, rows.stop - 1:rows.stop, lanes] + update
            st[h] = state_t
            before.append(seen)
        for h, lanes in enumerate(heads):
            qi = hand[done, 2, :, lanes].astype(_BF16)
            o_inter = [lax.dot_general(qi[rows], s_t, _NT, preferred_element_type=_F32)
                       for rows, s_t in zip(chunks, before[h])]
            o_ref[:, lanes] = _hgrn_finish(intra[h] + jnp.concatenate(o_inter, axis=0), g_ref[:, lanes], w)

    for parity in range(2):
        pl.when((n <= nb) & (n % 2 == parity))(lambda parity=parity: step(parity, 1 - parity))

    @pl.when(n == nb)
    def _():
        for h in range(B_HEADS):
            s_ref[0, h] = st[h].T

    @pl.when((n == nb + 1) & (b == pl.num_programs(0) - 1))
    def _():
        o_ref[...] = jnp.zeros(o_ref.shape, o_ref.dtype)


def _gate_columns(j):
    first = (OFF_B + j * B_W) // TN
    return (first, first + 1)


def _hgrn_prompt(proj, loglb, log1m, w, s_all, layer):
    nb = SEQ // BLK

    def ahead(j):
        return [pl.BlockSpec((BLK, TN), lambda b, n, c=c: (b * nb + jnp.minimum(n, nb - 1), c))
                for c in _gate_columns(j)]

    def behind(j):
        return [pl.BlockSpec((BLK, TN), lambda b, n, c=c: (b * nb + jnp.clip(n - 1, 0, nb - 1), c))
                for c in _gate_columns(j)]

    def out_block(b, n):
        tail = (n == nb + 1) & (b == BATCH - 1)
        return (jnp.where(tail, BATCH * nb, b * nb + jnp.clip(n - 1, 0, nb - 1)), 0)

    return pl.pallas_call(
        _hgrn_prompt_kernel,
        grid=(BATCH, nb + 2),
        in_specs=[*ahead(0), *ahead(1), *behind(2), *behind(3), _layer_vec(B_W, layer), _layer_vec(B_W, layer),
                  pl.BlockSpec((3 * BLK, BLK), lambda b, n: (0, 0)),
                  _layer_vec(B_VAL_DIM, layer), pl.BlockSpec(memory_space=pl.ANY)],
        out_specs=[pl.BlockSpec((BLK, B_W), out_block),
                   pl.BlockSpec((None, 1, B_HEADS, B_KEY_DIM, B_VAL_DIM), lambda b, n: (layer, b, 0, 0, 0))],
        out_shape=[jax.ShapeDtypeStruct((M_ALL, B_W), _BF16),
                   jax.ShapeDtypeStruct((DEPTH, BATCH, B_HEADS, B_KEY_DIM, B_VAL_DIM), _F32)],
        scratch_shapes=[pltpu.VMEM((B_HEADS, B_VAL_DIM, B_KEY_DIM), _F32),
                        pltpu.VMEM((2, 5, BLK, B_W), _F32)],
        input_output_aliases={12: 1},
        compiler_params=_params(2),
        name="hgrn_prompt",
    )(*(proj,) * 8, loglb, log1m, _chunk_matrices(BLK, B_CHUNK), w, s_all)


def _hgrn_sample_kernel(q_lo, q_hi, f_lo, f_hi, i_lo, i_hi, g_lo, g_hi, loglb_ref, log1m_ref, lt_ref, w_ref,
                        s0_ref, o_full_ref, s_all_ref, o_ref, s_ref):
    q_ref, f_ref, i_ref, g_ref = (_WideRef(q_lo, q_hi), _WideRef(f_lo, f_hi), _WideRef(i_lo, i_hi),
                                  _WideRef(g_lo, g_hi))
    del o_full_ref, s_all_ref
    rows = HGRN_SAMPLE_GROUP * DEC_SEQ

    def padded(ref):
        return jnp.concatenate([ref[...], jnp.zeros((BLK - rows, B_W), _F32)], axis=0)

    v_all = padded(i_ref)
    g_all = padded(g_ref)
    q_intra, k_intra, q_inter, k_state, e_cum = _hgrn_gates(
        padded(q_ref), padded(f_ref), loglb_ref[...], log1m_ref[...], lt_ref[...], BLK)
    row = lax.broadcasted_iota(jnp.int32, (BLK, BLK), 0)
    col = lax.broadcasted_iota(jnp.int32, (BLK, BLK), 1)
    causal = (row // DEC_SEQ == col // DEC_SEQ) & (col <= row)
    w = w_ref[...]
    for h in range(B_HEADS):
        lanes = slice(h * B_KEY_DIM, (h + 1) * B_KEY_DIM)
        v = v_all[:, lanes].astype(_BF16)
        o = _hgrn_intra(q_intra[:, lanes], k_intra[:, lanes], v, causal)
        qi = q_inter[:, lanes].astype(_BF16)
        ec_t = e_cum[:, lanes].T
        ks_t = k_state[:, lanes].T
        for s in range(HGRN_SAMPLE_GROUP):
            state = s0_ref[s, h]
            o_s = jnp.dot(qi, state.astype(_BF16), preferred_element_type=_F32)
            o = o + jnp.where(row // DEC_SEQ == s, o_s, 0.0)
            ks_seq = jnp.where(col // DEC_SEQ == s, ks_t, 0.0).astype(_BF16)
            update = jnp.dot(ks_seq, v, preferred_element_type=_F32)
            decay = ec_t[:, (s + 1) * DEC_SEQ - 1:(s + 1) * DEC_SEQ]
            s_ref[s, h] = state * decay + update
        o_ref[:, lanes] = _hgrn_finish(o, g_all[:, lanes], w)[0:rows]


def _hgrn_sample(proj, o_full, state, loglb, log1m, w, s_all, layer):
    rows = HGRN_SAMPLE_GROUP * DEC_SEQ
    first = M_PROMPT // rows
    gate = lambda j: [pl.BlockSpec((rows, TN), lambda s, c=c: (first + s, c)) for c in _gate_columns(j)]
    state_shape = (HGRN_SAMPLE_GROUP, B_HEADS, B_KEY_DIM, B_VAL_DIM)
    return pl.pallas_call(
        _hgrn_sample_kernel,
        grid=(DEC_BATCH // HGRN_SAMPLE_GROUP,),
        in_specs=[*gate(0), *gate(1), *gate(2), *gate(3), _layer_vec(B_W, layer), _layer_vec(B_W, layer),
                  pl.BlockSpec((3 * BLK, BLK), lambda s: (0, 0)),
                  _layer_vec(B_VAL_DIM, layer),
                  pl.BlockSpec((None,) + state_shape, lambda s: (layer, s, 0, 0, 0)),
                  pl.BlockSpec(memory_space=pl.ANY), pl.BlockSpec(memory_space=pl.ANY)],
        out_specs=[pl.BlockSpec((rows, B_W), lambda s: (first + s, 0)),
                   pl.BlockSpec((None,) + state_shape, lambda s: (layer, s, 0, 0, 0))],
        out_shape=[jax.ShapeDtypeStruct((M_ALL, B_W), _BF16),
                   jax.ShapeDtypeStruct(state.shape, _F32)],
        input_output_aliases={13: 0, 14: 1},
        compiler_params=_params(1),
        name="hgrn_sample",
    )(*(proj,) * 8, loglb, log1m, _chunk_matrices(BLK, DEC_SEQ), w, state, o_full, s_all)


def _sgu_kernel(u_lo, u_hi, v_lo, v_hi, w_ref, b_ref, g_ref, vn_all_ref, o_ref, vn_ref):
    u_ref, v_ref = _WideRef(u_lo, u_hi), _WideRef(v_lo, v_hi)
    last_step = pl.program_id(0) == pl.num_programs(0) - 1
    row = lax.broadcasted_iota(jnp.int32, (BLK, BLK), 0)
    col = lax.broadcasted_iota(jnp.int32, (BLK, BLK), 1)
    gain = g_ref[...]
    for sub in range(SGU_BLOCKS):
        rows = slice(sub * BLK, (sub + 1) * BLK)
        maybe_sample = sub == SGU_BLOCKS - 1
        if maybe_sample:
            causal = (col <= row) & jnp.logical_or(jnp.logical_not(last_step), row // DEC_SEQ == col // DEC_SEQ)
            bias = jnp.where(last_step, b_ref[1], b_ref[0])
        else:
            causal = col <= row
            bias = b_ref[0]
        for g in range(C_GROUPS):
            lanes = slice(g * C_GROUP_DIM, (g + 1) * C_GROUP_DIM)
            v = v_ref[rows, lanes]
            vn = v * lax.rsqrt(jnp.mean(v * v, axis=-1, keepdims=True) + EPS) * gain
            w = jnp.where(last_step, w_ref[1, g], w_ref[0, g]) if maybe_sample else w_ref[0, g]
            w = jnp.where(causal, w, 0.0).astype(_BF16)
            z = jnp.dot(w, vn.astype(_BF16), preferred_element_type=_F32) + bias[:, g:g + 1]
            o_ref[rows, lanes] = (u_ref[rows, lanes] * z).astype(_BF16)
            if maybe_sample:
                vn_ref[:, lanes] = vn


def _sgu(proj, w2, b2, gain, vn_all, layer):
    rows = SGU_BLOCKS * BLK
    return pl.pallas_call(
        _sgu_kernel,
        grid=(M_ALL // rows,),
        in_specs=[*(pl.BlockSpec((rows, TN), lambda i, c=c: (i, OFF_C // TN + c)) for c in range(4)),
                  pl.BlockSpec((None, 2, C_GROUPS, C_CHUNK, C_CHUNK), lambda i: (layer, 0, 0, 0, 0)),
                  pl.BlockSpec((None, 2, C_CHUNK, C_GROUPS), lambda i: (layer, 0, 0, 0)),
                  _layer_vec(C_GROUP_DIM, layer), pl.BlockSpec(memory_space=pl.ANY)],
        out_specs=[pl.BlockSpec((rows, C_W), lambda i: (i, 0)),
                   pl.BlockSpec((None, BLK, C_W), lambda i: (layer, 0, 0))],
        out_shape=[jax.ShapeDtypeStruct((M_ALL, C_W), _BF16),
                   jax.ShapeDtypeStruct((DEPTH, M_SAMPLE, C_W), _F32)],
        input_output_aliases={7: 1},
        compiler_params=_params(1),
        name="sgu",
    )(*(proj,) * 4, w2, b2, gain, vn_all)


def _sgu_params(w_spatial, b_spatial):
    reps = BLK // DEC_SEQ
    w_sample = jnp.tile(w_spatial[:, :, :DEC_SEQ, :DEC_SEQ], (1, 1, reps, reps))
    b_sample = jnp.tile(b_spatial[:, :, :DEC_SEQ], (1, 1, reps))
    w2 = jnp.stack([w_spatial, w_sample], axis=1)
    b2 = jnp.stack([jnp.swapaxes(b_spatial, 1, 2), jnp.swapaxes(b_sample, 1, 2)], axis=1)
    return w2, b2


def kernel(x_prompt, x_sample, cache_k, cache_v, state_hgrn, norm_mix, w_in, q_norm, k_norm, sinks,
           lb_logits, hgrn_out_norm, sgu_v_norm, w_spatial, b_spatial, w_branch_a, w_branch_b,
           w_branch_c, w_out, norm_ffn, w_ffn_up, w_ffn_down):
    loglb, log1m = _lower_bounds(lb_logits)
    loglb = loglb.reshape(DEPTH, 1, B_W)
    log1m = log1m.reshape(DEPTH, 1, B_W)
    tables_prompt = _rope_tables(0, SEQ)
    tables_sample = tuple(jnp.tile(t, (ATTN_SAMPLE_GROUP, 1)) for t in _rope_tables(PAST_LEN, DEC_SEQ))
    gq = jnp.tile(q_norm, (1, A_HEADS)).reshape(DEPTH, 1, A_Q_W)
    gk = jnp.tile(k_norm, (1, A_KV_HEADS)).reshape(DEPTH, 1, A_KV_W)
    w_hg = hgrn_out_norm.reshape(DEPTH, 1, B_VAL_DIM)
    w_sg = sgu_v_norm.reshape(DEPTH, 1, C_GROUP_DIM)
    g_mix = norm_mix.reshape(DEPTH, 1, D_MODEL)
    g_ffn = norm_ffn.reshape(DEPTH, 1, D_MODEL)
    sinks = sinks.reshape(DEPTH, 1, A_HEADS)
    w_out_bf16 = w_out.astype(_BF16)
    w2, b2 = _sgu_params(w_spatial, b_spatial)
    ck_all = cache_k.reshape(DEPTH, DEC_BATCH, WINDOW, A_KV_W)
    cv_all = cache_v.reshape(DEPTH, DEC_BATCH, WINDOW, A_KV_W)

    kc_p = jnp.zeros((DEPTH, BATCH, WINDOW, A_KV_W), _F32)
    vc_p = jnp.zeros((DEPTH, BATCH, WINDOW, A_KV_W), _F32)
    kc_s = jnp.zeros((DEPTH, DEC_BATCH, WINDOW, A_KV_W), _F32)
    vc_s = jnp.zeros((DEPTH, DEC_BATCH, WINDOW, A_KV_W), _F32)
    st_p = jnp.zeros((DEPTH, BATCH, B_HEADS, B_KEY_DIM, B_VAL_DIM), _F32)
    st_s = jnp.zeros(state_hgrn.shape, _F32)
    vn_s = jnp.zeros((DEPTH, M_SAMPLE, C_W), _F32)
    for l in range(DEPTH):
        if l == 0:
            x, h = _join_norm(x_prompt.reshape(M_PROMPT, D_MODEL), x_sample.reshape(M_SAMPLE, D_MODEL), g_mix, l)
        else:
            h = _rmsnorm(x, g_mix, l)
        proj = _proj_in(h, w_in, l)

        oa, kc_p, vc_p = _attn_prompt(proj, sinks, gq, gk, tables_prompt, kc_p, vc_p, l)
        oa, kc_s, vc_s = _attn_sample(proj, oa, ck_all, cv_all, sinks, gq, gk, tables_sample, kc_s, vc_s, l)
        ob, st_p = _hgrn_prompt(proj, loglb, log1m, w_hg, st_p, l)
        ob, st_s = _hgrn_sample(proj, ob, state_hgrn, loglb, log1m, w_hg, st_s, l)
        oc, vn_s = _sgu(proj, w2, b2, w_sg, vn_s, l)
        merged = _merge(oa, ob, oc, proj, w_branch_a, w_branch_b, w_branch_c, l)
        x, h2 = _matmul_residual_norm(merged, w_out_bf16, l, x, g_ffn)
        act = _ffn_up(h2, w_ffn_up, l)
        x = _matmul_residual(act, w_ffn_down, l, x, TM_DOWN, TN_NARROW)


    y_prompt = x[:M_PROMPT].reshape(BATCH, SEQ, D_MODEL)
    y_sample = x[M_PROMPT:].reshape(DEC_BATCH, DEC_SEQ, D_MODEL)
    heads_p = (DEPTH, BATCH, WINDOW, A_KV_HEADS, A_HEAD_DIM)
    heads_s = (DEPTH, DEC_BATCH, WINDOW, A_KV_HEADS, A_HEAD_DIM)
    return (y_prompt, y_sample, kc_p.reshape(heads_p), vc_p.reshape(heads_p), st_p,
            kc_s.reshape(heads_s), vc_s.reshape(heads_s), st_s,
            vn_s.reshape(DEPTH, DEC_BATCH, DEC_SEQ, C_GROUPS, C_GROUP_DIM))
```

```python
import numpy as np
import jax
import jax.numpy as jnp
from jax import lax
from jax.experimental import pallas as pl
from jax.experimental.pallas import tpu as pltpu

D_MODEL = 2048
BATCH = 4
SEQ = 2048
DEPTH = 4
DEC_BATCH = 32
DEC_SEQ = 4
PAST_LEN = 16384

A_HEADS = 16
A_KV_HEADS = 4
A_HEAD_DIM = 64
A_GROUP = A_HEADS // A_KV_HEADS
WINDOW = 128
ROT_DIM = A_HEAD_DIM // 4
ROT_HALF = ROT_DIM // 2
ROPE_THETA = 500000.0
B_HEADS = 8
B_KEY_DIM = 128
B_VAL_DIM = 128
B_CHUNK = 16
C_GROUPS = 8
C_GROUP_DIM = 128
C_CHUNK = 128
A_Q_W = A_HEADS * A_HEAD_DIM
A_KV_W = A_KV_HEADS * A_HEAD_DIM
B_W = B_HEADS * B_KEY_DIM
C_W = C_GROUPS * C_GROUP_DIM
FFN_DIM = ((8 * D_MODEL + 3 * 256 - 1) // (3 * 256)) * 256
EPS = 1e-6

M_PROMPT = BATCH * SEQ
M_SAMPLE = DEC_BATCH * DEC_SEQ
M_ALL = M_PROMPT + M_SAMPLE

OFF_A = 0
W_A = A_Q_W + 2 * A_KV_W
OFF_B = OFF_A + W_A
W_B = 4 * B_W
OFF_C = OFF_B + W_B
W_C = 2 * C_W
OFF_G = OFF_C + W_C
W_G = 3 * D_MODEL
N_IN = OFF_G + W_G

LANES = 128
BLK = 128
TM = M_ALL // 4
TM_IN = M_ALL // 2
TM_DOWN = M_ALL // 8
TM_NORM = M_ALL // 16
TM_FULL = M_ALL // 16
TN = 512
TN_NARROW = 256
SGU_BLOCKS = 5
ATTN_SAMPLE_GROUP = 4
HGRN_SAMPLE_GROUP = 8
VMEM_LIMIT = 56 * 1024 * 1024

_BF16 = jnp.bfloat16
_F32 = jnp.float32
_NT = (((1,), (1,)), ((), ()))
_TN = (((0,), (0,)), ((), ()))


def _params(n_grid):
    return pltpu.CompilerParams(dimension_semantics=("arbitrary",) * n_grid,
                                vmem_limit_bytes=VMEM_LIMIT)


def _sigmoid(x):
    return 0.5 * jnp.tanh(0.5 * x) + 0.5


def _layer_vec(width, layer):
    return pl.BlockSpec((None, 1, width), lambda *_: (layer, 0, 0))


def _rmsnorm_kernel(x_ref, g_ref, o_ref):
    x = x_ref[...]
    y = x * lax.rsqrt(jnp.mean(x * x, axis=-1, keepdims=True) + EPS)
    o_ref[...] = (y * g_ref[...]).astype(_BF16)


def _rmsnorm(x, g, layer):
    m, d = x.shape
    return pl.pallas_call(
        _rmsnorm_kernel,
        grid=(m // TM_NORM,),
        in_specs=[pl.BlockSpec((TM_NORM, d), lambda i: (i, 0)), _layer_vec(d, layer)],
        out_specs=pl.BlockSpec((TM_NORM, d), lambda i: (i, 0)),
        out_shape=jax.ShapeDtypeStruct((m, d), _BF16),
        compiler_params=_params(1),
        name="rmsnorm",
    )(x, g)


def _join_norm_kernel(xp_ref, xs_ref, g_ref, x_ref, h_ref):
    is_sample = pl.program_id(0) == pl.num_programs(0) - 1
    x = jnp.where(is_sample, xs_ref[...], xp_ref[...])
    x_ref[...] = x
    y = x * lax.rsqrt(jnp.mean(x * x, axis=-1, keepdims=True) + EPS)
    h_ref[...] = (y * g_ref[...]).astype(_BF16)


def _join_norm(x_prompt, x_sample, g, layer):
    d = x_prompt.shape[1]
    nblk = M_ALL // BLK
    rows = pl.BlockSpec((BLK, d), lambda i: (i, 0))
    return pl.pallas_call(
        _join_norm_kernel,
        grid=(nblk,),
        in_specs=[pl.BlockSpec((BLK, d), lambda i: (jnp.minimum(i, nblk - 2), 0)),
                  pl.BlockSpec((BLK, d), lambda i: (0, 0)), _layer_vec(d, layer)],
        out_specs=[rows, rows],
        out_shape=[jax.ShapeDtypeStruct((M_ALL, d), _F32), jax.ShapeDtypeStruct((M_ALL, d), _BF16)],
        compiler_params=_params(1),
        name="join_norm",
    )(x_prompt, x_sample, g)


def _log_sigmoid(z):
    return jnp.minimum(z, 0.0) - jnp.log(1.0 + jnp.exp(-jnp.abs(z)))


def _mm_kernel(a_ref, w_ref, o_ref):
    o_ref[...] = jnp.dot(a_ref[...], w_ref[...].astype(_BF16), preferred_element_type=_F32)


def _proj_in(a, w, layer):
    m, k = a.shape
    return pl.pallas_call(
        _mm_kernel,
        grid=(m // TM_IN, N_IN // TN),
        in_specs=[pl.BlockSpec((TM_IN, k), lambda i, j: (i, 0), pipeline_mode=pl.Buffered(1)),
                  pl.BlockSpec((None, k, TN), lambda i, j: (layer, 0, j))],
        out_specs=pl.BlockSpec((TM_IN, TN), lambda i, j: (i, j)),
        out_shape=jax.ShapeDtypeStruct((m, N_IN), _F32),
        compiler_params=_params(2),
        name="proj_in",
    )(a, w)


class _WideRef:
    def __init__(self, low, high):
        self.parts = (low, high)

    def __getitem__(self, idx):
        if idx is Ellipsis:
            return jnp.concatenate([p[...] for p in self.parts], axis=1)
        rows, cols = idx
        part, start = divmod(cols.start, TN)
        return self.parts[part][rows, start:start + cols.stop - cols.start]


def _mm_res_kernel(a_ref, w_ref, r_ref, o_ref):
    o_ref[...] = r_ref[...] + jnp.dot(a_ref[...], w_ref[...].astype(_BF16),
                                      preferred_element_type=_F32)


def _matmul_residual(a, w, layer, r, tm, tn):
    m, k = a.shape
    n = w.shape[2]
    stream = pltpu.emit_pipeline(
        _mm_res_kernel,
        grid=(m // tm, n // tn),
        in_specs=[pl.BlockSpec((tm, k), lambda i, j: (i, 0)),
                  pl.BlockSpec((None, k, tn), lambda i, j: (layer, 0, j), pipeline_mode=pl.Buffered(3)),
                  pl.BlockSpec((tm, tn), lambda i, j: (i, j))],
        out_specs=[pl.BlockSpec((tm, tn), lambda i, j: (i, j))],
    )
    whole = pl.BlockSpec(memory_space=pl.ANY)
    return pl.pallas_call(
        lambda a_ref, w_ref, r_ref, o_ref: stream(a_ref, w_ref, r_ref, o_ref),
        in_specs=[whole, whole, whole],
        out_specs=whole,
        out_shape=jax.ShapeDtypeStruct((m, n), _F32),
        compiler_params=pltpu.CompilerParams(vmem_limit_bytes=VMEM_LIMIT),
        name="proj_residual",
    )(a, w, r)


def _mm_res_norm_kernel(a_ref, w_ref, r_ref, g_ref, o_ref, h_ref):
    x = r_ref[...] + jnp.dot(a_ref[...], w_ref[...], preferred_element_type=_F32)
    o_ref[...] = x
    y = x * lax.rsqrt(jnp.mean(x * x, axis=-1, keepdims=True) + EPS)
    h_ref[...] = (y * g_ref[...]).astype(_BF16)


def _matmul_residual_norm(a, w_bf16, layer, r, g):
    m, k = a.shape
    n = w_bf16.shape[2]
    rows = lambda width: pl.BlockSpec((TM_FULL, width), lambda i: (i, 0))
    return pl.pallas_call(
        _mm_res_norm_kernel,
        grid=(m // TM_FULL,),
        in_specs=[rows(k),
                  pl.BlockSpec((None, k, n), lambda i: (layer, 0, 0), pipeline_mode=pl.Buffered(1)),
                  rows(n), _layer_vec(n, layer)],
        out_specs=[rows(n), rows(n)],
        out_shape=[jax.ShapeDtypeStruct((m, n), _F32), jax.ShapeDtypeStruct((m, n), _BF16)],
        compiler_params=_params(1),
        name="proj_out_norm",
    )(a, w_bf16, r, g)


def _ffn_up_kernel(a_ref, wg_ref, wu_ref, o_ref):
    a = a_ref[...]
    g = jnp.dot(a, wg_ref[...].astype(_BF16), preferred_element_type=_F32)
    u = jnp.dot(a, wu_ref[...].astype(_BF16), preferred_element_type=_F32)
    o_ref[...] = (g * _sigmoid(g) * u).astype(_BF16)


def _ffn_up(a, w_up, layer):
    m, k = a.shape
    tn = TN_NARROW
    nj = FFN_DIM // tn
    return pl.pallas_call(
        _ffn_up_kernel,
        grid=(m // TM, nj),
        in_specs=[pl.BlockSpec((TM, k), lambda i, j: (i, 0)),
                  pl.BlockSpec((None, k, tn), lambda i, j: (layer, 0, j)),
                  pl.BlockSpec((None, k, tn), lambda i, j: (layer, 0, j + nj))],
        out_specs=pl.BlockSpec((TM, tn), lambda i, j: (i, j)),
        out_shape=jax.ShapeDtypeStruct((m, FFN_DIM), _BF16),
        compiler_params=_params(2),
        name="ffn_up",
    )(a, w_up, w_up)


def _merge_kernel(oa_ref, ob_ref, oc_ref, wa_ref, wb_ref, wc_ref, ga_ref, gb_ref, gc_ref, o_ref):
    ya = jnp.dot(oa_ref[...], wa_ref[...].astype(_BF16), preferred_element_type=_F32)
    yb = jnp.dot(ob_ref[...], wb_ref[...].astype(_BF16), preferred_element_type=_F32)
    yc = jnp.dot(oc_ref[...], wc_ref[...].astype(_BF16), preferred_element_type=_F32)
    merged = _sigmoid(ga_ref[...]) * ya + _sigmoid(gb_ref[...]) * yb + _sigmoid(gc_ref[...]) * yc
    o_ref[...] = merged.astype(_BF16)


def _merge(oa, ob, oc, proj, wa, wb, wc, layer):
    m, k = oa.shape
    tn = TN_NARROW
    nj = D_MODEL // tn
    branch = pl.BlockSpec((TM, k), lambda i, j: (i, 0))
    weight = pl.BlockSpec((None, k, tn), lambda i, j: (layer, 0, j))
    return pl.pallas_call(
        _merge_kernel,
        grid=(m // TM, nj),
        in_specs=[branch, branch, branch, weight, weight, weight,
                  *(pl.BlockSpec((TM, tn), lambda i, j, g=g: (i, OFF_G // tn + g * nj + j)) for g in range(3))],
        out_specs=pl.BlockSpec((TM, tn), lambda i, j: (i, j)),
        out_shape=jax.ShapeDtypeStruct((m, D_MODEL), _BF16),
        compiler_params=_params(2),
        name="merge",
    )(oa, ob, oc, wa, wb, wc, proj, proj, proj)


def _rope_tables(p0, rows):
    pos = (p0 + jnp.arange(rows, dtype=jnp.int32)).astype(_F32)
    inv_freq = jnp.power(jnp.float32(ROPE_THETA), -jnp.arange(ROT_HALF, dtype=_F32) / ROT_HALF)
    ang = pos[:, None] * inv_freq[None, :]
    cos, sin = jnp.cos(ang), jnp.sin(ang)
    rest = A_HEAD_DIM - ROT_DIM
    zeros = jnp.zeros((rows, ROT_HALF), _F32)
    pad = jnp.zeros((rows, rest), _F32)
    c = jnp.concatenate([cos, cos, jnp.ones((rows, rest), _F32)], axis=1)
    s1 = jnp.concatenate([-sin, zeros, pad], axis=1)
    s2 = jnp.concatenate([zeros, sin, pad], axis=1)
    return tuple(jnp.tile(t, (1, LANES // A_HEAD_DIM)) for t in (c, s1, s2))


def _rope(x, c, s1, s2):
    width = x.shape[1]
    reps = width // c.shape[1]
    c, s1, s2 = (jnp.concatenate([t] * reps, axis=1) for t in (c, s1, s2))
    ahead = pltpu.roll(x, width - ROT_HALF, axis=1)
    behind = pltpu.roll(x, ROT_HALF, axis=1)
    return x * c + ahead * s1 + behind * s2


def _head_mean_matrix():
    i = np.arange(LANES)
    same = (i[:, None] // A_HEAD_DIM) == (i[None, :] // A_HEAD_DIM)
    return jnp.asarray(same.astype(np.float32) / A_HEAD_DIM, dtype=_BF16)


def _qk_prep(x, gain, head_mean, c, s1, s2, scale):
    sq = x * x
    hi = sq.astype(_BF16)
    lo = (sq - hi.astype(_F32)).astype(_BF16)
    ms = jnp.concatenate(
        [jnp.dot(hi[:, l:l + LANES], head_mean, preferred_element_type=_F32)
         + jnp.dot(lo[:, l:l + LANES], head_mean, preferred_element_type=_F32)
         for l in range(0, x.shape[1], LANES)], axis=1)
    return _rope(x * gain, c, s1, s2) * (lax.rsqrt(ms + EPS) * scale)


def _low_half(shape):
    return lax.broadcasted_iota(jnp.int32, shape, 1) < A_HEAD_DIM


def _both_halves(col, half):
    low = _low_half(col.shape)
    sel = jnp.where(low if half == 0 else jnp.logical_not(low), col, 0.0)
    return sel + pltpu.roll(sel, A_HEAD_DIM, axis=1)


def _kv_pairs(k, v, h):
    pair, half = divmod(h, 2)
    lanes = slice(pair * LANES, (pair + 1) * LANES)
    return _both_halves(k[:, lanes], half), _both_halves(v[:, lanes], half)


def _mxu_tiles(k_pair, v_pair):
    v_pair = v_pair.astype(_BF16)
    return k_pair.astype(_BF16), jnp.concatenate([v_pair, jnp.ones(v_pair.shape, _BF16)], axis=1)


def _kv_tiles(k, v, h):
    return _mxu_tiles(*_kv_pairs(k, v, h))


def _query_rows(qn, h):
    low = _low_half((qn.shape[0], LANES))
    parts = []
    for g in range(A_GROUP):
        pair, half = divmod(h * A_GROUP + g, 2)
        keep = low if half == 0 else jnp.logical_not(low)
        parts.append(jnp.where(keep, qn[:, pair * LANES:(pair + 1) * LANES], 0.0))
    return jnp.concatenate(parts, axis=0).astype(_BF16)


def _sink_rows(sinks_ref, h, t):
    return jnp.concatenate([jnp.full((t, LANES), sinks_ref[0, h * A_GROUP + g], _F32)
                            for g in range(A_GROUP)], axis=0)


def _store_heads(o_ref, o, h, t):
    low = _low_half((t, LANES))
    for j in range(A_GROUP // 2):
        even = o[(2 * j) * t:(2 * j + 1) * t]
        odd = o[(2 * j + 1) * t:(2 * j + 2) * t]
        pair = (h * A_GROUP) // 2 + j
        o_ref[:, pair * LANES:(pair + 1) * LANES] = jnp.where(low, even, odd).astype(o_ref.dtype)


def _attn_prompt_kernel(sinks_ref, q_ref, kv_ref, c_ref, s1_ref, s2_ref, gq_ref, gk_ref, hm_ref,
                        kc_all_ref, vc_all_ref, o_ref, kc_ref, vc_ref, kprev, vprev):
    b = pl.program_id(0)
    n = pl.program_id(1)
    nb = pl.num_programs(1) - 1

    @pl.when(n == 0)
    def _():
        kprev[...] = jnp.zeros(kprev.shape, _F32)
        vprev[...] = jnp.zeros(vprev.shape, _F32)

    @pl.when(n < nb)
    def _():
        kv = kv_ref[...]
        v = kv[:, A_KV_W:]
        c, s1, s2 = c_ref[...], s1_ref[...], s2_ref[...]
        qn = _qk_prep(q_ref[...], gq_ref[...], hm_ref[...], c, s1, s2, A_HEAD_DIM ** -0.5)
        kn = _qk_prep(kv[:, :A_KV_W], gk_ref[...], hm_ref[...], c, s1, s2, 1.0)
        kc_ref[0] = kn
        vc_ref[0] = v

        rows = A_GROUP * BLK
        row = lax.broadcasted_iota(jnp.int32, (rows, WINDOW), 0) % BLK
        col = lax.broadcasted_iota(jnp.int32, (rows, WINDOW), 1)
        before = col > row
        has_prev = n > 0
        for h in range(A_KV_HEADS):
            k_pair, v_pair = _kv_pairs(kn, v, h)
            k_tile, v_tile = _mxu_tiles(k_pair, v_pair)
            k_before, v_before = _mxu_tiles(kprev[h], vprev[h])
            kprev[h] = k_pair
            vprev[h] = v_pair
            q4 = _query_rows(qn, h)
            s_prev = lax.dot_general(q4, k_before, _NT, preferred_element_type=_F32)
            s_cur = lax.dot_general(q4, k_tile, _NT, preferred_element_type=_F32)
            s = jnp.where(before, jnp.where(has_prev, s_prev, -jnp.inf), s_cur)
            sink = _sink_rows(sinks_ref, h, BLK)
            m = jnp.maximum(jnp.broadcast_to(jnp.max(s, axis=-1, keepdims=True), s.shape), sink)
            p = jnp.exp(s - m)
            acc = (jnp.dot(jnp.where(before, p, 0.0).astype(_BF16), v_before, preferred_element_type=_F32)
                   + jnp.dot(jnp.where(before, 0.0, p).astype(_BF16), v_tile, preferred_element_type=_F32))
            o = acc[:, :LANES] / (acc[:, LANES:] + jnp.exp(sink - m))
            _store_heads(o_ref, o, h, BLK)

    @pl.when((n == nb) & (b == pl.num_programs(0) - 1))
    def _():
        o_ref[...] = jnp.zeros(o_ref.shape, o_ref.dtype)


def _attn_prompt(a, sinks, gq, gk, tables, kc_all, vc_all, layer):
    nb = SEQ // BLK
    c, s1, s2 = tables
    block = lambda b, n: b * nb + jnp.minimum(n, nb - 1)
    table = pl.BlockSpec((BLK, LANES), lambda b, n: (jnp.minimum(n, nb - 1), 0))
    cache = pl.BlockSpec((None, 1, BLK, A_KV_W), lambda b, n: (layer, b, 0, 0))
    stacked = jax.ShapeDtypeStruct((DEPTH, BATCH, WINDOW, A_KV_W), _F32)
    const = lambda w: pl.BlockSpec((w, w), lambda b, n: (0, 0))

    def out_block(b, n):
        tail = (n == nb) & (b == BATCH - 1)
        return (jnp.where(tail, BATCH * nb, block(b, n)), 0)

    return pl.pallas_call(
        _attn_prompt_kernel,
        grid=(BATCH, nb + 1),
        in_specs=[pl.BlockSpec((None, 1, A_HEADS), lambda b, n: (layer, 0, 0), memory_space=pltpu.SMEM),
                  pl.BlockSpec((BLK, A_Q_W), lambda b, n: (block(b, n), 0)),
                  pl.BlockSpec((BLK, 2 * A_KV_W), lambda b, n: (block(b, n), A_Q_W // (2 * A_KV_W))),
                  table, table, table,
                  _layer_vec(A_Q_W, layer), _layer_vec(A_KV_W, layer), const(LANES),
                  pl.BlockSpec(memory_space=pl.ANY), pl.BlockSpec(memory_space=pl.ANY)],
        out_specs=[pl.BlockSpec((BLK, A_Q_W), out_block), cache, cache],
        out_shape=[jax.ShapeDtypeStruct((M_ALL, A_Q_W), _BF16), stacked, stacked],
        input_output_aliases={9: 1, 10: 2},
        scratch_shapes=[pltpu.VMEM((A_KV_HEADS, BLK, LANES), _F32),
                        pltpu.VMEM((A_KV_HEADS, BLK, LANES), _F32)],
        compiler_params=_params(2),
        name="attn_prompt",
    )(sinks, a, a, c, s1, s2, gq, gk, _head_mean_matrix(), kc_all, vc_all)


def _attn_sample_kernel(sinks_ref, qkv_ref, ck_ref, cv_ref, c_ref, s1_ref, s2_ref, gq_ref, gk_ref,
                        hm_ref, o_full_ref, kc_all_ref, vc_all_ref, o_ref, kc_ref, vc_ref,
                        kbuf, vbuf, knew, vnew):
    del o_full_ref, kc_all_ref, vc_all_ref
    grp = ATTN_SAMPLE_GROUP
    t = grp * DEC_SEQ
    qkv = qkv_ref[...]
    v = qkv[:, A_Q_W + A_KV_W:]
    c, s1, s2 = c_ref[...], s1_ref[...], s2_ref[...]
    qn = _qk_prep(qkv[:, :A_Q_W], gq_ref[...], hm_ref[...], c, s1, s2, A_HEAD_DIM ** -0.5)
    kn = _qk_prep(qkv[:, A_Q_W:A_Q_W + A_KV_W], gk_ref[...], hm_ref[...], c, s1, s2, 1.0)

    knew[...] = jnp.zeros(knew.shape, _F32)
    vnew[...] = jnp.zeros(vnew.shape, _F32)
    knew[0:t, :] = kn
    vnew[0:t, :] = v
    for s in range(grp):
        kbuf[s, 0:WINDOW, :] = ck_ref[s]
        vbuf[s, 0:WINDOW, :] = cv_ref[s]
        kbuf[s, WINDOW:, :] = knew[s * DEC_SEQ:s * DEC_SEQ + 8, :]
        vbuf[s, WINDOW:, :] = vnew[s * DEC_SEQ:s * DEC_SEQ + 8, :]
        kc_ref[s] = kbuf[s, DEC_SEQ:DEC_SEQ + WINDOW, :]
        vc_ref[s] = vbuf[s, DEC_SEQ:DEC_SEQ + WINDOW, :]

    rows = A_GROUP * t
    r_old = lax.broadcasted_iota(jnp.int32, (rows, grp * WINDOW), 0) % t
    c_old = lax.broadcasted_iota(jnp.int32, (rows, grp * WINDOW), 1)
    see_old = (c_old // WINDOW == r_old // DEC_SEQ) & (c_old % WINDOW > r_old % DEC_SEQ)
    r_new = lax.broadcasted_iota(jnp.int32, (rows, WINDOW), 0) % t
    c_new = lax.broadcasted_iota(jnp.int32, (rows, WINDOW), 1)
    see_new = (c_new < t) & (c_new // DEC_SEQ == r_new // DEC_SEQ) & (c_new % DEC_SEQ <= r_new % DEC_SEQ)
    pad_k = jnp.zeros((WINDOW - t, LANES), _BF16)
    pad_v = jnp.zeros((WINDOW - t, 2 * LANES), _BF16)
    for h in range(A_KV_HEADS):
        old = [_kv_tiles(ck_ref[s], cv_ref[s], h) for s in range(grp)]
        k_old = jnp.concatenate([kt for kt, _ in old], axis=0)
        v_old = jnp.concatenate([vt for _, vt in old], axis=0)
        k_new, v_new = _kv_tiles(kn, v, h)
        k_new = jnp.concatenate([k_new, pad_k], axis=0)
        v_new = jnp.concatenate([v_new, pad_v], axis=0)
        q4 = _query_rows(qn, h)
        s_old = jnp.where(see_old, lax.dot_general(q4, k_old, _NT, preferred_element_type=_F32), -jnp.inf)
        s_new = jnp.where(see_new, lax.dot_general(q4, k_new, _NT, preferred_element_type=_F32), -jnp.inf)
        sink = _sink_rows(sinks_ref, h, t)
        top = jnp.maximum(jnp.max(s_old, axis=-1, keepdims=True), jnp.max(s_new, axis=-1, keepdims=True))
        m = jnp.maximum(jnp.broadcast_to(top, sink.shape), sink)
        p_old = jnp.exp(s_old - jnp.concatenate([m] * grp, axis=1))
        p_new = jnp.exp(s_new - m)
        acc = (jnp.dot(p_old.astype(_BF16), v_old, preferred_element_type=_F32)
               + jnp.dot(p_new.astype(_BF16), v_new, preferred_element_type=_F32))
        o = acc[:, :LANES] / (acc[:, LANES:] + jnp.exp(sink - m))
        _store_heads(o_ref, o, h, t)


def _attn_sample(a, o_full, cache_k, cache_v, sinks, gq, gk, tables, kc_all, vc_all, layer):
    c, s1, s2 = tables
    grp = ATTN_SAMPLE_GROUP
    t = grp * DEC_SEQ
    first = M_PROMPT // t
    table = pl.BlockSpec((t, LANES), lambda b: (0, 0))
    cache_in = pl.BlockSpec((None, grp, WINDOW, A_KV_W), lambda b: (layer, b, 0, 0))
    cache_out = pl.BlockSpec((None, grp, WINDOW, A_KV_W), lambda b: (layer, b, 0, 0))
    stacked = jax.ShapeDtypeStruct((DEPTH, DEC_BATCH, WINDOW, A_KV_W), _F32)
    any_space = pl.BlockSpec(memory_space=pl.ANY)
    const = lambda w: pl.BlockSpec((w, w), lambda b: (0, 0))
    return pl.pallas_call(
        _attn_sample_kernel,
        grid=(DEC_BATCH // grp,),
        in_specs=[pl.BlockSpec((None, 1, A_HEADS), lambda b: (layer, 0, 0), memory_space=pltpu.SMEM),
                  pl.BlockSpec((t, W_A), lambda b: (first + b, 0)),
                  cache_in, cache_in, table, table, table,
                  _layer_vec(A_Q_W, layer), _layer_vec(A_KV_W, layer), const(LANES),
                  any_space, any_space, any_space],
        out_specs=[pl.BlockSpec((t, A_Q_W), lambda b: (first + b, 0)), cache_out, cache_out],
        out_shape=[jax.ShapeDtypeStruct((M_ALL, A_Q_W), _BF16), stacked, stacked],
        scratch_shapes=[pltpu.VMEM((grp, WINDOW + 8, A_KV_W), _F32),
                        pltpu.VMEM((grp, WINDOW + 8, A_KV_W), _F32),
                        pltpu.VMEM((t + 8, A_KV_W), _F32), pltpu.VMEM((t + 8, A_KV_W), _F32)],
        input_output_aliases={10: 0, 11: 1, 12: 2},
        compiler_params=_params(1),
        name="attn_sample",
    )(sinks, a, cache_k, cache_v, c, s1, s2, gq, gk, _head_mean_matrix(),
      o_full, kc_all, vc_all)


def _lower_bound_kernel(logits_ref, loglb_ref, log1m_ref):
    x = logits_ref[...]
    e = jnp.exp(x - jnp.max(x, axis=0, keepdims=True))
    sm = e / jnp.sum(e, axis=0, keepdims=True)
    acc = sm[0:1]
    rows = [acc]
    for l in range(1, DEPTH):
        acc = acc + sm[l:l + 1]
        rows.append(acc)
    for l in range(DEPTH):
        lb = rows[l] - rows[0]
        loglb_ref[l:l + 1, :] = jnp.log(lb)
        log1m_ref[l:l + 1, :] = jnp.log1p(-lb)


def _lower_bounds(lb_logits):
    shape = jax.ShapeDtypeStruct(lb_logits.shape, _F32)
    return pl.pallas_call(_lower_bound_kernel, out_shape=[shape, shape], name="hgrn_lower_bounds")(lb_logits)


def _chunk_matrices(rows, chunk):
    t = np.arange(rows)[:, None]
    s = np.arange(rows)[None, :]
    same = (t // chunk) == (s // chunk)
    tri = (same & (s <= t)).astype(np.float32)
    ref = (same & ((s % chunk) <= chunk // 2)).astype(np.float32)
    last = same.astype(np.float32)
    return jnp.asarray(np.concatenate([tri, tri - ref, last - tri], axis=0), dtype=_BF16)


def _split2(x):
    hi = x.astype(_BF16)
    return hi, (x - hi.astype(_F32)).astype(_BF16)


def _hgrn_gates(q, z, loglb, log1m, lt, rows):
    b = log1m + _log_sigmoid(z)
    log_f = jnp.maximum(loglb, b) + jnp.log(1.0 + jnp.exp(-jnp.abs(loglb - b)))
    kk = -jnp.tanh(0.5 * log_f) * (jnp.exp(log_f) + 1.0)
    hi, lo = _split2(log_f)
    cums = jnp.dot(lt, hi, preferred_element_type=_F32) + jnp.dot(lt, lo, preferred_element_type=_F32)
    cum = cums[0:rows]
    cum_ref = cums[rows:2 * rows]
    cum_end = cums[2 * rows:3 * rows]
    e_cum = jnp.exp(cum)
    q_intra = q * jnp.exp(cum_ref)
    k_intra = kk * jnp.exp(-cum_ref)
    q_inter = q * e_cum
    k_state = kk * jnp.exp(cum_end)
    return q_intra, k_intra, q_inter, k_state, e_cum


def _hgrn_finish(o, g, w):
    y = o * lax.rsqrt(jnp.mean(o * o, axis=-1, keepdims=True) + EPS) * w
    return (y * (g * _sigmoid(g))).astype(_BF16)


def _hgrn_intra(q_intra, k_intra, v, causal):
    att = lax.dot_general(q_intra.astype(_BF16), k_intra.astype(_BF16), _NT, preferred_element_type=_F32)
    att = jnp.where(causal, att, 0.0)
    return jnp.dot(att.astype(_BF16), v, preferred_element_type=_F32)


def _hgrn_prompt_kernel(q_lo, q_hi, f_lo, f_hi, i_lo, i_hi, g_lo, g_hi, loglb_ref, log1m_ref, lt_ref, w_ref,
                        s_all_ref, o_ref, s_ref, st, hand):
    q_ref, f_ref, i_ref, g_ref = (_WideRef(q_lo, q_hi), _WideRef(f_lo, f_hi), _WideRef(i_lo, i_hi),
                                  _WideRef(g_lo, g_hi))
    b = pl.program_id(0)
    n = pl.program_id(1)
    nb = pl.num_programs(1) - 2

    @pl.when(n == 0)
    def _():
        st[...] = jnp.zeros(st.shape, _F32)
        hand[...] = jnp.zeros(hand.shape, _F32)

    def step(slot, done):
        row = lax.broadcasted_iota(jnp.int32, (BLK, BLK), 0)
        col = lax.broadcasted_iota(jnp.int32, (BLK, BLK), 1)
        causal = (row // B_CHUNK == col // B_CHUNK) & (col <= row)
        chunks = [slice(c * B_CHUNK, (c + 1) * B_CHUNK) for c in range(BLK // B_CHUNK)]
        w = w_ref[...]
        lt = lt_ref[...]
        heads = [slice(h * B_KEY_DIM, (h + 1) * B_KEY_DIM) for h in range(B_HEADS)]
        intra, updates = [], []
        for lanes in heads:
            for j, t in enumerate(_hgrn_gates(q_ref[:, lanes], f_ref[:, lanes], loglb_ref[:, lanes],
                                              log1m_ref[:, lanes], lt, BLK)):
                hand[slot, j, :, lanes] = t
            v = i_ref[:, lanes].astype(_BF16)
            ks = hand[done, 3, :, lanes].astype(_BF16)
            intra.append(_hgrn_intra(hand[done, 0, :, lanes], hand[done, 1, :, lanes], v, causal))
            updates.append([lax.dot_general(v[rows], ks[rows], _TN, preferred_element_type=_F32)
                            for rows in chunks])
        before = []
        for h, lanes in enumerate(heads):
            state_t = st[h]
            seen = []
            for rows, update in zip(chunks, updates[h]):
                seen.append(state_t.astype(_BF16))
                state_t = state_t * hand[done, 4, rows.stop - 1:rows.stop, lanes] + update
            st[h] = state_t
            before.append(seen)
        for h, lanes in enumerate(heads):
            qi = hand[done, 2, :, lanes].astype(_BF16)
            o_inter = [lax.dot_general(qi[rows], s_t, _NT, preferred_element_type=_F32)
                       for rows, s_t in zip(chunks, before[h])]
            o_ref[:, lanes] = _hgrn_finish(intra[h] + jnp.concatenate(o_inter, axis=0), g_ref[:, lanes], w)

    for parity in range(2):
        pl.when((n <= nb) & (n % 2 == parity))(lambda parity=parity: step(parity, 1 - parity))

    @pl.when(n == nb)
    def _():
        for h in range(B_HEADS):
            s_ref[0, h] = st[h].T

    @pl.when((n == nb + 1) & (b == pl.num_programs(0) - 1))
    def _():
        o_ref[...] = jnp.zeros(o_ref.shape, o_ref.dtype)


def _gate_columns(j):
    first = (OFF_B + j * B_W) // TN
    return (first, first + 1)


def _hgrn_prompt(proj, loglb, log1m, w, s_all, layer):
    nb = SEQ // BLK

    def ahead(j):
        return [pl.BlockSpec((BLK, TN), lambda b, n, c=c: (b * nb + jnp.minimum(n, nb - 1), c))
                for c in _gate_columns(j)]

    def behind(j):
        return [pl.BlockSpec((BLK, TN), lambda b, n, c=c: (b * nb + jnp.clip(n - 1, 0, nb - 1), c))
                for c in _gate_columns(j)]

    def out_block(b, n):
        tail = (n == nb + 1) & (b == BATCH - 1)
        return (jnp.where(tail, BATCH * nb, b * nb + jnp.clip(n - 1, 0, nb - 1)), 0)

    return pl.pallas_call(
        _hgrn_prompt_kernel,
        grid=(BATCH, nb + 2),
        in_specs=[*ahead(0), *ahead(1), *behind(2), *behind(3), _layer_vec(B_W, layer), _layer_vec(B_W, layer),
                  pl.BlockSpec((3 * BLK, BLK), lambda b, n: (0, 0)),
                  _layer_vec(B_VAL_DIM, layer), pl.BlockSpec(memory_space=pl.ANY)],
        out_specs=[pl.BlockSpec((BLK, B_W), out_block),
                   pl.BlockSpec((None, 1, B_HEADS, B_KEY_DIM, B_VAL_DIM), lambda b, n: (layer, b, 0, 0, 0))],
        out_shape=[jax.ShapeDtypeStruct((M_ALL, B_W), _BF16),
                   jax.ShapeDtypeStruct((DEPTH, BATCH, B_HEADS, B_KEY_DIM, B_VAL_DIM), _F32)],
        scratch_shapes=[pltpu.VMEM((B_HEADS, B_VAL_DIM, B_KEY_DIM), _F32),
                        pltpu.VMEM((2, 5, BLK, B_W), _F32)],
        input_output_aliases={12: 1},
        compiler_params=_params(2),
        name="hgrn_prompt",
    )(*(proj,) * 8, loglb, log1m, _chunk_matrices(BLK, B_CHUNK), w, s_all)


def _hgrn_sample_kernel(q_lo, q_hi, f_lo, f_hi, i_lo, i_hi, g_lo, g_hi, loglb_ref, log1m_ref, lt_ref, w_ref,
                        s0_ref, o_full_ref, s_all_ref, o_ref, s_ref):
    q_ref, f_ref, i_ref, g_ref = (_WideRef(q_lo, q_hi), _WideRef(f_lo, f_hi), _WideRef(i_lo, i_hi),
                                  _WideRef(g_lo, g_hi))
    del o_full_ref, s_all_ref
    rows = HGRN_SAMPLE_GROUP * DEC_SEQ

    def padded(ref):
        return jnp.concatenate([ref[...], jnp.zeros((BLK - rows, B_W), _F32)], axis=0)

    v_all = padded(i_ref)
    g_all = padded(g_ref)
    q_intra, k_intra, q_inter, k_state, e_cum = _hgrn_gates(
        padded(q_ref), padded(f_ref), loglb_ref[...], log1m_ref[...], lt_ref[...], BLK)
    row = lax.broadcasted_iota(jnp.int32, (BLK, BLK), 0)
    col = lax.broadcasted_iota(jnp.int32, (BLK, BLK), 1)
    causal = (row // DEC_SEQ == col // DEC_SEQ) & (col <= row)
    w = w_ref[...]
    for h in range(B_HEADS):
        lanes = slice(h * B_KEY_DIM, (h + 1) * B_KEY_DIM)
        v = v_all[:, lanes].astype(_BF16)
        o = _hgrn_intra(q_intra[:, lanes], k_intra[:, lanes], v, causal)
        qi = q_inter[:, lanes].astype(_BF16)
        ec_t = e_cum[:, lanes].T
        ks_t = k_state[:, lanes].T
        for s in range(HGRN_SAMPLE_GROUP):
            state = s0_ref[s, h]
            o_s = jnp.dot(qi, state.astype(_BF16), preferred_element_type=_F32)
            o = o + jnp.where(row // DEC_SEQ == s, o_s, 0.0)
            ks_seq = jnp.where(col // DEC_SEQ == s, ks_t, 0.0).astype(_BF16)
            update = jnp.dot(ks_seq, v, preferred_element_type=_F32)
            decay = ec_t[:, (s + 1) * DEC_SEQ - 1:(s + 1) * DEC_SEQ]
            s_ref[s, h] = state * decay + update
        o_ref[:, lanes] = _hgrn_finish(o, g_all[:, lanes], w)[0:rows]


def _hgrn_sample(proj, o_full, state, loglb, log1m, w, s_all, layer):
    rows = HGRN_SAMPLE_GROUP * DEC_SEQ
    first = M_PROMPT // rows
    gate = lambda j: [pl.BlockSpec((rows, TN), lambda s, c=c: (first + s, c)) for c in _gate_columns(j)]
    state_shape = (HGRN_SAMPLE_GROUP, B_HEADS, B_KEY_DIM, B_VAL_DIM)
    return pl.pallas_call(
        _hgrn_sample_kernel,
        grid=(DEC_BATCH // HGRN_SAMPLE_GROUP,),
        in_specs=[*gate(0), *gate(1), *gate(2), *gate(3), _layer_vec(B_W, layer), _layer_vec(B_W, layer),
                  pl.BlockSpec((3 * BLK, BLK), lambda s: (0, 0)),
                  _layer_vec(B_VAL_DIM, layer),
                  pl.BlockSpec((None,) + state_shape, lambda s: (layer, s, 0, 0, 0)),
                  pl.BlockSpec(memory_space=pl.ANY), pl.BlockSpec(memory_space=pl.ANY)],
        out_specs=[pl.BlockSpec((rows, B_W), lambda s: (first + s, 0)),
                   pl.BlockSpec((None,) + state_shape, lambda s: (layer, s, 0, 0, 0))],
        out_shape=[jax.ShapeDtypeStruct((M_ALL, B_W), _BF16),
                   jax.ShapeDtypeStruct(state.shape, _F32)],
        input_output_aliases={13: 0, 14: 1},
        compiler_params=_params(1),
        name="hgrn_sample",
    )(*(proj,) * 8, loglb, log1m, _chunk_matrices(BLK, DEC_SEQ), w, state, o_full, s_all)


def _sgu_kernel(u_lo, u_hi, v_lo, v_hi, w_ref, b_ref, g_ref, vn_all_ref, o_ref, vn_ref):
    u_ref, v_ref = _WideRef(u_lo, u_hi), _WideRef(v_lo, v_hi)
    last_step = pl.program_id(0) == pl.num_programs(0) - 1
    row = lax.broadcasted_iota(jnp.int32, (BLK, BLK), 0)
    col = lax.broadcasted_iota(jnp.int32, (BLK, BLK), 1)
    gain = g_ref[...]
    for sub in range(SGU_BLOCKS):
        rows = slice(sub * BLK, (sub + 1) * BLK)
        maybe_sample = sub == SGU_BLOCKS - 1
        if maybe_sample:
            causal = (col <= row) & jnp.logical_or(jnp.logical_not(last_step), row // DEC_SEQ == col // DEC_SEQ)
            bias = jnp.where(last_step, b_ref[1], b_ref[0])
        else:
            causal = col <= row
            bias = b_ref[0]
        for g in range(C_GROUPS):
            lanes = slice(g * C_GROUP_DIM, (g + 1) * C_GROUP_DIM)
            v = v_ref[rows, lanes]
            vn = v * lax.rsqrt(jnp.mean(v * v, axis=-1, keepdims=True) + EPS) * gain
            w = jnp.where(last_step, w_ref[1, g], w_ref[0, g]) if maybe_sample else w_ref[0, g]
            w = jnp.where(causal, w, 0.0).astype(_BF16)
            z = jnp.dot(w, vn.astype(_BF16), preferred_element_type=_F32) + bias[:, g:g + 1]
            o_ref[rows, lanes] = (u_ref[rows, lanes] * z).astype(_BF16)
            if maybe_sample:
                vn_ref[:, lanes] = vn


def _sgu(proj, w2, b2, gain, vn_all, layer):
    rows = SGU_BLOCKS * BLK
    return pl.pallas_call(
        _sgu_kernel,
        grid=(M_ALL // rows,),
        in_specs=[*(pl.BlockSpec((rows, TN), lambda i, c=c: (i, OFF_C // TN + c)) for c in range(4)),
                  pl.BlockSpec((None, 2, C_GROUPS, C_CHUNK, C_CHUNK), lambda i: (layer, 0, 0, 0, 0)),
                  pl.BlockSpec((None, 2, C_CHUNK, C_GROUPS), lambda i: (layer, 0, 0, 0)),
                  _layer_vec(C_GROUP_DIM, layer), pl.BlockSpec(memory_space=pl.ANY)],
        out_specs=[pl.BlockSpec((rows, C_W), lambda i: (i, 0)),
                   pl.BlockSpec((None, BLK, C_W), lambda i: (layer, 0, 0))],
        out_shape=[jax.ShapeDtypeStruct((M_ALL, C_W), _BF16),
                   jax.ShapeDtypeStruct((DEPTH, M_SAMPLE, C_W), _F32)],
        input_output_aliases={7: 1},
        compiler_params=_params(1),
        name="sgu",
    )(*(proj,) * 4, w2, b2, gain, vn_all)


def _sgu_params(w_spatial, b_spatial):
    reps = BLK // DEC_SEQ
    w_sample = jnp.tile(w_spatial[:, :, :DEC_SEQ, :DEC_SEQ], (1, 1, reps, reps))
    b_sample = jnp.tile(b_spatial[:, :, :DEC_SEQ], (1, 1, reps))
    w2 = jnp.stack([w_spatial, w_sample], axis=1)
    b2 = jnp.stack([jnp.swapaxes(b_spatial, 1, 2), jnp.swapaxes(b_sample, 1, 2)], axis=1)
    return w2, b2


def kernel(x_prompt, x_sample, cache_k, cache_v, state_hgrn, norm_mix, w_in, q_norm, k_norm, sinks,
           lb_logits, hgrn_out_norm, sgu_v_norm, w_spatial, b_spatial, w_branch_a, w_branch_b,
           w_branch_c, w_out, norm_ffn, w_ffn_up, w_ffn_down):
    loglb, log1m = _lower_bounds(lb_logits)
    loglb = loglb.reshape(DEPTH, 1, B_W)
    log1m = log1m.reshape(DEPTH, 1, B_W)
    tables_prompt = _rope_tables(0, SEQ)
    tables_sample = tuple(jnp.tile(t, (ATTN_SAMPLE_GROUP, 1)) for t in _rope_tables(PAST_LEN, DEC_SEQ))
    gq = jnp.tile(q_norm, (1, A_HEADS)).reshape(DEPTH, 1, A_Q_W)
    gk = jnp.tile(k_norm, (1, A_KV_HEADS)).reshape(DEPTH, 1, A_KV_W)
    w_hg = hgrn_out_norm.reshape(DEPTH, 1, B_VAL_DIM)
    w_sg = sgu_v_norm.reshape(DEPTH, 1, C_GROUP_DIM)
    g_mix = norm_mix.reshape(DEPTH, 1, D_MODEL)
    g_ffn = norm_ffn.reshape(DEPTH, 1, D_MODEL)
    sinks = sinks.reshape(DEPTH, 1, A_HEADS)
    w_out_bf16 = w_out.astype(_BF16)
    w2, b2 = _sgu_params(w_spatial, b_spatial)
    ck_all = cache_k.reshape(DEPTH, DEC_BATCH, WINDOW, A_KV_W)
    cv_all = cache_v.reshape(DEPTH, DEC_BATCH, WINDOW, A_KV_W)

    kc_p = jnp.zeros((DEPTH, BATCH, WINDOW, A_KV_W), _F32)
    vc_p = jnp.zeros((DEPTH, BATCH, WINDOW, A_KV_W), _F32)
    kc_s = jnp.zeros((DEPTH, DEC_BATCH, WINDOW, A_KV_W), _F32)
    vc_s = jnp.zeros((DEPTH, DEC_BATCH, WINDOW, A_KV_W), _F32)
    st_p = jnp.zeros((DEPTH, BATCH, B_HEADS, B_KEY_DIM, B_VAL_DIM), _F32)
    st_s = jnp.zeros(state_hgrn.shape, _F32)
    vn_s = jnp.zeros((DEPTH, M_SAMPLE, C_W), _F32)
    for l in range(DEPTH):
        if l == 0:
            x, h = _join_norm(x_prompt.reshape(M_PROMPT, D_MODEL), x_sample.reshape(M_SAMPLE, D_MODEL), g_mix, l)
        else:
            h = _rmsnorm(x, g_mix, l)
        proj = _proj_in(h, w_in, l)

        oa, kc_p, vc_p = _attn_prompt(proj, sinks, gq, gk, tables_prompt, kc_p, vc_p, l)
        oa, kc_s, vc_s = _attn_sample(proj, oa, ck_all, cv_all, sinks, gq, gk, tables_sample, kc_s, vc_s, l)
        ob, st_p = _hgrn_prompt(proj, loglb, log1m, w_hg, st_p, l)
        ob, st_s = _hgrn_sample(proj, ob, state_hgrn, loglb, log1m, w_hg, st_s, l)
        oc, vn_s = _sgu(proj, w2, b2, w_sg, vn_s, l)
        merged = _merge(oa, ob, oc, proj, w_branch_a, w_branch_b, w_branch_c, l)
        x, h2 = _matmul_residual_norm(merged, w_out_bf16, l, x, g_ffn)
        act = _ffn_up(h2, w_ffn_up, l)
        x = _matmul_residual(act, w_ffn_down, l, x, TM_DOWN, TN_NARROW)


    y_prompt = x[:M_PROMPT].reshape(BATCH, SEQ, D_MODEL)
    y_sample = x[M_PROMPT:].reshape(DEC_BATCH, DEC_SEQ, D_MODEL)
    heads_p = (DEPTH, BATCH, WINDOW, A_KV_HEADS, A_HEAD_DIM)
    heads_s = (DEPTH, DEC_BATCH, WINDOW, A_KV_HEADS, A_HEAD_DIM)
    return (y_prompt, y_sample, kc_p.reshape(heads_p), vc_p.reshape(heads_p), st_p,
            kc_s.reshape(heads_s), vc_s.reshape(heads_s), st_s,
            vn_s.reshape(DEPTH, DEC_BATCH, DEC_SEQ, C_GROUPS, C_GROUP_DIM))
```
